```python
import jax, jax.numpy as jnp
from jax import lax
import numpy as np

D_MODEL = 1024
BATCH = 8
SEQ = 4096
DEPTH = 2

GRID_W = 64
CTX_LEN = 256
N_MIXERS = 2
N_LAYERS_A = (DEPTH + 1) // 2
N_LAYERS_B = DEPTH // 2
N_MOD = 6

NA_HEADS = 16
NA_HEAD_DIM = D_MODEL // NA_HEADS
WIN_H = 8
WIN_W = 16

LRU_WIDTH = D_MODEL
LRU_BLOCKS = 4
LRU_BLOCK_W = LRU_WIDTH // LRU_BLOCKS
CONV_W = 4
LRU_C = 8.0

N_GROUPS = 4
EXPERTS_PER_GROUP = 8
N_EXPERTS = N_GROUPS * EXPERTS_PER_GROUP
TOP_K = 2
EXPERT_FF = 512
MOE_BLOCK = 128
EPS = 1e-6

kernel_name = "hybrid_na_rglru_hmoe_dit"


def rmsnorm(x, g):
    x32 = x.astype(jnp.float32)
    y = x32 * lax.rsqrt(jnp.mean(x32 * x32, axis=-1, keepdims=True) + EPS)
    return (y * g.astype(jnp.float32)).astype(x.dtype)


def modulate(x, shift, scale):
    return x * (1 + scale) + shift


def neighbourhood_attention(hl, hc, w_qkv, b_qkv, rpb, w_o, b_o, with_ctx_out):
    B_, L, D = hl.shape
    C = hc.shape[1]
    rows = L // GRID_W
    kh = min(WIN_H, rows)
    kw = WIN_W
    qkv = (hl @ w_qkv + b_qkv).reshape(B_, rows, GRID_W, 3, NA_HEADS, NA_HEAD_DIM)
    q, k, v = qkv[:, :, :, 0], qkv[:, :, :, 1], qkv[:, :, :, 2]
    kv_c = (hc @ w_qkv[:, D:] + b_qkv[D:]).reshape(B_, C, 2, NA_HEADS, NA_HEAD_DIM)
    kc, vc = kv_c[:, :, 0], kv_c[:, :, 1]
    scale = NA_HEAD_DIM ** -0.5

    cols = jnp.arange(GRID_W)
    col_start = jnp.clip(cols - kw // 2, 0, GRID_W - kw)
    col_idx = col_start[:, None] + jnp.arange(kw)[None, :]
    dci = col_idx - cols[:, None] + (WIN_W - 1)

    def row_block(r):
        rs = jnp.clip(r - kh // 2, 0, rows - kh)
        q_r = lax.dynamic_index_in_dim(q, r, axis=1, keepdims=False)
        k_win = lax.dynamic_slice_in_dim(k, rs, kh, axis=1)[:, :, col_idx]
        v_win = lax.dynamic_slice_in_dim(v, rs, kh, axis=1)[:, :, col_idx]
        dri = rs + jnp.arange(kh) - r + (WIN_H - 1)
        bias = rpb[:, dri[:, None, None], dci[None, :, :]]
        s_win = jnp.einsum('bqhd,biqjhd->bhqij', q_r, k_win) * scale
        s_win = s_win.astype(jnp.float32) + jnp.transpose(bias, (0, 2, 1, 3))[None].astype(jnp.float32)
        s_ctx = (jnp.einsum('bqhd,bchd->bhqc', q_r, kc) * scale).astype(jnp.float32)
        s = jnp.concatenate([s_win.reshape(B_, NA_HEADS, GRID_W, kh * kw), s_ctx], axis=-1)
        p = jax.nn.softmax(s, axis=-1).astype(v.dtype)
        p_win = p[..., :kh * kw].reshape(B_, NA_HEADS, GRID_W, kh, kw)
        p_ctx = p[..., kh * kw:]
        return (jnp.einsum('bhqij,biqjhd->bqhd', p_win, v_win)
                + jnp.einsum('bhqc,bchd->bqhd', p_ctx, vc))

    o = lax.map(row_block, jnp.arange(rows))
    o = jnp.moveaxis(o, 0, 1).reshape(B_, L, D)
    out_l = o @ w_o + b_o
    out_c = None
    if with_ctx_out:
        qc = (hc @ w_qkv[:, :D] + b_qkv[:D]).reshape(B_, C, NA_HEADS, NA_HEAD_DIM)
        sc = (jnp.einsum('bqhd,bkhd->bhqk', qc, kc) * scale).astype(jnp.float32)
        pc = jax.nn.softmax(sc, axis=-1).astype(vc.dtype)
        oc = jnp.einsum('bhqk,bkhd->bqhd', pc, vc).reshape(B_, C, D)
        out_c = oc @ w_o + b_o
    return out_l, out_c


def centred_dwconv(u, w, b):
    T = u.shape[1]
    left = CONV_W // 2
    right = CONV_W - 1 - left
    up = jnp.pad(u, ((0, 0), (left, right), (0, 0)))
    out = up[:, 0:T] * w[0]
    for kk in range(1, CONV_W):
        out = out + up[:, kk:kk + T] * w[kk]
    return out + b


def rglru_coeffs(u, w_gate, b_gate, lam):
    B_, T, _ = u.shape
    ub = u.reshape(B_, T, LRU_BLOCKS, LRU_BLOCK_W)
    g = jnp.einsum('btnk,gnkj->gbtnj', ub, w_gate).reshape(2, B_, T, LRU_WIDTH) + b_gate[:, None, None, :]
    g = g.astype(jnp.float32)
    r = jax.nn.sigmoid(g[0])
    ig = jax.nn.sigmoid(g[1])
    log_a = LRU_C * r * jax.nn.log_sigmoid(lam.astype(jnp.float32))
    a = jnp.exp(log_a)
    b = jnp.sqrt(-jnp.expm1(2.0 * log_a)) * (ig * u.astype(jnp.float32))
    return a, b


def linear_scan(a, b, h0, reverse):
    if h0 is not None:
        if reverse:
            b = b.at[:, -1].add(a[:, -1] * h0)
        else:
            b = b.at[:, 0].add(a[:, 0] * h0)

    def combine(lhs, rhs):
        a1, b1 = lhs
        a2, b2 = rhs
        return a1 * a2, a2 * b1 + b2

    _, h = lax.associative_scan(combine, (a, b), reverse=reverse, axis=1)
    return h


def rglru_mixer(hl, hc, w_in, b_in, conv_w, conv_b, w_gate, b_gate, lam, w_o, b_o, with_ctx_out):
    yl, ul = jnp.split(hl @ w_in + b_in, 2, axis=-1)
    uc = hc @ w_in[:, LRU_WIDTH:] + b_in[LRU_WIDTH:]
    ul = centred_dwconv(ul, conv_w, conv_b)
    uc = centred_dwconv(uc, conv_w, conv_b)
    h_l_sum = None
    h_c_sum = None
    for d in range(2):
        rev = d == 1
        ac, bc = rglru_coeffs(uc, w_gate[d], b_gate[d], lam[d])
        h_c = linear_scan(ac, bc, None, rev)
        h_end = h_c[:, 0] if rev else h_c[:, -1]
        al, bl = rglru_coeffs(ul, w_gate[d], b_gate[d], lam[d])
        h_l = linear_scan(al, bl, h_end, rev)
        h_l_sum = h_l if h_l_sum is None else h_l_sum + h_l
        if with_ctx_out:
            h_c_sum = h_c if h_c_sum is None else h_c_sum + h_c
    out_l = (jax.nn.gelu(yl) * h_l_sum.astype(yl.dtype)) @ w_o + b_o
    out_c = None
    if with_ctx_out:
        yc = hc @ w_in[:, :LRU_WIDTH] + b_in[:LRU_WIDTH]
        out_c = (jax.nn.gelu(yc) * h_c_sum.astype(yc.dtype)) @ w_o + b_o
    return out_l, out_c


def hier_moe(xt, w_group, b_group, w_expert, b_expert, w1, w3, w2):
    N, D = xt.shape
    glog = (xt @ w_group + b_group).astype(jnp.float32)
    gprob = jax.nn.softmax(glog, axis=-1)
    _, g_sel = lax.top_k(glog, 1)
    g_w = jnp.take_along_axis(gprob, g_sel, axis=1)
    elog = (jnp.einsum('nd,gde->nge', xt, w_expert) + b_expert[None]).astype(jnp.float32)
    elog_g = jnp.take_along_axis(elog, g_sel[:, :, None], axis=1)[:, 0]
    e_val, e_idx = lax.top_k(elog_g, TOP_K)
    weights = g_w * jax.nn.softmax(e_val, axis=-1)
    eids = g_sel * EXPERTS_PER_GROUP + e_idx

    A = N * TOP_K
    flat_e = eids.reshape(A)
    flat_w = weights.reshape(A)
    order = jnp.argsort(flat_e)
    sorted_e = flat_e[order]
    tok = order // TOP_K
    counts = jax.ops.segment_sum(jnp.ones((A,), jnp.int32), flat_e, num_segments=N_EXPERTS)
    start = jnp.cumsum(counts) - counts
    padded = (counts + MOE_BLOCK - 1) // MOE_BLOCK * MOE_BLOCK
    pend = jnp.cumsum(padded)
    pstart = pend - padded
    dest = pstart[sorted_e] + (jnp.arange(A) - start[sorted_e])
    nb = (A + N_EXPERTS * (MOE_BLOCK - 1) + MOE_BLOCK - 1) // MOE_BLOCK
    buf = jnp.zeros((nb * MOE_BLOCK, D), xt.dtype).at[dest].set(xt[tok])
    block_e = jnp.clip(jnp.searchsorted(pend, jnp.arange(nb) * MOE_BLOCK, side='right'), 0, N_EXPERTS - 1)

    def expert_block(args):
        xb, e = args
        hdn = jax.nn.silu(xb @ w1[e]) * (xb @ w3[e])
        return hdn @ w2[e]

    ys = lax.map(expert_block, (buf.reshape(nb, MOE_BLOCK, D), block_e)).reshape(nb * MOE_BLOCK, D)
    y = ys[dest] * flat_w[order][:, None].astype(xt.dtype)
    return jax.ops.segment_sum(y, tok, num_segments=N)


def setup_inputs(seed: int = 0) -> dict:
    key = jax.random.key(seed)
    ks = jax.random.split(key, 32)
    D = D_MODEL
    f32 = jnp.float32

    def nrm(k, shape, s):
        return jax.random.normal(k, shape, f32) * s

    u = jax.random.uniform(ks[17], (N_LAYERS_B, 2, LRU_WIDTH), f32, minval=0.9, maxval=0.999)
    return {
        "x": nrm(ks[0], (BATCH, SEQ, D), 1.0),
        "c": nrm(ks[1], (BATCH, D), 1.0),
        "ctx": nrm(ks[2], (BATCH, CTX_LEN, D), 1.0),
        "c_ctx": nrm(ks[3], (D,), 1.0),
        "ada_w": nrm(ks[4], (DEPTH, D, N_MOD * D), D ** -0.5),
        "ada_b": nrm(ks[5], (DEPTH, N_MOD * D), 0.02),
        "norm_g": 1.0 + nrm(ks[6], (DEPTH, 2, D), 0.02),
        "na_w_qkv": nrm(ks[7], (N_LAYERS_A, D, 3 * D), D ** -0.5),
        "na_b_qkv": nrm(ks[8], (N_LAYERS_A, 3 * D), 0.02),
        "na_rpb": nrm(ks[9], (N_LAYERS_A, NA_HEADS, 2 * WIN_H - 1, 2 * WIN_W - 1), 0.5),
        "na_w_o": nrm(ks[10], (N_LAYERS_A, D, D), D ** -0.5),
        "na_b_o": nrm(ks[11], (N_LAYERS_A, D), 0.02),
        "lru_w_in": nrm(ks[12], (N_LAYERS_B, D, 2 * LRU_WIDTH), D ** -0.5),
        "lru_b_in": nrm(ks[13], (N_LAYERS_B, 2 * LRU_WIDTH), 0.02),
        "lru_conv_w": nrm(ks[14], (N_LAYERS_B, CONV_W, LRU_WIDTH), CONV_W ** -0.5),
        "lru_conv_b": nrm(ks[15], (N_LAYERS_B, LRU_WIDTH), 0.02),
        "lru_w_gate": nrm(ks[16], (N_LAYERS_B, 2, 2, LRU_BLOCKS, LRU_BLOCK_W, LRU_BLOCK_W), LRU_BLOCK_W ** -0.5),
        "lru_b_gate": nrm(ks[18], (N_LAYERS_B, 2, 2, LRU_WIDTH), 0.02),
        "lru_lambda": jnp.log(u) - jnp.log1p(-u),
        "lru_w_o": nrm(ks[19], (N_LAYERS_B, LRU_WIDTH, D), LRU_WIDTH ** -0.5),
        "lru_b_o": nrm(ks[20], (N_LAYERS_B, D), 0.02),
        "moe_w_group": nrm(ks[21], (DEPTH, D, N_GROUPS), D ** -0.5),
        "moe_b_group": nrm(ks[22], (DEPTH, N_GROUPS), 0.01),
        "moe_w_expert": nrm(ks[23], (DEPTH, N_GROUPS, D, EXPERTS_PER_GROUP), D ** -0.5),
        "moe_b_expert": nrm(ks[24], (DEPTH, N_GROUPS, EXPERTS_PER_GROUP), 0.01),
        "moe_w1": nrm(ks[25], (DEPTH, N_EXPERTS, D, EXPERT_FF), D ** -0.5),
        "moe_w3": nrm(ks[26], (DEPTH, N_EXPERTS, D, EXPERT_FF), D ** -0.5),
        "moe_w2": nrm(ks[27], (DEPTH, N_EXPERTS, EXPERT_FF, D), EXPERT_FF ** -0.5),
        "final_g": 1.0 + nrm(ks[28], (D,), 0.02),
    }


def reference(x, c, ctx, c_ctx, ada_w, ada_b, norm_g,
              na_w_qkv, na_b_qkv, na_rpb, na_w_o, na_b_o,
              lru_w_in, lru_b_in, lru_conv_w, lru_conv_b, lru_w_gate, lru_b_gate, lru_lambda, lru_w_o, lru_b_o,
              moe_w_group, moe_b_group, moe_w_expert, moe_b_expert, moe_w1, moe_w3, moe_w2,
              final_g):
    B_, L, D = x.shape
    C = ctx.shape[1]
    xl, xc = x, ctx
    for i in range(DEPTH):
        last = i == DEPTH - 1
        mod_l = jnp.split(jax.nn.silu(c) @ ada_w[i] + ada_b[i], N_MOD, axis=-1)
        mod_c = jnp.split(jax.nn.silu(c_ctx) @ ada_w[i] + ada_b[i], N_MOD, axis=-1)
        sh1, sc1, gt1, sh2, sc2, gt2 = [m[:, None, :] for m in mod_l]
        csh1, csc1, cgt1, csh2, csc2, cgt2 = mod_c

        hl = modulate(rmsnorm(xl, norm_g[i, 0]), sh1, sc1)
        hc = modulate(rmsnorm(xc, norm_g[i, 0]), csh1, csc1)
        j = i // N_MIXERS
        if i % N_MIXERS == 0:
            ol, oc = neighbourhood_attention(hl, hc, na_w_qkv[j], na_b_qkv[j], na_rpb[j],
                                             na_w_o[j], na_b_o[j], not last)
        else:
            ol, oc = rglru_mixer(hl, hc, lru_w_in[j], lru_b_in[j], lru_conv_w[j], lru_conv_b[j],
                                 lru_w_gate[j], lru_b_gate[j], lru_lambda[j], lru_w_o[j], lru_b_o[j], not last)
        xl = xl + gt1 * ol
        if not last:
            xc = xc + cgt1 * oc

        hl = modulate(rmsnorm(xl, norm_g[i, 1]), sh2, sc2).reshape(B_ * L, D)
        moe_args = (moe_w_group[i], moe_b_group[i], moe_w_expert[i], moe_b_expert[i],
                    moe_w1[i], moe_w3[i], moe_w2[i])
        if not last:
            hc = modulate(rmsnorm(xc, norm_g[i, 1]), csh2, csc2).reshape(B_ * C, D)
            out = hier_moe(jnp.concatenate([hl, hc], axis=0), *moe_args)
            xl = xl + gt2 * out[:B_ * L].reshape(B_, L, D)
            xc = xc + cgt2 * out[B_ * L:].reshape(B_, C, D)
        else:
            xl = xl + gt2 * hier_moe(hl, *moe_args).reshape(B_, L, D)
    return rmsnorm(xl, final_g)
```

```python
import functools

import jax
import jax.numpy as jnp
import numpy as np
from jax import lax
from jax.experimental import pallas as pl
from jax.experimental.pallas import tpu as pltpu

F32 = jnp.float32
BF16 = jnp.bfloat16

GRID_W = 64
N_MOD = 6
NA_HEADS = 16
WIN_H = 8
WIN_W = 16
LRU_BLOCKS = 4
CONV_W = 4
LRU_C = 8.0
N_GROUPS = 4
EXPERTS_PER_GROUP = 8
N_EXPERTS = N_GROUPS * EXPERTS_PER_GROUP
EPS = 1e-6

LANES = 128
MOD_ROWS = 16
NEG = -1e30
VMEM_LIMIT = 56 * 1024 * 1024
HIGHEST = lax.Precision.HIGHEST


def _cparams(sem, vmem=VMEM_LIMIT):
    return pltpu.CompilerParams(dimension_semantics=sem, vmem_limit_bytes=vmem)


def _mod_kernel(c_ref, w_ref, b_ref, o_ref):
    c = c_ref[...]
    s = c * jax.nn.sigmoid(c)
    o_ref[0] = jnp.dot(s, w_ref[0], precision=HIGHEST, preferred_element_type=F32) + b_ref[0]


def _modulation(cvec, ada_w, ada_b):
    depth, d, n = ada_w.shape
    tn = 1536
    return pl.pallas_call(
        _mod_kernel,
        grid=(depth, n // tn),
        in_specs=[
            pl.BlockSpec((MOD_ROWS, d), lambda l, j: (0, 0)),
            pl.BlockSpec((1, d, tn), lambda l, j: (l, 0, j)),
            pl.BlockSpec((1, 1, tn), lambda l, j: (l, 0, j)),
        ],
        out_specs=pl.BlockSpec((1, MOD_ROWS, tn), lambda l, j: (l, 0, j)),
        out_shape=jax.ShapeDtypeStruct((depth, MOD_ROWS, n), F32),
        compiler_params=_cparams(("arbitrary", "arbitrary")),
        name="adaln_mod",
    )(cvec, ada_w, ada_b.reshape(depth, 1, n))


def _norm_mod(x, g, sh, sc):
    ms = jnp.mean(x * x, axis=-1, keepdims=True)
    y = x * lax.rsqrt(ms + EPS) * g
    return y * (1.0 + sc) + sh


def _proj_kernel(x_ref, g_ref, sh_ref, sc_ref, w_ref, b_ref, *o_refs, splits, scales):
    h = _norm_mod(x_ref[...], g_ref[...], sh_ref[0], sc_ref[0]).astype(BF16)
    off = 0
    for o_ref, n, s in zip(o_refs, splits, scales):
        y = jnp.dot(h, w_ref[:, off:off + n], preferred_element_type=F32) + b_ref[:, off:off + n]
        if s != 1.0:
            y = y * s
        o_ref[...] = y.astype(o_ref.dtype)
        off += n


def _proj(x2d, g, mod3, row_of_tile, sh_chunk, sc_chunk, w, b, splits, dtypes, scales, tm):
    n_tok, d = x2d.shape
    n_out = w.shape[1]
    assert sum(splits) == n_out and n_tok % tm == 0
    return pl.pallas_call(
        functools.partial(_proj_kernel, splits=tuple(splits), scales=tuple(scales)),
        grid=(n_tok // tm,),
        in_specs=[
            pl.BlockSpec((tm, d), lambda i: (i, 0)),
            pl.BlockSpec((1, d), lambda i: (0, 0)),
            pl.BlockSpec((1, 1, d), lambda i: (row_of_tile(i), 0, sh_chunk)),
            pl.BlockSpec((1, 1, d), lambda i: (row_of_tile(i), 0, sc_chunk)),
            pl.BlockSpec((d, n_out), lambda i: (0, 0)),
            pl.BlockSpec((1, n_out), lambda i: (0, 0)),
        ],
        out_specs=[pl.BlockSpec((tm, n), lambda i: (i, 0)) for n in splits],
        out_shape=[jax.ShapeDtypeStruct((n_tok, n), dt) for n, dt in zip(splits, dtypes)],
        compiler_params=_cparams(("parallel",)),
        name="norm_mod_proj",
    )(x2d, g.reshape(1, d), mod3, mod3, w, b.reshape(1, n_out))


def _resid_kernel(a_ref, w_ref, b_ref, res_ref, gate_ref, o_ref):
    y = jnp.dot(a_ref[...], w_ref[...], preferred_element_type=F32) + b_ref[...]
    o_ref[...] = res_ref[...] + gate_ref[0] * y


def _resid_proj(a, w, b, res, mod3, row_of_tile, gate_chunk, tm):
    n_tok, k = a.shape
    d = w.shape[1]
    return pl.pallas_call(
        _resid_kernel,
        grid=(n_tok // tm,),
        in_specs=[
            pl.BlockSpec((tm, k), lambda i: (i, 0)),
            pl.BlockSpec((k, d), lambda i: (0, 0)),
            pl.BlockSpec((1, d), lambda i: (0, 0)),
            pl.BlockSpec((tm, d), lambda i: (i, 0)),
            pl.BlockSpec((1, 1, d), lambda i: (row_of_tile(i), 0, gate_chunk)),
        ],
        out_specs=pl.BlockSpec((tm, d), lambda i: (i, 0)),
        out_shape=jax.ShapeDtypeStruct((n_tok, d), F32),
        compiler_params=_cparams(("parallel",)),
        name="proj_residual",
    )(a, w, b.reshape(1, d), res, mod3)


def _head_pair_attention(q2, keys, vals, biases, lane_lo):
    outs = []
    for j in range(2):
        sel = lane_lo if j == 0 else jnp.logical_not(lane_lo)
        qm = jnp.where(sel, q2, jnp.zeros_like(q2))
        s_parts = []
        for kk, bb in zip(keys, biases[j]):
            s = lax.dot_general(qm, kk, (((1,), (1,)), ((), ())), preferred_element_type=F32)
            s_parts.append(s if bb is None else s + bb)
        m = s_parts[0].max(axis=-1, keepdims=True)
        for s in s_parts[1:]:
            m = jnp.maximum(m, s.max(axis=-1, keepdims=True))
        p_parts = [jnp.exp(s - m) for s in s_parts]
        l = p_parts[0].sum(axis=-1, keepdims=True)
        for p in p_parts[1:]:
            l = l + p.sum(axis=-1, keepdims=True)
        o = None
        for p, vv in zip(p_parts, vals):
            t = jnp.dot(p.astype(BF16), vv, preferred_element_type=F32)
            o = t if o is None else o + t
        outs.append(o / l)
    return jnp.where(lane_lo, outs[0], outs[1])


def _na_kernel(q_ref, k_ref, v_ref, kc_ref, vc_ref, mb_ref, o_ref, *, rows, kh):
    r = pl.program_id(1)
    rs = jnp.clip(r - kh // 2, 0, rows - kh)
    k0 = pl.multiple_of(rs * GRID_W, GRID_W)
    n_win = kh * GRID_W
    lane_lo = lax.broadcasted_iota(jnp.int32, (GRID_W, LANES), 1) < (LANES // 2)

    def pair(p, carry):
        c0 = pl.multiple_of(p * LANES, LANES)
        q2 = q_ref[0, :, pl.ds(c0, LANES)]
        kw = k_ref[0, pl.ds(k0, n_win), pl.ds(c0, LANES)]
        vw = v_ref[0, pl.ds(k0, n_win), pl.ds(c0, LANES)]
        kc = kc_ref[0, :, pl.ds(c0, LANES)]
        vc = vc_ref[0, :, pl.ds(c0, LANES)]
        biases = [[mb_ref[0, 2 * p + j], None] for j in range(2)]
        o = _head_pair_attention(q2, [kw, kc], [vw, vc], biases, lane_lo)
        o_ref[0, :, pl.ds(c0, LANES)] = o.astype(o_ref.dtype)
        return carry

    lax.fori_loop(0, NA_HEADS // 2, pair, 0)


def _window_bias(rpb, rows):
    kh = min(WIN_H, rows)
    cols = np.arange(GRID_W)
    col_start = np.clip(cols - WIN_W // 2, 0, GRID_W - WIN_W)
    ck = np.arange(GRID_W)
    in_win = (ck[None, :] >= col_start[:, None]) & (ck[None, :] < col_start[:, None] + WIN_W)
    dc = np.clip(ck[None, :] - cols[:, None] + (WIN_W - 1), 0, 2 * WIN_W - 2)
    dr = -np.arange(kh)[:, None] + np.arange(kh)[None, :] + (WIN_H - 1)
    dr = np.clip(dr, 0, 2 * WIN_H - 2)
    dr_b = dr[:, None, :, None]
    dc_b = dc[None, :, None, :]
    t = rpb[:, dr_b, dc_b]
    t = jnp.where(in_win[None, None, :, None, :], t, NEG)
    t = jnp.transpose(t, (1, 0, 2, 3, 4))
    return t.reshape(kh, NA_HEADS, GRID_W, kh * GRID_W).astype(F32)


def _na_attention(q, k, v, kc, vc, mb):
    b, l, d = q.shape
    c = kc.shape[1]
    rows = l // GRID_W
    kh = mb.shape[0]

    def delta(bi, r):
        return r - jnp.clip(r - kh // 2, 0, rows - kh)

    return pl.pallas_call(
        functools.partial(_na_kernel, rows=rows, kh=kh),
        grid=(b, rows),
        in_specs=[
            pl.BlockSpec((1, GRID_W, d), lambda bi, r: (bi, r, 0)),
            pl.BlockSpec((1, l, d), lambda bi, r: (bi, 0, 0), pipeline_mode=pl.Buffered(1)),
            pl.BlockSpec((1, l, d), lambda bi, r: (bi, 0, 0), pipeline_mode=pl.Buffered(1)),
            pl.BlockSpec((1, c, d), lambda bi, r: (bi, 0, 0)),
            pl.BlockSpec((1, c, d), lambda bi, r: (bi, 0, 0)),
            pl.BlockSpec((1, NA_HEADS, GRID_W, kh * GRID_W), lambda bi, r: (delta(bi, r), 0, 0, 0)),
        ],
        out_specs=pl.BlockSpec((1, GRID_W, d), lambda bi, r: (bi, r, 0)),
        out_shape=jax.ShapeDtypeStruct((b, l, d), BF16),
        compiler_params=_cparams(("parallel", "arbitrary")),
        name="na_attention",
    )(q, k, v, kc, vc, mb)


def _ctx_attn_kernel(q_ref, k_ref, v_ref, o_ref):
    n_q = q_ref.shape[1]
    lane_lo = lax.broadcasted_iota(jnp.int32, (n_q, LANES), 1) < (LANES // 2)

    def pair(p, carry):
        c0 = pl.multiple_of(p * LANES, LANES)
        q2 = q_ref[0, :, pl.ds(c0, LANES)]
        kc = k_ref[0, :, pl.ds(c0, LANES)]
        vc = v_ref[0, :, pl.ds(c0, LANES)]
        o = _head_pair_attention(q2, [kc], [vc], [[None], [None]], lane_lo)
        o_ref[0, :, pl.ds(c0, LANES)] = o.astype(o_ref.dtype)
        return carry

    lax.fori_loop(0, NA_HEADS // 2, pair, 0)


def _ctx_attention(qc, kc, vc):
    b, c, d = qc.shape
    spec = pl.BlockSpec((1, c, d), lambda bi: (bi, 0, 0))
    return pl.pallas_call(
        _ctx_attn_kernel,
        grid=(b,),
        in_specs=[spec, spec, spec],
        out_specs=spec,
        out_shape=jax.ShapeDtypeStruct((b, c, d), BF16),
        compiler_params=_cparams(("parallel",)),
        name="ctx_attention",
    )(qc, kc, vc)


GROUP_LANE0 = N_EXPERTS


def _src_specs(srcs, tm):
    d = srcs[0].shape[1]
    tiles = [s.shape[0] // tm for s in srcs]
    specs, first = [], 0
    for t in tiles:
        specs.append(pl.BlockSpec((tm, d), functools.partial(
            lambda i, first, t: (jnp.clip(i - first, 0, t - 1), 0), first=first, t=t)))
        first += t
    return specs, tiles


def _select_src(i, refs, tiles):
    x = refs[-1][...]
    first = sum(tiles[:-1])
    for ref, t in zip(refs[-2::-1], tiles[-2::-1]):
        x = jnp.where(i < first, ref[...], x)
        first -= t
    return x


def _router_kernel(*refs, tiles):
    n_src = len(tiles)
    x_refs = refs[:n_src]
    g_ref, sh_ref, sc_ref, wr_ref, br_ref, h_ref, oh_ref, wd_ref, cnt_ref = refs[n_src:]
    i = pl.program_id(0)
    h = _norm_mod(_select_src(i, x_refs, tiles), g_ref[...], sh_ref[0], sc_ref[0])
    h_ref[...] = h
    logits = jnp.dot(h, wr_ref[...], precision=HIGHEST, preferred_element_type=F32) + br_ref[...]
    tm = logits.shape[0]
    lane = lax.broadcasted_iota(jnp.int32, (tm, LANES), 1)
    big = jnp.int32(LANES)

    gmask = (lane >= GROUP_LANE0) & (lane < GROUP_LANE0 + N_GROUPS)
    gl = jnp.where(gmask, logits, NEG)
    gmax = gl.max(axis=-1, keepdims=True)
    gsel = jnp.where(gmask & (gl == gmax), lane, big).min(axis=-1, keepdims=True) - GROUP_LANE0
    g_w = 1.0 / jnp.where(gmask, jnp.exp(gl - gmax), 0.0).sum(axis=-1, keepdims=True)

    e0 = gsel * EXPERTS_PER_GROUP
    emask = (lane >= e0) & (lane < e0 + EXPERTS_PER_GROUP)
    el = jnp.where(emask, logits, NEG)
    v1 = el.max(axis=-1, keepdims=True)
    i1 = jnp.where(emask & (el == v1), lane, big).min(axis=-1, keepdims=True)
    el2 = jnp.where(lane == i1, NEG, el)
    v2 = el2.max(axis=-1, keepdims=True)
    i2 = jnp.where(emask & (lane != i1) & (el2 == v2), lane, big).min(axis=-1, keepdims=True)
    t = jnp.exp(v2 - v1)
    w1 = g_w / (1.0 + t)
    w2 = g_w * t / (1.0 + t)

    sel1 = lane == i1
    sel2 = lane == i2
    oh = jnp.where(sel1 | sel2, 1.0, 0.0)
    oh_ref[...] = oh.astype(oh_ref.dtype)
    wd_ref[...] = jnp.where(sel1, w1, jnp.where(sel2, w2, 0.0))

    @pl.when(i == 0)
    def _():
        cnt_ref[...] = jnp.zeros_like(cnt_ref)

    cnt_ref[...] += oh.sum(axis=0, keepdims=True)


def _router(srcs, g, mod3, row_of_tile, sh_chunk, sc_chunk, wr, br, tm):
    d = srcs[0].shape[1]
    src_specs, tiles = _src_specs(srcs, tm)
    n = sum(tiles) * tm
    tok = pl.BlockSpec((tm, LANES), lambda i: (i, 0))
    return pl.pallas_call(
        functools.partial(_router_kernel, tiles=tuple(tiles)),
        grid=(sum(tiles),),
        in_specs=src_specs + [
            pl.BlockSpec((1, d), lambda i: (0, 0)),
            pl.BlockSpec((1, 1, d), lambda i: (row_of_tile(i), 0, sh_chunk)),
            pl.BlockSpec((1, 1, d), lambda i: (row_of_tile(i), 0, sc_chunk)),
            pl.BlockSpec((d, LANES), lambda i: (0, 0)),
            pl.BlockSpec((1, LANES), lambda i: (0, 0)),
        ],
        out_specs=[pl.BlockSpec((tm, d), lambda i: (i, 0)), tok, tok,
                   pl.BlockSpec((1, LANES), lambda i: (0, 0))],
        out_shape=[jax.ShapeDtypeStruct((n, d), F32), jax.ShapeDtypeStruct((n, LANES), BF16),
                   jax.ShapeDtypeStruct((n, LANES), F32), jax.ShapeDtypeStruct((1, LANES), F32)],
        compiler_params=_cparams(("arbitrary",)),
        name="moe_router",
    )(*srcs, g.reshape(1, d), mod3, mod3, wr, br)


def _dest_kernel(start_ref, oh_ref, wd_ref, dest_ref, wsel_ref, carry_ref):
    i = pl.program_id(0)

    @pl.when(i == 0)
    def _():
        carry_ref[...] = jnp.zeros_like(carry_ref)

    oh = oh_ref[...]
    tm = oh.shape[0]
    row = lax.broadcasted_iota(jnp.int32, (tm, tm), 0)
    col = lax.broadcasted_iota(jnp.int32, (tm, tm), 1)
    tri = jnp.where(row > col, 1.0, 0.0).astype(BF16)
    base = jnp.dot(tri, oh, preferred_element_type=F32) + carry_ref[...] + start_ref[...]
    ohf = oh.astype(F32)
    lane = lax.broadcasted_iota(jnp.int32, (tm, LANES), 1)
    sel = ohf > 0.0
    ea = jnp.where(sel, lane, LANES).min(axis=-1, keepdims=True)
    eb = jnp.where(sel, lane, -1).max(axis=-1, keepdims=True)
    wd = wd_ref[...]
    pick = lambda e, val: jnp.where(lane == e, val, 0.0).sum(axis=-1, keepdims=True)
    two = lax.broadcasted_iota(jnp.int32, (tm, 2), 1) == 0
    dest_ref[...] = jnp.where(two, pick(ea, base), pick(eb, base)).astype(jnp.int32)
    wsel_ref[...] = jnp.where(two, pick(ea, wd), pick(eb, wd))
    carry_ref[...] += ohf.sum(axis=0, keepdims=True)


def _dest(start, oh, wd, tm):
    n = oh.shape[0]
    tok = pl.BlockSpec((tm, LANES), lambda i: (i, 0))
    two = pl.BlockSpec((tm, 2), lambda i: (i, 0))
    return pl.pallas_call(
        _dest_kernel,
        grid=(n // tm,),
        in_specs=[pl.BlockSpec((1, LANES), lambda i: (0, 0)), tok, tok],
        out_specs=[two, two],
        out_shape=[jax.ShapeDtypeStruct((n, 2), jnp.int32), jax.ShapeDtypeStruct((n, 2), F32)],
        scratch_shapes=[pltpu.VMEM((1, LANES), F32)],
        compiler_params=_cparams(("arbitrary",)),
        name="moe_dest",
    )(start, oh, wd)


def _row_copy(src, s, dst, t, sem):
    return pltpu.make_async_copy(src.at[pl.ds(s, 1), :], dst.at[pl.ds(t, 1), :], sem)


def _scatter_kernel(dest_ref, h_ref, xs_ref, sem):
    tm = h_ref.shape[0]

    def body(r, carry):
        for k in range(2):
            _row_copy(h_ref, r, xs_ref, dest_ref[0, 0, 2 * r + k], sem).start()
        return carry

    lax.fori_loop(0, tm, body, 0)
    for k in range(2):
        pltpu.make_async_copy(h_ref, xs_ref.at[pl.ds(0, tm), :], sem).wait()


def _scatter_rows(dest, h, tm):
    n, d = h.shape
    return pl.pallas_call(
        _scatter_kernel,
        grid=(n // tm,),
        in_specs=[
            pl.BlockSpec((1, 1, 2 * tm), lambda i: (i, 0, 0), memory_space=pltpu.SMEM),
            pl.BlockSpec((tm, d), lambda i: (i, 0)),
        ],
        out_specs=pl.BlockSpec(memory_space=pl.ANY),
        out_shape=jax.ShapeDtypeStruct((2 * n, d), F32),
        scratch_shapes=[pltpu.SemaphoreType.DMA(())],
        compiler_params=_cparams(("arbitrary",)),
        name="moe_scatter",
    )(dest.reshape(n // tm, 1, 2 * tm), h)


def _gmm_kernel(vb_ref, ve_ref, lo_ref, hi_ref, x_ref, w1_ref, w3_ref, w2_ref, o_ref, w1b, w3b, w2b):
    v = pl.program_id(0)
    pv = jnp.maximum(v - 1, 0)

    @pl.when((v == 0) | (ve_ref[v] != ve_ref[pv]))
    def _():
        w1b[...] = w1_ref[0].astype(BF16)
        w3b[...] = w3_ref[0].astype(BF16)
        w2b[...] = w2_ref[0].astype(BF16)

    lo = lo_ref[v]
    hi = hi_ref[v]

    @pl.when(hi > lo)
    def _():
        xb = x_ref[...].astype(BF16)
        a = jnp.dot(xb, w1b[...], preferred_element_type=F32)
        g = jnp.dot(xb, w3b[...], preferred_element_type=F32)
        hdn = (a * jax.nn.sigmoid(a) * g).astype(BF16)
        y = jnp.dot(hdn, w2b[...], preferred_element_type=F32)
        rows = lax.broadcasted_iota(jnp.int32, (y.shape[0], 1), 0)
        mine = (rows >= lo) & (rows < hi)
        first = (v == 0) | (vb_ref[v] != vb_ref[pv])

        @pl.when(first)
        def _():
            o_ref[...] = jnp.where(mine, y, 0.0)

        @pl.when(jnp.logical_not(first))
        def _():
            o_ref[...] = jnp.where(mine, y, o_ref[...])


def _gmm(visits, xs, w1, w3, w2, bm):
    a, d = xs.shape
    f = w1.shape[2]
    n_vis = visits[0].shape[0]
    grid_spec = pltpu.PrefetchScalarGridSpec(
        num_scalar_prefetch=4,
        grid=(n_vis,),
        in_specs=[
            pl.BlockSpec((bm, d), lambda v, vb, ve, lo, hi: (vb[v], 0)),
            pl.BlockSpec((1, d, f), lambda v, vb, ve, lo, hi: (ve[v], 0, 0)),
            pl.BlockSpec((1, d, f), lambda v, vb, ve, lo, hi: (ve[v], 0, 0)),
            pl.BlockSpec((1, f, d), lambda v, vb, ve, lo, hi: (ve[v], 0, 0)),
        ],
        out_specs=pl.BlockSpec((bm, d), lambda v, vb, ve, lo, hi: (vb[v], 0)),
        scratch_shapes=[pltpu.VMEM((d, f), BF16), pltpu.VMEM((d, f), BF16), pltpu.VMEM((f, d), BF16)],
    )
    return pl.pallas_call(
        _gmm_kernel,
        grid_spec=grid_spec,
        out_shape=jax.ShapeDtypeStruct((a, d), F32),
        compiler_params=_cparams(("arbitrary",)),
        name="moe_experts",
    )(*visits, xs, w1, w3, w2)


def _visit_plan(counts, n_rows, bm):
    counts = counts.astype(jnp.int32)
    end = jnp.cumsum(counts)
    start = end - counts
    n_blocks = n_rows // bm
    n_vis = n_blocks + N_EXPERTS
    tiles = jnp.where(counts > 0, (end - 1) // bm - start // bm + 1, 0)
    vend = jnp.cumsum(tiles)
    vstart = vend - tiles
    v = jnp.arange(n_vis, dtype=jnp.int32)
    e = jnp.minimum(jnp.searchsorted(vend, v, side="right").astype(jnp.int32), N_EXPERTS - 1)
    valid = v < vend[-1]
    blk = jnp.where(valid, start[e] // bm + (v - vstart[e]), n_blocks - 1)
    lo = jnp.clip(start[e] - blk * bm, 0, bm)
    hi = jnp.clip(end[e] - blk * bm, 0, bm)
    e = jnp.where(valid, e, jnp.max(jnp.where(valid, e, 0)))
    lo = jnp.where(valid, lo, 0)
    hi = jnp.where(valid, hi, 0)
    return start, (blk.astype(jnp.int32), e.astype(jnp.int32), lo.astype(jnp.int32), hi.astype(jnp.int32))


def _combine_kernel(dest_ref, ys_ref, *refs, tiles, final_norm):
    n_src = len(tiles)
    x_refs = refs[:n_src]
    w_ref, gate_ref = refs[n_src:n_src + 2]
    rest = refs[n_src + 2:]
    fg_ref = rest[0] if final_norm else None
    o_refs = rest[-n_src - 2:-2]
    buf, sem = rest[-2:]
    i = pl.program_id(0)
    tm = x_refs[0].shape[0]

    def body(r, carry):
        for k in range(2):
            _row_copy(ys_ref, dest_ref[0, 0, 2 * r + k], buf.at[k], r, sem).start()
        return carry

    lax.fori_loop(0, tm, body, 0)
    for k in range(2):
        pltpu.make_async_copy(ys_ref.at[pl.ds(0, tm), :], buf.at[k], sem).wait()
    w = w_ref[...]
    moe = w[:, 0:1] * buf[0] + w[:, 1:2] * buf[1]
    y = _select_src(i, x_refs, tiles) + gate_ref[0] * moe
    if final_norm:
        y = y * lax.rsqrt(jnp.mean(y * y, axis=-1, keepdims=True) + EPS) * fg_ref[...]

    first = 0
    for o_ref, t in zip(o_refs, tiles):
        @pl.when((i >= first) & (i < first + t))
        def _(o_ref=o_ref):
            o_ref[...] = y
        first += t


def _combine(dest, wsel, ys, srcs, mod3, row_of_tile, gate_chunk, final_g, tm):
    d = srcs[0].shape[1]
    src_specs, tiles = _src_specs(srcs, tm)
    n = dest.shape[0]
    final_norm = final_g is not None
    extra_specs = [pl.BlockSpec((1, d), lambda i: (0, 0))] if final_norm else []
    extra_args = [final_g.reshape(1, d)] if final_norm else []
    return pl.pallas_call(
        functools.partial(_combine_kernel, tiles=tuple(tiles), final_norm=final_norm),
        grid=(sum(tiles),),
        in_specs=[
            pl.BlockSpec((1, 1, 2 * tm), lambda i: (i, 0, 0), memory_space=pltpu.SMEM),
            pl.BlockSpec(memory_space=pl.ANY),
        ] + src_specs + [
            pl.BlockSpec((tm, 2), lambda i: (i, 0)),
            pl.BlockSpec((1, 1, d), lambda i: (row_of_tile(i), 0, gate_chunk)),
        ] + extra_specs,
        out_specs=list(src_specs),
        out_shape=[jax.ShapeDtypeStruct(s.shape, F32) for s in srcs],
        scratch_shapes=[pltpu.VMEM((2, tm, d), F32), pltpu.SemaphoreType.DMA(())],
        compiler_params=_cparams(("arbitrary",)),
        name="moe_combine",
    )(dest.reshape(n // tm, 1, 2 * tm), ys, *srcs, wsel, mod3, *extra_args)


def _router_weights(w_group, b_group, w_expert, b_expert):
    d = w_group.shape[0]
    we = jnp.transpose(w_expert, (1, 0, 2)).reshape(d, N_EXPERTS)
    wr = jnp.concatenate([we, w_group, jnp.zeros((d, LANES - N_EXPERTS - N_GROUPS), F32)], axis=1)
    br = jnp.concatenate([b_expert.reshape(N_EXPERTS), b_group,
                          jnp.zeros((LANES - N_EXPERTS - N_GROUPS,), F32)]).reshape(1, LANES)
    return wr, br


def _moe_layer(srcs, g, mod3, row_of_tile, chunks, router_w, w1, w3, w2, final_g, tm, bm):
    sh_chunk, sc_chunk, gate_chunk = chunks
    wr, br = _router_weights(*router_w)
    h, oh, wd, cnt = _router(srcs, g, mod3, row_of_tile, sh_chunk, sc_chunk, wr, br, tm)
    n = h.shape[0]
    start, visits = _visit_plan(cnt[0, :N_EXPERTS], 2 * n, bm)
    start_row = jnp.zeros((1, LANES), F32).at[0, :N_EXPERTS].set(start.astype(F32))
    dest, wsel = _dest(start_row, oh, wd, tm)
    xs = _scatter_rows(dest, h, tm)
    ys = _gmm(visits, xs, w1, w3, w2, bm)
    return _combine(dest, wsel, ys, srcs, mod3, row_of_tile, gate_chunk, final_g, tm)


SUB = 8
HALO = 8


def _lru_kernel(u_ref, up_ref, un_ref, cw_ref, cb_ref, wg_ref, bg_ref, lam_ref, h0_ref,
                h_ref, hend_ref, ubuf, a_s, b_s, carry, *, tt, n_t, reverse):
    i = pl.program_id(2)
    ti = (n_t - 1 - i) if reverse else i
    cb = u_ref.shape[2]
    groups = tt // SUB

    @pl.when(i == 0)
    def _():
        carry[...] = h0_ref[0]

    u = u_ref[0]
    ubuf[0:HALO] = jnp.where(ti > 0, up_ref[0], 0.0)
    ubuf[HALO:HALO + tt] = u
    ubuf[HALO + tt:2 * HALO + tt] = jnp.where(ti < n_t - 1, un_ref[0], 0.0)
    cw = cw_ref[...]
    left = CONV_W // 2
    uc = cb_ref[...] + cw[left:left + 1] * u
    for kk in range(CONV_W):
        if kk != left:
            o = HALO + kk - left
            uc = uc + cw[kk:kk + 1] * ubuf[o:o + tt]

    ub = uc.astype(BF16)
    r = jax.nn.sigmoid(jnp.dot(ub, wg_ref[0, 0], preferred_element_type=F32) + bg_ref[0])
    ig = jax.nn.sigmoid(jnp.dot(ub, wg_ref[1, 0], preferred_element_type=F32) + bg_ref[1])
    lam = lam_ref[...]
    log_a = (LRU_C * r) * (-jnp.log1p(jnp.exp(-lam)))
    a = jnp.exp(log_a)
    b = jnp.sqrt(1.0 - a * a) * (ig * uc)

    a3 = a.reshape(groups, SUB, cb)
    b3 = b.reshape(groups, SUB, cb)
    sub = lax.broadcasted_iota(jnp.int32, (groups, SUB, cb), 1)
    s = 1
    while s < SUB:
        shift = SUB - s if reverse else s
        keep = (sub < SUB - s) if reverse else (sub >= s)
        a_sh = pltpu.roll(a3, shift, 1)
        b_sh = pltpu.roll(b3, shift, 1)
        b3 = jnp.where(keep, a3 * b_sh + b3, b3)
        a3 = jnp.where(keep, a3 * a_sh, a3)
        s *= 2
    a_s[...] = a3
    b_s[...] = b3

    def chain(gi, h):
        g = (groups - 1 - gi) if reverse else gi
        hg = b_s[g] + a_s[g] * h
        h_ref[0, pl.ds(pl.multiple_of(g * SUB, SUB), SUB), :] = hg
        return hg[0:1] if reverse else hg[SUB - 1:SUB]

    h_last = lax.fori_loop(0, groups, chain, carry[...], unroll=8)
    carry[...] = h_last

    @pl.when(i == n_t - 1)
    def _():
        hend_ref[0] = h_last


def _lru_scan(u, h0, conv_w, conv_b, wg, bg, lam, reverse, tt):
    b, t, w = u.shape
    cb = w // LRU_BLOCKS
    n_t = t // tt
    hb = tt // HALO
    tmap = (lambda i: n_t - 1 - i) if reverse else (lambda i: i)
    return pl.pallas_call(
        functools.partial(_lru_kernel, tt=tt, n_t=n_t, reverse=reverse),
        grid=(b, LRU_BLOCKS, n_t),
        in_specs=[
            pl.BlockSpec((1, tt, cb), lambda bi, n, i: (bi, tmap(i), n)),
            pl.BlockSpec((1, HALO, cb), lambda bi, n, i: (bi, jnp.maximum(tmap(i) * hb - 1, 0), n)),
            pl.BlockSpec((1, HALO, cb), lambda bi, n, i: (bi, jnp.minimum((tmap(i) + 1) * hb, t // HALO - 1), n)),
            pl.BlockSpec((CONV_W, cb), lambda bi, n, i: (0, n)),
            pl.BlockSpec((1, cb), lambda bi, n, i: (0, n)),
            pl.BlockSpec((2, 1, cb, cb), lambda bi, n, i: (0, n, 0, 0)),
            pl.BlockSpec((2, 1, cb), lambda bi, n, i: (0, 0, n)),
            pl.BlockSpec((1, cb), lambda bi, n, i: (0, n)),
            pl.BlockSpec((1, 1, cb), lambda bi, n, i: (bi, 0, n)),
        ],
        out_specs=[pl.BlockSpec((1, tt, cb), lambda bi, n, i: (bi, tmap(i), n)),
                   pl.BlockSpec((1, 1, cb), lambda bi, n, i: (bi, 0, n))],
        out_shape=[jax.ShapeDtypeStruct((b, t, w), F32), jax.ShapeDtypeStruct((b, 1, w), F32)],
        scratch_shapes=[pltpu.VMEM((tt + 2 * HALO, cb), F32), pltpu.VMEM((tt // SUB, SUB, cb), F32),
                        pltpu.VMEM((tt // SUB, SUB, cb), F32), pltpu.VMEM((1, cb), F32)],
        compiler_params=_cparams(("parallel", "parallel", "arbitrary")),
        name="rglru_scan_rev" if reverse else "rglru_scan_fwd",
    )(u, u, u, conv_w, conv_b.reshape(1, w), wg, bg.reshape(2, 1, w), lam.reshape(1, w), h0)


def _gelu_tanh(x):
    return x * (0.5 * (1.0 + jnp.tanh(0.7978845608028654 * (x + 0.044715 * (x * x * x)))))


def _lru_out_kernel(y_ref, hf_ref, hb_ref, w_ref, b_ref, res_ref, gate_ref, o_ref):
    a = (_gelu_tanh(y_ref[...].astype(F32)) * (hf_ref[...] + hb_ref[...])).astype(BF16)
    y = jnp.dot(a, w_ref[...], preferred_element_type=F32) + b_ref[...]
    o_ref[...] = res_ref[...] + gate_ref[0] * y


def _lru_out_proj(y, hf, hb, w, b, res, mod3, row_of_tile, gate_chunk, tm):
    n_tok, k = y.shape
    d = w.shape[1]
    tok = pl.BlockSpec((tm, k), lambda i: (i, 0))
    return pl.pallas_call(
        _lru_out_kernel,
        grid=(n_tok // tm,),
        in_specs=[
            tok, tok, tok,
            pl.BlockSpec((k, d), lambda i: (0, 0)),
            pl.BlockSpec((1, d), lambda i: (0, 0)),
            pl.BlockSpec((tm, d), lambda i: (i, 0)),
            pl.BlockSpec((1, 1, d), lambda i: (row_of_tile(i), 0, gate_chunk)),
        ],
        out_specs=pl.BlockSpec((tm, d), lambda i: (i, 0)),
        out_shape=jax.ShapeDtypeStruct((n_tok, d), F32),
        compiler_params=_cparams(("parallel",)),
        name="rglru_out_proj",
    )(y, hf, hb, w, b.reshape(1, d), res, mod3)


def _lru_states(ul, uc, conv_w, conv_b, w_gate, b_gate, lam, tt):
    wg = w_gate.astype(BF16)
    b, _, w = ul.shape
    zero = jnp.zeros((b, 1, w), F32)
    hs = []
    for dirn in range(2):
        rev = dirn == 1
        args = (conv_w, conv_b, wg[dirn], b_gate[dirn], lam[dirn], rev)
        _, h_end = _lru_scan(uc, zero, *args, uc.shape[1])
        h, _ = _lru_scan(ul, h_end, *args, tt)
        hs.append(h)
    return hs


TM = 512
BM = 256
TT = 512


def kernel(x, c, ctx, c_ctx, ada_w, ada_b, norm_g, na_w_qkv, na_b_qkv, na_rpb, na_w_o, na_b_o, lru_w_in, lru_b_in, lru_conv_w, lru_conv_b, lru_w_gate, lru_b_gate, lru_lambda, lru_w_o, lru_b_o, moe_w_group, moe_b_group, moe_w_expert, moe_b_expert, moe_w1, moe_w3, moe_w2, final_g):
    b, l, d = x.shape
    n_ctx = ctx.shape[1]
    n_l, n_c = b * l, b * n_ctx
    assert l % TM == 0 and n_ctx <= TM and TM % n_ctx == 0 and b + 1 <= MOD_ROWS
    ctx_row = b
    tiles_per_batch = l // TM
    lat_row = lambda i: i // tiles_per_batch
    ctx_tile_row = lambda i: ctx_row
    nl_tiles = n_l // TM
    both_row = lambda i: jnp.where(i < nl_tiles, i // tiles_per_batch, ctx_row)

    cvec = jnp.concatenate([c, c_ctx[None], jnp.zeros((MOD_ROWS - b - 1, d), F32)], axis=0)
    mod = _modulation(cvec, ada_w, ada_b)
    mod3 = [mod[i].reshape(MOD_ROWS, 1, N_MOD * d) for i in range(mod.shape[0])]
    xl = x.reshape(n_l, d)
    xc = ctx.reshape(n_c, d)

    w_qkv = na_w_qkv[0].astype(BF16)
    qk_scale = (d // NA_HEADS) ** -0.5
    q, k, v = _proj(xl, norm_g[0, 0], mod3[0], lat_row, 0, 1, w_qkv, na_b_qkv[0],
                    (d, d, d), (BF16,) * 3, (qk_scale, 1.0, 1.0), TM)
    qc, kc, vc = _proj(xc, norm_g[0, 0], mod3[0], ctx_tile_row, 0, 1, w_qkv, na_b_qkv[0],
                       (d, d, d), (BF16,) * 3, (qk_scale, 1.0, 1.0), n_ctx)
    to3 = lambda a, s: a.reshape(b, s, d)
    mb = _window_bias(na_rpb[0], l // GRID_W)
    o_l = _na_attention(to3(q, l), to3(k, l), to3(v, l), to3(kc, n_ctx), to3(vc, n_ctx), mb)
    o_c = _ctx_attention(to3(qc, n_ctx), to3(kc, n_ctx), to3(vc, n_ctx))
    w_o = na_w_o[0].astype(BF16)
    xl = _resid_proj(o_l.reshape(n_l, d), w_o, na_b_o[0], xl, mod3[0], lat_row, 2, TM)
    xc = _resid_proj(o_c.reshape(n_c, d), w_o, na_b_o[0], xc, mod3[0], ctx_tile_row, 2, n_ctx)
    xl, xc = _moe_layer([xl, xc], norm_g[0, 1], mod3[0], both_row, (3, 4, 5),
                        (moe_w_group[0], moe_b_group[0], moe_w_expert[0], moe_b_expert[0]),
                        moe_w1[0], moe_w3[0], moe_w2[0], None, TM, BM)

    w_in = lru_w_in[0].astype(BF16)
    lw = w_in.shape[1] // 2
    y_l, u_l = _proj(xl, norm_g[1, 0], mod3[1], lat_row, 0, 1, w_in, lru_b_in[0],
                     (lw, lw), (BF16, F32), (1.0, 1.0), TM)
    (u_c,) = _proj(xc, norm_g[1, 0], mod3[1], ctx_tile_row, 0, 1, w_in[:, lw:], lru_b_in[0, lw:],
                   (lw,), (F32,), (1.0,), n_ctx)
    hf, hb = _lru_states(u_l.reshape(b, l, lw), u_c.reshape(b, n_ctx, lw), lru_conv_w[0], lru_conv_b[0],
                         lru_w_gate[0], lru_b_gate[0], lru_lambda[0], TT)
    xl = _lru_out_proj(y_l, hf.reshape(n_l, lw), hb.reshape(n_l, lw), lru_w_o[0].astype(BF16), lru_b_o[0],
                       xl, mod3[1], lat_row, 2, TM)
    (out,) = _moe_layer([xl], norm_g[1, 1], mod3[1], lat_row, (3, 4, 5),
                        (moe_w_group[1], moe_b_group[1], moe_w_expert[1], moe_b_expert[1]),
                        moe_w1[1], moe_w3[1], moe_w2[1], final_g, TM, BM)
    return out.reshape(b, l, d)
```

```python
import functools

import jax
import jax.numpy as jnp
import numpy as np
from jax import lax
from jax.experimental import pallas as pl
from jax.experimental.pallas import tpu as pltpu

F32 = jnp.float32
BF16 = jnp.bfloat16

GRID_W = 64
N_MOD = 6
NA_HEADS = 16
WIN_H = 8
WIN_W = 16
LRU_BLOCKS = 4
CONV_W = 4
LRU_C = 8.0
N_GROUPS = 4
EXPERTS_PER_GROUP = 8
N_EXPERTS = N_GROUPS * EXPERTS_PER_GROUP
EPS = 1e-6

LANES = 128
MOD_ROWS = 16
NEG = -1e30
VMEM_LIMIT = 56 * 1024 * 1024
HIGHEST = lax.Precision.HIGHEST


def _cparams(sem, vmem=VMEM_LIMIT):
    return pltpu.CompilerParams(dimension_semantics=sem, vmem_limit_bytes=vmem)


def _mod_kernel(c_ref, w_ref, b_ref, o_ref):
    c = c_ref[...]
    s = c * jax.nn.sigmoid(c)
    o_ref[0] = jnp.dot(s, w_ref[0], precision=HIGHEST, preferred_element_type=F32) + b_ref[0]


def _modulation(cvec, ada_w, ada_b):
    depth, d, n = ada_w.shape
    tn = 1536
    return pl.pallas_call(
        _mod_kernel,
        grid=(depth, n // tn),
        in_specs=[
            pl.BlockSpec((MOD_ROWS, d), lambda l, j: (0, 0)),
            pl.BlockSpec((1, d, tn), lambda l, j: (l, 0, j)),
            pl.BlockSpec((1, 1, tn), lambda l, j: (l, 0, j)),
        ],
        out_specs=pl.BlockSpec((1, MOD_ROWS, tn), lambda l, j: (l, 0, j)),
        out_shape=jax.ShapeDtypeStruct((depth, MOD_ROWS, n), F32),
        compiler_params=_cparams(("arbitrary", "arbitrary")),
        name="adaln_mod",
    )(cvec, ada_w, ada_b.reshape(depth, 1, n))


def _norm_mod(x, g, sh, sc):
    ms = jnp.mean(x * x, axis=-1, keepdims=True)
    y = x * lax.rsqrt(ms + EPS) * g
    return y * (1.0 + sc) + sh


def _proj_kernel(x_ref, g_ref, sh_ref, sc_ref, w_ref, b_ref, *o_refs, splits, scales):
    h = _norm_mod(x_ref[...], g_ref[...], sh_ref[0], sc_ref[0]).astype(BF16)
    off = 0
    for o_ref, n, s in zip(o_refs, splits, scales):
        y = jnp.dot(h, w_ref[:, off:off + n], preferred_element_type=F32) + b_ref[:, off:off + n]
        if s != 1.0:
            y = y * s
        o_ref[...] = y.astype(o_ref.dtype)
        off += n


def _proj(x2d, g, mod3, row_of_tile, sh_chunk, sc_chunk, w, b, splits, dtypes, scales, tm):
    n_tok, d = x2d.shape
    n_out = w.shape[1]
    assert sum(splits) == n_out and n_tok % tm == 0
    return pl.pallas_call(
        functools.partial(_proj_kernel, splits=tuple(splits), scales=tuple(scales)),
        grid=(n_tok // tm,),
        in_specs=[
            pl.BlockSpec((tm, d), lambda i: (i, 0)),
            pl.BlockSpec((1, d), lambda i: (0, 0)),
            pl.BlockSpec((1, 1, d), lambda i: (row_of_tile(i), 0, sh_chunk)),
            pl.BlockSpec((1, 1, d), lambda i: (row_of_tile(i), 0, sc_chunk)),
            pl.BlockSpec((d, n_out), lambda i: (0, 0)),
            pl.BlockSpec((1, n_out), lambda i: (0, 0)),
        ],
        out_specs=[pl.BlockSpec((tm, n), lambda i: (i, 0)) for n in splits],
        out_shape=[jax.ShapeDtypeStruct((n_tok, n), dt) for n, dt in zip(splits, dtypes)],
        compiler_params=_cparams(("parallel",)),
        name="norm_mod_proj",
    )(x2d, g.reshape(1, d), mod3, mod3, w, b.reshape(1, n_out))


def _resid_kernel(a_ref, w_ref, b_ref, res_ref, gate_ref, o_ref):
    y = jnp.dot(a_ref[...], w_ref[...], preferred_element_type=F32) + b_ref[...]
    o_ref[...] = res_ref[...] + gate_ref[0] * y


def _resid_proj(a, w, b, res, mod3, row_of_tile, gate_chunk, tm):
    n_tok, k = a.shape
    d = w.shape[1]
    return pl.pallas_call(
        _resid_kernel,
        grid=(n_tok // tm,),
        in_specs=[
            pl.BlockSpec((tm, k), lambda i: (i, 0)),
            pl.BlockSpec((k, d), lambda i: (0, 0)),
            pl.BlockSpec((1, d), lambda i: (0, 0)),
            pl.BlockSpec((tm, d), lambda i: (i, 0)),
            pl.BlockSpec((1, 1, d), lambda i: (row_of_tile(i), 0, gate_chunk)),
        ],
        out_specs=pl.BlockSpec((tm, d), lambda i: (i, 0)),
        out_shape=jax.ShapeDtypeStruct((n_tok, d), F32),
        compiler_params=_cparams(("parallel",)),
        name="proj_residual",
    )(a, w, b.reshape(1, d), res, mod3)


HEAD_GROUP = 4


def _head_group_attention(qg, keys, vals, biases):
    n_q, width = qg.shape
    dh = width // HEAD_GROUP
    head_of_lane = lax.broadcasted_iota(jnp.int32, (n_q, width), 1) // dh
    qs = jnp.concatenate([jnp.where(head_of_lane == h, qg, jnp.zeros_like(qg)) for h in range(HEAD_GROUP)],
                         axis=0)
    s_parts = []
    for kk, bb in zip(keys, biases):
        s = lax.dot_general(qs, kk, (((1,), (1,)), ((), ())), preferred_element_type=F32)
        s_parts.append(s if bb is None else s + bb)
    m = s_parts[0].max(axis=-1, keepdims=True)
    for s in s_parts[1:]:
        m = jnp.maximum(m, s.max(axis=-1, keepdims=True))
    p_parts = [jnp.exp(s - m) for s in s_parts]
    l = p_parts[0].sum(axis=-1, keepdims=True)
    for p in p_parts[1:]:
        l = l + p.sum(axis=-1, keepdims=True)
    o = None
    for p, vv in zip(p_parts, vals):
        t = jnp.dot(p.astype(BF16), vv, preferred_element_type=F32)
        o = t if o is None else o + t
    o = o / l
    out = jnp.where(head_of_lane == 0, o[0:n_q], 0.0)
    for h in range(1, HEAD_GROUP):
        out = jnp.where(head_of_lane == h, o[h * n_q:(h + 1) * n_q], out)
    return out


def _na_kernel(q_ref, k_ref, v_ref, kc_ref, vc_ref, mb_ref, o_ref, *, rows, kh):
    r = pl.program_id(1)
    rs = jnp.clip(r - kh // 2, 0, rows - kh)
    k0 = pl.multiple_of(rs * GRID_W, GRID_W)
    n_win = kh * GRID_W
    width = HEAD_GROUP * (q_ref.shape[2] // NA_HEADS)

    for g in range(NA_HEADS // HEAD_GROUP):
        c0 = g * width
        qg = q_ref[0, :, c0:c0 + width]
        kw = k_ref[0, pl.ds(k0, n_win), c0:c0 + width]
        vw = v_ref[0, pl.ds(k0, n_win), c0:c0 + width]
        kc = kc_ref[0, :, c0:c0 + width]
        vc = vc_ref[0, :, c0:c0 + width]
        bias = mb_ref[0, g * HEAD_GROUP:(g + 1) * HEAD_GROUP].reshape(HEAD_GROUP * GRID_W, n_win)
        o = _head_group_attention(qg, [kw, kc], [vw, vc], [bias, None])
        o_ref[0, :, c0:c0 + width] = o.astype(o_ref.dtype)


def _window_bias(rpb, rows):
    kh = min(WIN_H, rows)
    cols = np.arange(GRID_W)
    col_start = np.clip(cols - WIN_W // 2, 0, GRID_W - WIN_W)
    ck = np.arange(GRID_W)
    in_win = (ck[None, :] >= col_start[:, None]) & (ck[None, :] < col_start[:, None] + WIN_W)
    dc = ck[None, :] - cols[:, None] + (WIN_W - 1)
    n_dc = 2 * WIN_W - 1
    sel = ((dc[None] == np.arange(n_dc)[:, None, None]) & in_win[None]).astype(np.float32)
    t = jnp.einsum("hrc,cqk->hrqk", rpb.astype(F32), jnp.asarray(sel), precision=HIGHEST)
    t = jnp.where(jnp.asarray(in_win)[None, None], t, NEG)
    per_delta = []
    for delta in range(kh):
        r0 = WIN_H - 1 - delta
        per_delta.append(jnp.transpose(t[:, r0:r0 + kh], (0, 2, 1, 3)).reshape(NA_HEADS, GRID_W, kh * GRID_W))
    return jnp.stack(per_delta)


def _na_attention(q, k, v, kc, vc, mb):
    b, l, d = q.shape
    c = kc.shape[1]
    rows = l // GRID_W
    kh = mb.shape[0]

    def delta(bi, r):
        return r - jnp.clip(r - kh // 2, 0, rows - kh)

    return pl.pallas_call(
        functools.partial(_na_kernel, rows=rows, kh=kh),
        grid=(b, rows),
        in_specs=[
            pl.BlockSpec((1, GRID_W, d), lambda bi, r: (bi, r, 0)),
            pl.BlockSpec((1, l, d), lambda bi, r: (bi, 0, 0), pipeline_mode=pl.Buffered(1)),
            pl.BlockSpec((1, l, d), lambda bi, r: (bi, 0, 0), pipeline_mode=pl.Buffered(1)),
            pl.BlockSpec((1, c, d), lambda bi, r: (bi, 0, 0)),
            pl.BlockSpec((1, c, d), lambda bi, r: (bi, 0, 0)),
            pl.BlockSpec((1, NA_HEADS, GRID_W, kh * GRID_W), lambda bi, r: (delta(bi, r), 0, 0, 0)),
        ],
        out_specs=pl.BlockSpec((1, GRID_W, d), lambda bi, r: (bi, r, 0)),
        out_shape=jax.ShapeDtypeStruct((b, l, d), BF16),
        compiler_params=_cparams(("parallel", "arbitrary")),
        name="na_attention",
    )(q, k, v, kc, vc, mb)


def _ctx_attn_kernel(q_ref, k_ref, v_ref, o_ref):
    width = HEAD_GROUP * (q_ref.shape[2] // NA_HEADS)

    def group(g, carry):
        c0 = pl.multiple_of(g * width, width)
        o = _head_group_attention(q_ref[0, :, pl.ds(c0, width)], [k_ref[0, :, pl.ds(c0, width)]],
                                  [v_ref[0, :, pl.ds(c0, width)]], [None])
        o_ref[0, :, pl.ds(c0, width)] = o.astype(o_ref.dtype)
        return carry

    lax.fori_loop(0, NA_HEADS // HEAD_GROUP, group, 0)


def _ctx_attention(qc, kc, vc):
    b, c, d = qc.shape
    spec = pl.BlockSpec((1, c, d), lambda bi: (bi, 0, 0))
    return pl.pallas_call(
        _ctx_attn_kernel,
        grid=(b,),
        in_specs=[spec, spec, spec],
        out_specs=spec,
        out_shape=jax.ShapeDtypeStruct((b, c, d), BF16),
        compiler_params=_cparams(("parallel",)),
        name="ctx_attention",
    )(qc, kc, vc)


GROUP_LANE0 = N_EXPERTS


def _src_specs(srcs, tm):
    d = srcs[0].shape[1]
    tiles = [s.shape[0] // tm for s in srcs]
    specs, first = [], 0
    for t in tiles:
        specs.append(pl.BlockSpec((tm, d), functools.partial(
            lambda i, first, t: (jnp.clip(i - first, 0, t - 1), 0), first=first, t=t)))
        first += t
    return specs, tiles


def _select_src(i, refs, tiles):
    x = refs[-1][...]
    first = sum(tiles[:-1])
    for ref, t in zip(refs[-2::-1], tiles[-2::-1]):
        x = jnp.where(i < first, ref[...], x)
        first -= t
    return x


def _router_kernel(*refs, tiles):
    n_src = len(tiles)
    x_refs = refs[:n_src]
    g_ref, sh_ref, sc_ref, wr_ref, br_ref, h_ref, oh_ref, wd_ref, cnt_ref = refs[n_src:]
    i = pl.program_id(0)
    h = _norm_mod(_select_src(i, x_refs, tiles), g_ref[...], sh_ref[0], sc_ref[0])
    h_ref[...] = h
    logits = jnp.dot(h, wr_ref[...], precision=HIGHEST, preferred_element_type=F32) + br_ref[...]
    tm = logits.shape[0]
    lane = lax.broadcasted_iota(jnp.int32, (tm, LANES), 1)
    big = jnp.int32(LANES)

    gmask = (lane >= GROUP_LANE0) & (lane < GROUP_LANE0 + N_GROUPS)
    gl = jnp.where(gmask, logits, NEG)
    gmax = gl.max(axis=-1, keepdims=True)
    gsel = jnp.where(gmask & (gl == gmax), lane, big).min(axis=-1, keepdims=True) - GROUP_LANE0
    g_w = 1.0 / jnp.where(gmask, jnp.exp(gl - gmax), 0.0).sum(axis=-1, keepdims=True)

    e0 = gsel * EXPERTS_PER_GROUP
    emask = (lane >= e0) & (lane < e0 + EXPERTS_PER_GROUP)
    el = jnp.where(emask, logits, NEG)
    v1 = el.max(axis=-1, keepdims=True)
    i1 = jnp.where(emask & (el == v1), lane, big).min(axis=-1, keepdims=True)
    el2 = jnp.where(lane == i1, NEG, el)
    v2 = el2.max(axis=-1, keepdims=True)
    i2 = jnp.where(emask & (lane != i1) & (el2 == v2), lane, big).min(axis=-1, keepdims=True)
    t = jnp.exp(v2 - v1)
    w1 = g_w / (1.0 + t)
    w2 = g_w * t / (1.0 + t)

    sel1 = lane == i1
    sel2 = lane == i2
    oh = jnp.where(sel1 | sel2, 1.0, 0.0)
    oh_ref[...] = oh.astype(oh_ref.dtype)
    wd_ref[...] = jnp.where(sel1, w1, jnp.where(sel2, w2, 0.0))

    @pl.when(i == 0)
    def _():
        cnt_ref[...] = jnp.zeros_like(cnt_ref)

    cnt_ref[...] += oh.sum(axis=0, keepdims=True)


def _router(srcs, g, mod3, row_of_tile, sh_chunk, sc_chunk, wr, br, tm):
    d = srcs[0].shape[1]
    src_specs, tiles = _src_specs(srcs, tm)
    n = sum(tiles) * tm
    tok = pl.BlockSpec((tm, LANES), lambda i: (i, 0))
    return pl.pallas_call(
        functools.partial(_router_kernel, tiles=tuple(tiles)),
        grid=(sum(tiles),),
        in_specs=src_specs + [
            pl.BlockSpec((1, d), lambda i: (0, 0)),
            pl.BlockSpec((1, 1, d), lambda i: (row_of_tile(i), 0, sh_chunk)),
            pl.BlockSpec((1, 1, d), lambda i: (row_of_tile(i), 0, sc_chunk)),
            pl.BlockSpec((d, LANES), lambda i: (0, 0)),
            pl.BlockSpec((1, LANES), lambda i: (0, 0)),
        ],
        out_specs=[pl.BlockSpec((tm, d), lambda i: (i, 0)), tok, tok,
                   pl.BlockSpec((1, LANES), lambda i: (0, 0))],
        out_shape=[jax.ShapeDtypeStruct((n, d), F32), jax.ShapeDtypeStruct((n, LANES), BF16),
                   jax.ShapeDtypeStruct((n, LANES), F32), jax.ShapeDtypeStruct((1, LANES), F32)],
        compiler_params=_cparams(("arbitrary",)),
        name="moe_router",
    )(*srcs, g.reshape(1, d), mod3, mod3, wr, br)


def _dest_kernel(start_ref, oh_ref, wd_ref, dest_ref, wsel_ref, carry_ref):
    i = pl.program_id(0)

    @pl.when(i == 0)
    def _():
        carry_ref[...] = jnp.zeros_like(carry_ref)

    oh = oh_ref[...]
    tm = oh.shape[0]
    row = lax.broadcasted_iota(jnp.int32, (tm, tm), 0)
    col = lax.broadcasted_iota(jnp.int32, (tm, tm), 1)
    tri = jnp.where(row > col, 1.0, 0.0).astype(BF16)
    base = jnp.dot(tri, oh, preferred_element_type=F32) + carry_ref[...] + start_ref[...]
    ohf = oh.astype(F32)
    lane = lax.broadcasted_iota(jnp.int32, (tm, LANES), 1)
    sel = ohf > 0.0
    ea = jnp.where(sel, lane, LANES).min(axis=-1, keepdims=True)
    eb = jnp.where(sel, lane, -1).max(axis=-1, keepdims=True)
    wd = wd_ref[...]
    pick = lambda e, val: jnp.where(lane == e, val, 0.0).sum(axis=-1, keepdims=True)
    dcols = jnp.where(lane == 0, pick(ea, base), jnp.where(lane == 1, pick(eb, base), 0.0))
    dest_ref[0] = dcols.T[0:2, :].astype(jnp.int32)
    two = lax.broadcasted_iota(jnp.int32, (tm, 2), 1) == 0
    wsel_ref[...] = jnp.where(two, pick(ea, wd), pick(eb, wd))
    carry_ref[...] += ohf.sum(axis=0, keepdims=True)


def _dest(start, oh, wd, tm):
    n = oh.shape[0]
    tok = pl.BlockSpec((tm, LANES), lambda i: (i, 0))
    two = pl.BlockSpec((tm, 2), lambda i: (i, 0))
    return pl.pallas_call(
        _dest_kernel,
        grid=(n // tm,),
        in_specs=[pl.BlockSpec((1, LANES), lambda i: (0, 0)), tok, tok],
        out_specs=[pl.BlockSpec((1, 2, tm), lambda i: (i, 0, 0)), two],
        out_shape=[jax.ShapeDtypeStruct((n // tm, 2, tm), jnp.int32), jax.ShapeDtypeStruct((n, 2), F32)],
        scratch_shapes=[pltpu.VMEM((1, LANES), F32)],
        compiler_params=_cparams(("arbitrary",)),
        name="moe_dest",
    )(start, oh, wd)


def _row_copy(src, s, dst, t, sem):
    return pltpu.make_async_copy(src.at[pl.ds(s, 1), :], dst.at[pl.ds(t, 1), :], sem)


def _scatter_kernel(dest_ref, h_ref, xs_ref, sem):
    tm = h_ref.shape[0]

    def body(r, carry):
        for k in range(2):
            _row_copy(h_ref, r, xs_ref, dest_ref[0, k, r], sem).start()
        return carry

    lax.fori_loop(0, tm, body, 0)
    for k in range(2):
        pltpu.make_async_copy(h_ref, xs_ref.at[pl.ds(0, tm), :], sem).wait()


def _scatter_rows(dest, h, tm):
    n, d = h.shape
    return pl.pallas_call(
        _scatter_kernel,
        grid=(n // tm,),
        in_specs=[
            pl.BlockSpec((1, 2, tm), lambda i: (i, 0, 0), memory_space=pltpu.SMEM),
            pl.BlockSpec((tm, d), lambda i: (i, 0)),
        ],
        out_specs=pl.BlockSpec(memory_space=pl.ANY),
        out_shape=jax.ShapeDtypeStruct((2 * n, d), F32),
        scratch_shapes=[pltpu.SemaphoreType.DMA(())],
        compiler_params=_cparams(("arbitrary",)),
        name="moe_scatter",
    )(dest,h)


def _gmm_kernel(vb_ref, ve_ref, lo_ref, hi_ref, x_ref, w1_ref, w3_ref, w2_ref, o_ref, w1b, w3b, w2b):
    v = pl.program_id(0)
    pv = jnp.maximum(v - 1, 0)

    @pl.when((v == 0) | (ve_ref[v] != ve_ref[pv]))
    def _():
        w1b[...] = w1_ref[0].astype(BF16)
        w3b[...] = w3_ref[0].astype(BF16)
        w2b[...] = w2_ref[0].astype(BF16)

    lo = lo_ref[v]
    hi = hi_ref[v]

    @pl.when(hi > lo)
    def _():
        xb = x_ref[...].astype(BF16)
        a = jnp.dot(xb, w1b[...], preferred_element_type=F32)
        g = jnp.dot(xb, w3b[...], preferred_element_type=F32)
        hdn = (a * jax.nn.sigmoid(a) * g).astype(BF16)
        y = jnp.dot(hdn, w2b[...], preferred_element_type=F32)
        rows = lax.broadcasted_iota(jnp.int32, (y.shape[0], 1), 0)
        mine = (rows >= lo) & (rows < hi)
        first = (v == 0) | (vb_ref[v] != vb_ref[pv])

        @pl.when(first)
        def _():
            o_ref[...] = jnp.where(mine, y, 0.0)

        @pl.when(jnp.logical_not(first))
        def _():
            o_ref[...] = jnp.where(mine, y, o_ref[...])


def _gmm(visits, xs, w1, w3, w2, bm):
    a, d = xs.shape
    f = w1.shape[2]
    n_vis = visits[0].shape[0]
    grid_spec = pltpu.PrefetchScalarGridSpec(
        num_scalar_prefetch=4,
        grid=(n_vis,),
        in_specs=[
            pl.BlockSpec((bm, d), lambda v, vb, ve, lo, hi: (vb[v], 0)),
            pl.BlockSpec((1, d, f), lambda v, vb, ve, lo, hi: (ve[v], 0, 0)),
            pl.BlockSpec((1, d, f), lambda v, vb, ve, lo, hi: (ve[v], 0, 0)),
            pl.BlockSpec((1, f, d), lambda v, vb, ve, lo, hi: (ve[v], 0, 0)),
        ],
        out_specs=pl.BlockSpec((bm, d), lambda v, vb, ve, lo, hi: (vb[v], 0)),
        scratch_shapes=[pltpu.VMEM((d, f), BF16), pltpu.VMEM((d, f), BF16), pltpu.VMEM((f, d), BF16)],
    )
    return pl.pallas_call(
        _gmm_kernel,
        grid_spec=grid_spec,
        out_shape=jax.ShapeDtypeStruct((a, d), F32),
        compiler_params=_cparams(("arbitrary",)),
        name="moe_experts",
    )(*visits, xs, w1, w3, w2)


def _visit_plan(counts, n_rows, bm):
    counts = counts.astype(jnp.int32)
    end = jnp.cumsum(counts)
    start = end - counts
    n_blocks = n_rows // bm
    n_vis = n_blocks + N_EXPERTS
    tiles = jnp.where(counts > 0, (end - 1) // bm - start // bm + 1, 0)
    vend = jnp.cumsum(tiles)
    vstart = vend - tiles
    v = jnp.arange(n_vis, dtype=jnp.int32)
    e = jnp.minimum(jnp.sum(vend[None, :] <= v[:, None], axis=1).astype(jnp.int32), N_EXPERTS - 1)
    valid = v < vend[-1]
    blk = jnp.where(valid, start[e] // bm + (v - vstart[e]), n_blocks - 1)
    lo = jnp.clip(start[e] - blk * bm, 0, bm)
    hi = jnp.clip(end[e] - blk * bm, 0, bm)
    e = jnp.where(valid, e, jnp.max(jnp.where(valid, e, 0)))
    lo = jnp.where(valid, lo, 0)
    hi = jnp.where(valid, hi, 0)
    return start, (blk.astype(jnp.int32), e.astype(jnp.int32), lo.astype(jnp.int32), hi.astype(jnp.int32))


def _combine_kernel(dest_ref, ys_ref, *refs, tiles, final_norm):
    n_src = len(tiles)
    x_refs = refs[:n_src]
    w_ref, gate_ref = refs[n_src:n_src + 2]
    rest = refs[n_src + 2:]
    fg_ref = rest[0] if final_norm else None
    o_refs = rest[-n_src - 2:-2]
    buf, sem = rest[-2:]
    i = pl.program_id(0)
    tm = x_refs[0].shape[0]

    def body(r, carry):
        for k in range(2):
            _row_copy(ys_ref, dest_ref[0, k, r], buf.at[k], r, sem).start()
        return carry

    lax.fori_loop(0, tm, body, 0)
    for k in range(2):
        pltpu.make_async_copy(ys_ref.at[pl.ds(0, tm), :], buf.at[k], sem).wait()
    w = w_ref[...]
    moe = w[:, 0:1] * buf[0] + w[:, 1:2] * buf[1]
    y = _select_src(i, x_refs, tiles) + gate_ref[0] * moe
    if final_norm:
        y = y * lax.rsqrt(jnp.mean(y * y, axis=-1, keepdims=True) + EPS) * fg_ref[...]

    first = 0
    for o_ref, t in zip(o_refs, tiles):
        @pl.when((i >= first) & (i < first + t))
        def _(o_ref=o_ref):
            o_ref[...] = y
        first += t


def _combine(dest, wsel, ys, srcs, mod3, row_of_tile, gate_chunk, final_g, tm):
    d = srcs[0].shape[1]
    src_specs, tiles = _src_specs(srcs, tm)
    final_norm = final_g is not None
    extra_specs = [pl.BlockSpec((1, d), lambda i: (0, 0))] if final_norm else []
    extra_args = [final_g.reshape(1, d)] if final_norm else []
    return pl.pallas_call(
        functools.partial(_combine_kernel, tiles=tuple(tiles), final_norm=final_norm),
        grid=(sum(tiles),),
        in_specs=[
            pl.BlockSpec((1, 2, tm), lambda i: (i, 0, 0), memory_space=pltpu.SMEM),
            pl.BlockSpec(memory_space=pl.ANY),
        ] + src_specs + [
            pl.BlockSpec((tm, 2), lambda i: (i, 0)),
            pl.BlockSpec((1, 1, d), lambda i: (row_of_tile(i), 0, gate_chunk)),
        ] + extra_specs,
        out_specs=list(src_specs),
        out_shape=[jax.ShapeDtypeStruct(s.shape, F32) for s in srcs],
        scratch_shapes=[pltpu.VMEM((2, tm, d), F32), pltpu.SemaphoreType.DMA(())],
        compiler_params=_cparams(("arbitrary",)),
        name="moe_combine",
    )(dest,ys, *srcs, wsel, mod3, *extra_args)


def _router_weights(w_group, b_group, w_expert, b_expert):
    d = w_group.shape[0]
    we = jnp.transpose(w_expert, (1, 0, 2)).reshape(d, N_EXPERTS)
    wr = jnp.concatenate([we, w_group, jnp.zeros((d, LANES - N_EXPERTS - N_GROUPS), F32)], axis=1)
    br = jnp.concatenate([b_expert.reshape(N_EXPERTS), b_group,
                          jnp.zeros((LANES - N_EXPERTS - N_GROUPS,), F32)]).reshape(1, LANES)
    return wr, br


def _moe_layer(srcs, g, mod3, row_of_tile, chunks, router_w, w1, w3, w2, final_g, tm, bm):
    sh_chunk, sc_chunk, gate_chunk = chunks
    wr, br = _router_weights(*router_w)
    h, oh, wd, cnt = _router(srcs, g, mod3, row_of_tile, sh_chunk, sc_chunk, wr, br, tm)
    n = h.shape[0]
    start, visits = _visit_plan(cnt[0, :N_EXPERTS], 2 * n, bm)
    start_row = jnp.zeros((1, LANES), F32).at[0, :N_EXPERTS].set(start.astype(F32))
    dest, wsel = _dest(start_row, oh, wd, tm)
    xs = _scatter_rows(dest, h, tm)
    ys = _gmm(visits, xs, w1, w3, w2, bm)
    return _combine(dest, wsel, ys, srcs, mod3, row_of_tile, gate_chunk, final_g, tm)


SUB = 8
HALO = 8


def _lru_kernel(u_ref, up_ref, un_ref, cw_ref, cb_ref, wg_ref, bg_ref, lam_ref, h0_ref,
                h_ref, hend_ref, ubuf, a_s, b_s, carry, *, tt, n_t, reverse):
    i = pl.program_id(2)
    ti = (n_t - 1 - i) if reverse else i
    cb = u_ref.shape[2]
    groups = tt // SUB

    @pl.when(i == 0)
    def _():
        carry[...] = h0_ref[0]

    u = u_ref[0]
    ubuf[0:HALO] = jnp.where(ti > 0, up_ref[0], 0.0)
    ubuf[HALO:HALO + tt] = u
    ubuf[HALO + tt:2 * HALO + tt] = jnp.where(ti < n_t - 1, un_ref[0], 0.0)
    cw = cw_ref[...]
    left = CONV_W // 2
    uc = cb_ref[...] + cw[left:left + 1] * u
    for kk in range(CONV_W):
        if kk != left:
            o = HALO + kk - left
            uc = uc + cw[kk:kk + 1] * ubuf[o:o + tt]

    ub = uc.astype(BF16)
    r = jax.nn.sigmoid(jnp.dot(ub, wg_ref[0, 0], preferred_element_type=F32) + bg_ref[0])
    ig = jax.nn.sigmoid(jnp.dot(ub, wg_ref[1, 0], preferred_element_type=F32) + bg_ref[1])
    lam = lam_ref[...]
    log_a = (LRU_C * r) * (-jnp.log1p(jnp.exp(-lam)))
    a = jnp.exp(log_a)
    b = jnp.sqrt(1.0 - a * a) * (ig * uc)

    a3 = a.reshape(groups, SUB, cb)
    b3 = b.reshape(groups, SUB, cb)
    sub = lax.broadcasted_iota(jnp.int32, (groups, SUB, cb), 1)
    s = 1
    while s < SUB:
        shift = SUB - s if reverse else s
        keep = (sub < SUB - s) if reverse else (sub >= s)
        a_sh = pltpu.roll(a3, shift, 1)
        b_sh = pltpu.roll(b3, shift, 1)
        b3 = jnp.where(keep, a3 * b_sh + b3, b3)
        a3 = jnp.where(keep, a3 * a_sh, a3)
        s *= 2
    a_s[...] = a3
    b_s[...] = b3

    def chain(gi, h):
        g = (groups - 1 - gi) if reverse else gi
        hg = b_s[g] + a_s[g] * h
        h_ref[0, pl.ds(pl.multiple_of(g * SUB, SUB), SUB), :] = hg
        return hg[0:1] if reverse else hg[SUB - 1:SUB]

    h_last = lax.fori_loop(0, groups, chain, carry[...], unroll=8)
    carry[...] = h_last

    @pl.when(i == n_t - 1)
    def _():
        hend_ref[0] = h_last


def _lru_scan(u, h0, conv_w, conv_b, wg, bg, lam, reverse, tt):
    b, t, w = u.shape
    cb = w // LRU_BLOCKS
    n_t = t // tt
    hb = tt // HALO
    tmap = (lambda i: n_t - 1 - i) if reverse else (lambda i: i)
    return pl.pallas_call(
        functools.partial(_lru_kernel, tt=tt, n_t=n_t, reverse=reverse),
        grid=(b, LRU_BLOCKS, n_t),
        in_specs=[
            pl.BlockSpec((1, tt, cb), lambda bi, n, i: (bi, tmap(i), n)),
            pl.BlockSpec((1, HALO, cb), lambda bi, n, i: (bi, jnp.maximum(tmap(i) * hb - 1, 0), n)),
            pl.BlockSpec((1, HALO, cb), lambda bi, n, i: (bi, jnp.minimum((tmap(i) + 1) * hb, t // HALO - 1), n)),
            pl.BlockSpec((CONV_W, cb), lambda bi, n, i: (0, n)),
            pl.BlockSpec((1, cb), lambda bi, n, i: (0, n)),
            pl.BlockSpec((2, 1, cb, cb), lambda bi, n, i: (0, n, 0, 0)),
            pl.BlockSpec((2, 1, cb), lambda bi, n, i: (0, 0, n)),
            pl.BlockSpec((1, cb), lambda bi, n, i: (0, n)),
            pl.BlockSpec((1, 1, cb), lambda bi, n, i: (bi, 0, n)),
        ],
        out_specs=[pl.BlockSpec((1, tt, cb), lambda bi, n, i: (bi, tmap(i), n)),
                   pl.BlockSpec((1, 1, cb), lambda bi, n, i: (bi, 0, n))],
        out_shape=[jax.ShapeDtypeStruct((b, t, w), F32), jax.ShapeDtypeStruct((b, 1, w), F32)],
        scratch_shapes=[pltpu.VMEM((tt + 2 * HALO, cb), F32), pltpu.VMEM((tt // SUB, SUB, cb), F32),
                        pltpu.VMEM((tt // SUB, SUB, cb), F32), pltpu.VMEM((1, cb), F32)],
        compiler_params=_cparams(("parallel", "parallel", "arbitrary")),
        name="rglru_scan_rev" if reverse else "rglru_scan_fwd",
    )(u, u, u, conv_w, conv_b.reshape(1, w), wg, bg.reshape(2, 1, w), lam.reshape(1, w), h0)


def _gelu_tanh(x):
    return x * (0.5 * (1.0 + jnp.tanh(0.7978845608028654 * (x + 0.044715 * (x * x * x)))))


def _lru_out_kernel(y_ref, hf_ref, hb_ref, w_ref, b_ref, res_ref, gate_ref, o_ref):
    a = (_gelu_tanh(y_ref[...].astype(F32)) * (hf_ref[...] + hb_ref[...])).astype(BF16)
    y = jnp.dot(a, w_ref[...], preferred_element_type=F32) + b_ref[...]
    o_ref[...] = res_ref[...] + gate_ref[0] * y


def _lru_out_proj(y, hf, hb, w, b, res, mod3, row_of_tile, gate_chunk, tm):
    n_tok, k = y.shape
    d = w.shape[1]
    tok = pl.BlockSpec((tm, k), lambda i: (i, 0))
    return pl.pallas_call(
        _lru_out_kernel,
        grid=(n_tok // tm,),
        in_specs=[
            tok, tok, tok,
            pl.BlockSpec((k, d), lambda i: (0, 0)),
            pl.BlockSpec((1, d), lambda i: (0, 0)),
            pl.BlockSpec((tm, d), lambda i: (i, 0)),
            pl.BlockSpec((1, 1, d), lambda i: (row_of_tile(i), 0, gate_chunk)),
        ],
        out_specs=pl.BlockSpec((tm, d), lambda i: (i, 0)),
        out_shape=jax.ShapeDtypeStruct((n_tok, d), F32),
        compiler_params=_cparams(("parallel",)),
        name="rglru_out_proj",
    )(y, hf, hb, w, b.reshape(1, d), res, mod3)


def _lru_states(ul, uc, conv_w, conv_b, w_gate, b_gate, lam, tt):
    wg = w_gate.astype(BF16)
    b, _, w = ul.shape
    zero = jnp.zeros((b, 1, w), F32)
    hs = []
    for dirn in range(2):
        rev = dirn == 1
        args = (conv_w, conv_b, wg[dirn], b_gate[dirn], lam[dirn], rev)
        _, h_end = _lru_scan(uc, zero, *args, uc.shape[1])
        h, _ = _lru_scan(ul, h_end, *args, tt)
        hs.append(h)
    return hs


TM = 512
BM = 256
TT = 512


def kernel(x, c, ctx, c_ctx, ada_w, ada_b, norm_g, na_w_qkv, na_b_qkv, na_rpb, na_w_o, na_b_o, lru_w_in, lru_b_in, lru_conv_w, lru_conv_b, lru_w_gate, lru_b_gate, lru_lambda, lru_w_o, lru_b_o, moe_w_group, moe_b_group, moe_w_expert, moe_b_expert, moe_w1, moe_w3, moe_w2, final_g):
    b, l, d = x.shape
    n_ctx = ctx.shape[1]
    n_l, n_c = b * l, b * n_ctx
    assert l % TM == 0 and n_ctx <= TM and TM % n_ctx == 0 and b + 1 <= MOD_ROWS
    ctx_row = b
    tiles_per_batch = l // TM
    lat_row = lambda i: i // tiles_per_batch
    ctx_tile_row = lambda i: ctx_row
    nl_tiles = n_l // TM
    both_row = lambda i: jnp.where(i < nl_tiles, i // tiles_per_batch, ctx_row)

    cvec = jnp.concatenate([c, c_ctx[None], jnp.zeros((MOD_ROWS - b - 1, d), F32)], axis=0)
    mod = _modulation(cvec, ada_w, ada_b)
    mod3 = [mod[i].reshape(MOD_ROWS, 1, N_MOD * d) for i in range(mod.shape[0])]
    xl = x.reshape(n_l, d)
    xc = ctx.reshape(n_c, d)

    w_qkv = na_w_qkv[0].astype(BF16)
    qk_scale = (d // NA_HEADS) ** -0.5
    q, k, v = _proj(xl, norm_g[0, 0], mod3[0], lat_row, 0, 1, w_qkv, na_b_qkv[0],
                    (d, d, d), (BF16,) * 3, (qk_scale, 1.0, 1.0), TM)
    qc, kc, vc = _proj(xc, norm_g[0, 0], mod3[0], ctx_tile_row, 0, 1, w_qkv, na_b_qkv[0],
                       (d, d, d), (BF16,) * 3, (qk_scale, 1.0, 1.0), n_ctx)
    to3 = lambda a, s: a.reshape(b, s, d)
    mb = _window_bias(na_rpb[0], l // GRID_W)
    o_l = _na_attention(to3(q, l), to3(k, l), to3(v, l), to3(kc, n_ctx), to3(vc, n_ctx), mb)
    o_c = _ctx_attention(to3(qc, n_ctx), to3(kc, n_ctx), to3(vc, n_ctx))
    w_o = na_w_o[0].astype(BF16)
    xl = _resid_proj(o_l.reshape(n_l, d), w_o, na_b_o[0], xl, mod3[0], lat_row, 2, TM)
    xc = _resid_proj(o_c.reshape(n_c, d), w_o, na_b_o[0], xc, mod3[0], ctx_tile_row, 2, n_ctx)
    xl, xc = _moe_layer([xl, xc], norm_g[0, 1], mod3[0], both_row, (3, 4, 5),
                        (moe_w_group[0], moe_b_group[0], moe_w_expert[0], moe_b_expert[0]),
                        moe_w1[0], moe_w3[0], moe_w2[0], None, TM, BM)

    w_in = lru_w_in[0].astype(BF16)
    lw = w_in.shape[1] // 2
    y_l, u_l = _proj(xl, norm_g[1, 0], mod3[1], lat_row, 0, 1, w_in, lru_b_in[0],
                     (lw, lw), (BF16, F32), (1.0, 1.0), TM)
    (u_c,) = _proj(xc, norm_g[1, 0], mod3[1], ctx_tile_row, 0, 1, w_in[:, lw:], lru_b_in[0, lw:],
                   (lw,), (F32,), (1.0,), n_ctx)
    hf, hb = _lru_states(u_l.reshape(b, l, lw), u_c.reshape(b, n_ctx, lw), lru_conv_w[0], lru_conv_b[0],
                         lru_w_gate[0], lru_b_gate[0], lru_lambda[0], TT)
    xl = _lru_out_proj(y_l, hf.reshape(n_l, lw), hb.reshape(n_l, lw), lru_w_o[0].astype(BF16), lru_b_o[0],
                       xl, mod3[1], lat_row, 2, TM)
    (out,) = _moe_layer([xl], norm_g[1, 1], mod3[1], lat_row, (3, 4, 5),
                        (moe_w_group[1], moe_b_group[1], moe_w_expert[1], moe_b_expert[1]),
                        moe_w1[1], moe_w3[1], moe_w2[1], final_g, TM, BM)
    return out.reshape(b, l, d)
```

```python
import functools

import jax
import jax.numpy as jnp
import numpy as np
from jax import lax
from jax.experimental import pallas as pl
from jax.experimental.pallas import tpu as pltpu

F32 = jnp.float32
BF16 = jnp.bfloat16

GRID_W = 64
N_MOD = 6
NA_HEADS = 16
WIN_H = 8
WIN_W = 16
LRU_BLOCKS = 4
CONV_W = 4
LRU_C = 8.0
N_GROUPS = 4
EXPERTS_PER_GROUP = 8
N_EXPERTS = N_GROUPS * EXPERTS_PER_GROUP
EPS = 1e-6

LANES = 128
MOD_ROWS = 16
NEG = -1e30
VMEM_LIMIT = 56 * 1024 * 1024
HIGHEST = lax.Precision.HIGHEST


def _cparams(sem, vmem=VMEM_LIMIT):
    return pltpu.CompilerParams(dimension_semantics=sem, vmem_limit_bytes=vmem)


def _mod_kernel(c_ref, w_ref, b_ref, o_ref):
    c = c_ref[...]
    s = c * jax.nn.sigmoid(c)
    o_ref[0] = jnp.dot(s, w_ref[0], precision=HIGHEST, preferred_element_type=F32) + b_ref[0]


def _modulation(cvec, ada_w, ada_b):
    depth, d, n = ada_w.shape
    tn = 1536
    return pl.pallas_call(
        _mod_kernel,
        grid=(depth, n // tn),
        in_specs=[
            pl.BlockSpec((MOD_ROWS, d), lambda l, j: (0, 0)),
            pl.BlockSpec((1, d, tn), lambda l, j: (l, 0, j)),
            pl.BlockSpec((1, 1, tn), lambda l, j: (l, 0, j)),
        ],
        out_specs=pl.BlockSpec((1, MOD_ROWS, tn), lambda l, j: (l, 0, j)),
        out_shape=jax.ShapeDtypeStruct((depth, MOD_ROWS, n), F32),
        compiler_params=_cparams(("arbitrary", "arbitrary")),
        name="adaln_mod",
    )(cvec, ada_w, ada_b.reshape(depth, 1, n))


def _norm_mod(x, g, sh, sc):
    ms = jnp.mean(x * x, axis=-1, keepdims=True)
    y = x * lax.rsqrt(ms + EPS) * g
    return y * (1.0 + sc) + sh


def _proj_kernel(x_ref, g_ref, sh_ref, sc_ref, w_ref, b_ref, *o_refs, splits, scales):
    h = _norm_mod(x_ref[...], g_ref[...], sh_ref[0], sc_ref[0]).astype(BF16)
    off = 0
    for o_ref, n, s in zip(o_refs, splits, scales):
        y = jnp.dot(h, w_ref[:, off:off + n], preferred_element_type=F32) + b_ref[:, off:off + n]
        if s != 1.0:
            y = y * s
        o_ref[...] = y.astype(o_ref.dtype)
        off += n


def _proj(x2d, g, mod3, row_of_tile, sh_chunk, sc_chunk, w, b, splits, dtypes, scales, tm):
    n_tok, d = x2d.shape
    n_out = w.shape[1]
    assert sum(splits) == n_out and n_tok % tm == 0
    return pl.pallas_call(
        functools.partial(_proj_kernel, splits=tuple(splits), scales=tuple(scales)),
        grid=(n_tok // tm,),
        in_specs=[
            pl.BlockSpec((tm, d), lambda i: (i, 0)),
            pl.BlockSpec((1, d), lambda i: (0, 0)),
            pl.BlockSpec((1, 1, d), lambda i: (row_of_tile(i), 0, sh_chunk)),
            pl.BlockSpec((1, 1, d), lambda i: (row_of_tile(i), 0, sc_chunk)),
            pl.BlockSpec((d, n_out), lambda i: (0, 0)),
            pl.BlockSpec((1, n_out), lambda i: (0, 0)),
        ],
        out_specs=[pl.BlockSpec((tm, n), lambda i: (i, 0)) for n in splits],
        out_shape=[jax.ShapeDtypeStruct((n_tok, n), dt) for n, dt in zip(splits, dtypes)],
        compiler_params=_cparams(("parallel",)),
        name="norm_mod_proj",
    )(x2d, g.reshape(1, d), mod3, mod3, w, b.reshape(1, n_out))


def _resid_kernel(a_ref, w_ref, b_ref, res_ref, gate_ref, o_ref):
    y = jnp.dot(a_ref[...], w_ref[...], preferred_element_type=F32) + b_ref[...]
    o_ref[...] = res_ref[...] + gate_ref[0] * y


def _resid_proj(a, w, b, res, mod3, row_of_tile, gate_chunk, tm):
    n_tok, k = a.shape
    d = w.shape[1]
    return pl.pallas_call(
        _resid_kernel,
        grid=(n_tok // tm,),
        in_specs=[
            pl.BlockSpec((tm, k), lambda i: (i, 0)),
            pl.BlockSpec((k, d), lambda i: (0, 0)),
            pl.BlockSpec((1, d), lambda i: (0, 0)),
            pl.BlockSpec((tm, d), lambda i: (i, 0)),
            pl.BlockSpec((1, 1, d), lambda i: (row_of_tile(i), 0, gate_chunk)),
        ],
        out_specs=pl.BlockSpec((tm, d), lambda i: (i, 0)),
        out_shape=jax.ShapeDtypeStruct((n_tok, d), F32),
        compiler_params=_cparams(("parallel",)),
        name="proj_residual",
    )(a, w, b.reshape(1, d), res, mod3)


HEAD_GROUP = 4


def _head_group_attention(qg, keys, vals, biases):
    n_q, width = qg.shape
    dh = width // HEAD_GROUP
    head_of_lane = lax.broadcasted_iota(jnp.int32, (n_q, width), 1) // dh
    qs = jnp.concatenate([jnp.where(head_of_lane == h, qg, jnp.zeros_like(qg)) for h in range(HEAD_GROUP)],
                         axis=0)
    s_parts = []
    for kk, bb in zip(keys, biases):
        s = lax.dot_general(qs, kk, (((1,), (1,)), ((), ())), preferred_element_type=F32)
        s_parts.append(s if bb is None else s + bb)
    m = s_parts[0].max(axis=-1, keepdims=True)
    for s in s_parts[1:]:
        m = jnp.maximum(m, s.max(axis=-1, keepdims=True))
    p_parts = [jnp.exp(s - m) for s in s_parts]
    l = p_parts[0].sum(axis=-1, keepdims=True)
    for p in p_parts[1:]:
        l = l + p.sum(axis=-1, keepdims=True)
    o = None
    for p, vv in zip(p_parts, vals):
        t = jnp.dot(p.astype(BF16), vv, preferred_element_type=F32)
        o = t if o is None else o + t
    o = o / l
    out = jnp.where(head_of_lane == 0, o[0:n_q], 0.0)
    for h in range(1, HEAD_GROUP):
        out = jnp.where(head_of_lane == h, o[h * n_q:(h + 1) * n_q], out)
    return out


def _na_kernel(q_ref, k_ref, v_ref, kc_ref, vc_ref, mb_ref, o_ref, *, rows, kh):
    r = pl.program_id(1)
    rs = jnp.clip(r - kh // 2, 0, rows - kh)
    k0 = pl.multiple_of(rs * GRID_W, GRID_W)
    n_win = kh * GRID_W
    width = HEAD_GROUP * (q_ref.shape[2] // NA_HEADS)

    for g in range(NA_HEADS // HEAD_GROUP):
        c0 = g * width
        qg = q_ref[0, :, c0:c0 + width]
        kw = k_ref[0, pl.ds(k0, n_win), c0:c0 + width]
        vw = v_ref[0, pl.ds(k0, n_win), c0:c0 + width]
        kc = kc_ref[0, :, c0:c0 + width]
        vc = vc_ref[0, :, c0:c0 + width]
        bias = mb_ref[0, g * HEAD_GROUP:(g + 1) * HEAD_GROUP].reshape(HEAD_GROUP * GRID_W, n_win)
        o = _head_group_attention(qg, [kw, kc], [vw, vc], [bias, None])
        o_ref[0, :, c0:c0 + width] = o.astype(o_ref.dtype)


def _window_bias(rpb, rows):
    kh = min(WIN_H, rows)
    cols = np.arange(GRID_W)
    col_start = np.clip(cols - WIN_W // 2, 0, GRID_W - WIN_W)
    ck = np.arange(GRID_W)
    in_win = (ck[None, :] >= col_start[:, None]) & (ck[None, :] < col_start[:, None] + WIN_W)
    dc = ck[None, :] - cols[:, None] + (WIN_W - 1)
    n_dc = 2 * WIN_W - 1
    sel = ((dc[None] == np.arange(n_dc)[:, None, None]) & in_win[None]).astype(np.float32)
    t = jnp.einsum("hrc,cqk->hrqk", rpb.astype(F32), jnp.asarray(sel), precision=HIGHEST)
    t = jnp.where(jnp.asarray(in_win)[None, None], t, NEG)
    per_delta = []
    for delta in range(kh):
        r0 = WIN_H - 1 - delta
        per_delta.append(jnp.transpose(t[:, r0:r0 + kh], (0, 2, 1, 3)).reshape(NA_HEADS, GRID_W, kh * GRID_W))
    return jnp.stack(per_delta)


def _na_attention(q, k, v, kc, vc, mb):
    b, l, d = q.shape
    c = kc.shape[1]
    rows = l // GRID_W
    kh = mb.shape[0]

    def delta(bi, r):
        return r - jnp.clip(r - kh // 2, 0, rows - kh)

    return pl.pallas_call(
        functools.partial(_na_kernel, rows=rows, kh=kh),
        grid=(b, rows),
        in_specs=[
            pl.BlockSpec((1, GRID_W, d), lambda bi, r: (bi, r, 0)),
            pl.BlockSpec((1, l, d), lambda bi, r: (bi, 0, 0), pipeline_mode=pl.Buffered(1)),
            pl.BlockSpec((1, l, d), lambda bi, r: (bi, 0, 0), pipeline_mode=pl.Buffered(1)),
            pl.BlockSpec((1, c, d), lambda bi, r: (bi, 0, 0)),
            pl.BlockSpec((1, c, d), lambda bi, r: (bi, 0, 0)),
            pl.BlockSpec((1, NA_HEADS, GRID_W, kh * GRID_W), lambda bi, r: (delta(bi, r), 0, 0, 0)),
        ],
        out_specs=pl.BlockSpec((1, GRID_W, d), lambda bi, r: (bi, r, 0)),
        out_shape=jax.ShapeDtypeStruct((b, l, d), BF16),
        compiler_params=_cparams(("parallel", "arbitrary")),
        name="na_attention",
    )(q, k, v, kc, vc, mb)


def _ctx_attn_kernel(q_ref, k_ref, v_ref, o_ref):
    width = HEAD_GROUP * (q_ref.shape[2] // NA_HEADS)

    def group(g, carry):
        c0 = pl.multiple_of(g * width, width)
        o = _head_group_attention(q_ref[0, :, pl.ds(c0, width)], [k_ref[0, :, pl.ds(c0, width)]],
                                  [v_ref[0, :, pl.ds(c0, width)]], [None])
        o_ref[0, :, pl.ds(c0, width)] = o.astype(o_ref.dtype)
        return carry

    lax.fori_loop(0, NA_HEADS // HEAD_GROUP, group, 0)


def _ctx_attention(qc, kc, vc):
    b, c, d = qc.shape
    spec = pl.BlockSpec((1, c, d), lambda bi: (bi, 0, 0))
    return pl.pallas_call(
        _ctx_attn_kernel,
        grid=(b,),
        in_specs=[spec, spec, spec],
        out_specs=spec,
        out_shape=jax.ShapeDtypeStruct((b, c, d), BF16),
        compiler_params=_cparams(("parallel",)),
        name="ctx_attention",
    )(qc, kc, vc)


GROUP_LANE0 = N_EXPERTS


def _src_specs(srcs, tm):
    d = srcs[0].shape[1]
    tiles = [s.shape[0] // tm for s in srcs]
    specs, first = [], 0
    for t in tiles:
        specs.append(pl.BlockSpec((tm, d), functools.partial(
            lambda i, first, t: (jnp.clip(i - first, 0, t - 1), 0), first=first, t=t)))
        first += t
    return specs, tiles


def _select_src(i, refs, tiles):
    x = refs[-1][...]
    first = sum(tiles[:-1])
    for ref, t in zip(refs[-2::-1], tiles[-2::-1]):
        x = jnp.where(i < first, ref[...], x)
        first -= t
    return x


TOK_ROWS = 8


def _store_token_major(ref, x):
    n, d = x.shape
    assert d == TOK_ROWS * LANES
    for j in range(TOK_ROWS):
        ref[pl.ds(j, n, stride=TOK_ROWS), :] = x[:, j * LANES:(j + 1) * LANES]


def _load_token_major(ref, n):
    return jnp.concatenate([ref[pl.ds(j, n, stride=TOK_ROWS), :] for j in range(TOK_ROWS)], axis=1)


def _router_kernel(*refs, tiles):
    n_src = len(tiles)
    x_refs = refs[:n_src]
    g_ref, sh_ref, sc_ref, wr_ref, br_ref, h_ref, oh_ref, wd_ref, cnt_ref = refs[n_src:]
    i = pl.program_id(0)
    h = _norm_mod(_select_src(i, x_refs, tiles), g_ref[...], sh_ref[0], sc_ref[0])
    _store_token_major(h_ref, h)
    logits = jnp.dot(h, wr_ref[...], precision=HIGHEST, preferred_element_type=F32) + br_ref[...]
    tm = logits.shape[0]
    lane = lax.broadcasted_iota(jnp.int32, (tm, LANES), 1)
    big = jnp.int32(LANES)

    gmask = (lane >= GROUP_LANE0) & (lane < GROUP_LANE0 + N_GROUPS)
    gl = jnp.where(gmask, logits, NEG)
    gmax = gl.max(axis=-1, keepdims=True)
    gsel = jnp.where(gmask & (gl == gmax), lane, big).min(axis=-1, keepdims=True) - GROUP_LANE0
    g_w = 1.0 / jnp.where(gmask, jnp.exp(gl - gmax), 0.0).sum(axis=-1, keepdims=True)

    e0 = gsel * EXPERTS_PER_GROUP
    emask = (lane >= e0) & (lane < e0 + EXPERTS_PER_GROUP)
    el = jnp.where(emask, logits, NEG)
    v1 = el.max(axis=-1, keepdims=True)
    i1 = jnp.where(emask & (el == v1), lane, big).min(axis=-1, keepdims=True)
    el2 = jnp.where(lane == i1, NEG, el)
    v2 = el2.max(axis=-1, keepdims=True)
    i2 = jnp.where(emask & (lane != i1) & (el2 == v2), lane, big).min(axis=-1, keepdims=True)
    t = jnp.exp(v2 - v1)
    w1 = g_w / (1.0 + t)
    w2 = g_w * t / (1.0 + t)

    sel1 = lane == i1
    sel2 = lane == i2
    oh = jnp.where(sel1 | sel2, 1.0, 0.0)
    oh_ref[...] = oh.astype(oh_ref.dtype)
    wd_ref[...] = jnp.where(sel1, w1, jnp.where(sel2, w2, 0.0))

    @pl.when(i == 0)
    def _():
        cnt_ref[...] = jnp.zeros_like(cnt_ref)

    cnt_ref[...] += oh.sum(axis=0, keepdims=True)


def _router(srcs, g, mod3, row_of_tile, sh_chunk, sc_chunk, wr, br, tm):
    d = srcs[0].shape[1]
    src_specs, tiles = _src_specs(srcs, tm)
    n = sum(tiles) * tm
    tok = pl.BlockSpec((tm, LANES), lambda i: (i, 0))
    return pl.pallas_call(
        functools.partial(_router_kernel, tiles=tuple(tiles)),
        grid=(sum(tiles),),
        in_specs=src_specs + [
            pl.BlockSpec((1, d), lambda i: (0, 0)),
            pl.BlockSpec((1, 1, d), lambda i: (row_of_tile(i), 0, sh_chunk)),
            pl.BlockSpec((1, 1, d), lambda i: (row_of_tile(i), 0, sc_chunk)),
            pl.BlockSpec((d, LANES), lambda i: (0, 0)),
            pl.BlockSpec((1, LANES), lambda i: (0, 0)),
        ],
        out_specs=[pl.BlockSpec((tm * TOK_ROWS, LANES), lambda i: (i, 0)), tok, tok,
                   pl.BlockSpec((1, LANES), lambda i: (0, 0))],
        out_shape=[jax.ShapeDtypeStruct((n * TOK_ROWS, LANES), F32), jax.ShapeDtypeStruct((n, LANES), BF16),
                   jax.ShapeDtypeStruct((n, LANES), F32), jax.ShapeDtypeStruct((1, LANES), F32)],
        compiler_params=_cparams(("arbitrary",)),
        name="moe_router",
    )(*srcs, g.reshape(1, d), mod3, mod3, wr, br)


def _dest_kernel(start_ref, oh_ref, wd_ref, dest_ref, wsel_ref, carry_ref):
    i = pl.program_id(0)

    @pl.when(i == 0)
    def _():
        carry_ref[...] = jnp.zeros_like(carry_ref)

    oh = oh_ref[...]
    tm = oh.shape[0]
    row = lax.broadcasted_iota(jnp.int32, (tm, tm), 0)
    col = lax.broadcasted_iota(jnp.int32, (tm, tm), 1)
    tri = jnp.where(row > col, 1.0, 0.0).astype(BF16)
    base = jnp.dot(tri, oh, preferred_element_type=F32) + carry_ref[...] + start_ref[...]
    ohf = oh.astype(F32)
    lane = lax.broadcasted_iota(jnp.int32, (tm, LANES), 1)
    sel = ohf > 0.0
    ea = jnp.where(sel, lane, LANES).min(axis=-1, keepdims=True)
    eb = jnp.where(sel, lane, -1).max(axis=-1, keepdims=True)
    wd = wd_ref[...]
    pick = lambda e, val: jnp.where(lane == e, val, 0.0).sum(axis=-1, keepdims=True)
    dcols = jnp.where(lane == 0, pick(ea, base), jnp.where(lane == 1, pick(eb, base), 0.0))
    dest_ref[0] = dcols.T[0:2, :].astype(jnp.int32)
    two = lax.broadcasted_iota(jnp.int32, (tm, 2), 1) == 0
    wsel_ref[...] = jnp.where(two, pick(ea, wd), pick(eb, wd))
    carry_ref[...] += ohf.sum(axis=0, keepdims=True)


def _dest(start, oh, wd, tm):
    n = oh.shape[0]
    tok = pl.BlockSpec((tm, LANES), lambda i: (i, 0))
    two = pl.BlockSpec((tm, 2), lambda i: (i, 0))
    return pl.pallas_call(
        _dest_kernel,
        grid=(n // tm,),
        in_specs=[pl.BlockSpec((1, LANES), lambda i: (0, 0)), tok, tok],
        out_specs=[pl.BlockSpec((1, 2, tm), lambda i: (i, 0, 0)), two],
        out_shape=[jax.ShapeDtypeStruct((n // tm, 2, tm), jnp.int32), jax.ShapeDtypeStruct((n, 2), F32)],
        scratch_shapes=[pltpu.VMEM((1, LANES), F32)],
        compiler_params=_cparams(("arbitrary",)),
        name="moe_dest",
    )(start, oh, wd)


def _tok_rows(t):
    return pl.ds(t * TOK_ROWS, TOK_ROWS)


def _wait_tokens(buf, n, sem):
    whole = buf.at[pl.ds(0, n * TOK_ROWS), :]
    pltpu.make_async_copy(whole, whole, sem).wait()


def _scatter_kernel(dest_ref, h_ref, xs_ref, sem):
    tm = h_ref.shape[0] // TOK_ROWS
    for r in range(tm):
        for k in range(2):
            pltpu.make_async_copy(h_ref.at[_tok_rows(r), :], xs_ref.at[dest_ref[0, k, r]], sem).start(priority=k)
    for k in range(2):
        _wait_tokens(h_ref, tm, sem)


def _scatter_rows(dest, h, tm):
    n = h.shape[0] // TOK_ROWS
    return pl.pallas_call(
        _scatter_kernel,
        grid=(n // tm,),
        in_specs=[
            pl.BlockSpec((1, 2, tm), lambda i: (i, 0, 0), memory_space=pltpu.SMEM),
            pl.BlockSpec((tm * TOK_ROWS, LANES), lambda i: (i, 0)),
        ],
        out_specs=pl.BlockSpec(memory_space=pl.ANY),
        out_shape=jax.ShapeDtypeStruct((2 * n, TOK_ROWS, LANES), F32),
        scratch_shapes=[pltpu.SemaphoreType.DMA(())],
        compiler_params=_cparams(("arbitrary",)),
        name="moe_scatter",
    )(dest, h)


def _gmm_kernel(vb_ref, ve_ref, lo_ref, hi_ref, x_ref, w1_ref, w3_ref, w2_ref, o_ref, w1b, w3b, w2b, *, bm):
    v = pl.program_id(0)
    pv = jnp.maximum(v - 1, 0)

    @pl.when((v == 0) | (ve_ref[v] != ve_ref[pv]))
    def _():
        w1b[...] = w1_ref[0, 0].astype(BF16)
        w3b[...] = w3_ref[0, 0].astype(BF16)
        w2b[...] = w2_ref[0, 0].astype(BF16)

    lo = lo_ref[v]
    hi = hi_ref[v]

    @pl.when(hi > lo)
    def _():
        xb = _load_token_major(x_ref, bm).astype(BF16)
        a = jnp.dot(xb, w1b[...], preferred_element_type=F32)
        g = jnp.dot(xb, w3b[...], preferred_element_type=F32)
        hdn = (a * jax.nn.sigmoid(a) * g).astype(BF16)
        y = jnp.dot(hdn, w2b[...], preferred_element_type=F32)
        rows = lax.broadcasted_iota(jnp.int32, (bm, 1), 0)
        mine = (rows >= lo) & (rows < hi)
        first = (v == 0) | (vb_ref[v] != vb_ref[pv])

        @pl.when(first)
        def _():
            _store_token_major(o_ref, jnp.where(mine, y, 0.0))

        @pl.when(jnp.logical_not(first))
        def _():
            _store_token_major(o_ref, jnp.where(mine, y, _load_token_major(o_ref, bm)))


def _gmm(visits, xs, layer, w1, w3, w2, bm):
    d, f = w1.shape[2:]
    n_vis = visits[0].shape[0]
    blk = pl.BlockSpec((bm * TOK_ROWS, LANES), lambda v, vb, ve, lo, hi: (vb[v], 0))
    grid_spec = pltpu.PrefetchScalarGridSpec(
        num_scalar_prefetch=4,
        grid=(n_vis,),
        in_specs=[
            blk,
            pl.BlockSpec((1, 1, d, f), lambda v, vb, ve, lo, hi: (layer, ve[v], 0, 0)),
            pl.BlockSpec((1, 1, d, f), lambda v, vb, ve, lo, hi: (layer, ve[v], 0, 0)),
            pl.BlockSpec((1, 1, f, d), lambda v, vb, ve, lo, hi: (layer, ve[v], 0, 0)),
        ],
        out_specs=blk,
        scratch_shapes=[pltpu.VMEM((d, f), BF16), pltpu.VMEM((d, f), BF16), pltpu.VMEM((f, d), BF16)],
    )
    return pl.pallas_call(
        functools.partial(_gmm_kernel, bm=bm),
        grid_spec=grid_spec,
        out_shape=jax.ShapeDtypeStruct(xs.shape, F32),
        compiler_params=_cparams(("arbitrary",)),
        name="moe_experts",
    )(*visits, xs, w1, w3, w2)


def _visit_plan(counts, n_rows, bm):
    counts = counts.astype(jnp.int32)
    end = jnp.cumsum(counts)
    start = end - counts
    n_blocks = n_rows // bm
    n_vis = n_blocks + N_EXPERTS
    tiles = jnp.where(counts > 0, (end - 1) // bm - start // bm + 1, 0)
    vend = jnp.cumsum(tiles)
    vstart = vend - tiles
    v = jnp.arange(n_vis, dtype=jnp.int32)[:, None]
    owns = (vstart[None, :] <= v) & (v < vend[None, :])
    take = lambda a: jnp.sum(jnp.where(owns, a[None, :], 0), axis=1)
    valid = v[:, 0] < vend[-1]
    last_e = jnp.max(jnp.where(counts > 0, jnp.arange(N_EXPERTS, dtype=jnp.int32), 0))
    e = jnp.where(valid, take(jnp.arange(N_EXPERTS, dtype=jnp.int32)), last_e)
    blk = jnp.where(valid, take(start // bm - vstart) + v[:, 0], n_blocks - 1)
    lo = jnp.where(valid, jnp.clip(take(start) - blk * bm, 0, bm), 0)
    hi = jnp.where(valid, jnp.clip(take(end) - blk * bm, 0, bm), 0)
    return start, (blk.astype(jnp.int32), e.astype(jnp.int32), lo.astype(jnp.int32), hi.astype(jnp.int32))


def _combine_kernel(dest_ref, ys_ref, *refs, tiles, final_norm):
    n_src = len(tiles)
    x_refs = refs[:n_src]
    w_ref, gate_ref = refs[n_src:n_src + 2]
    rest = refs[n_src + 2:]
    fg_ref = rest[0] if final_norm else None
    o_refs = rest[-n_src - 2:-2]
    buf, sem = rest[-2:]
    i = pl.program_id(0)
    tm = x_refs[0].shape[0]
    for r in range(tm):
        for k in range(2):
            pltpu.make_async_copy(ys_ref.at[dest_ref[0, k, r]], buf.at[k, _tok_rows(r), :], sem).start(priority=k)
    for k in range(2):
        _wait_tokens(buf.at[k], tm, sem)
    w = w_ref[...]
    moe = w[:, 0:1] * _load_token_major(buf.at[0], tm) + w[:, 1:2] * _load_token_major(buf.at[1], tm)
    y = _select_src(i, x_refs, tiles) + gate_ref[0] * moe
    if final_norm:
        y = y * lax.rsqrt(jnp.mean(y * y, axis=-1, keepdims=True) + EPS) * fg_ref[...]

    first = 0
    for o_ref, t in zip(o_refs, tiles):
        @pl.when((i >= first) & (i < first + t))
        def _(o_ref=o_ref):
            o_ref[...] = y
        first += t


def _combine(dest, wsel, ys, srcs, mod3, row_of_tile, gate_chunk, final_g, tm):
    d = srcs[0].shape[1]
    src_specs, tiles = _src_specs(srcs, tm)
    final_norm = final_g is not None
    extra_specs = [pl.BlockSpec((1, d), lambda i: (0, 0))] if final_norm else []
    extra_args = [final_g.reshape(1, d)] if final_norm else []
    return pl.pallas_call(
        functools.partial(_combine_kernel, tiles=tuple(tiles), final_norm=final_norm),
        grid=(sum(tiles),),
        in_specs=[
            pl.BlockSpec((1, 2, tm), lambda i: (i, 0, 0), memory_space=pltpu.SMEM),
            pl.BlockSpec(memory_space=pl.ANY),
        ] + src_specs + [
            pl.BlockSpec((tm, 2), lambda i: (i, 0)),
            pl.BlockSpec((1, 1, d), lambda i: (row_of_tile(i), 0, gate_chunk)),
        ] + extra_specs,
        out_specs=list(src_specs),
        out_shape=[jax.ShapeDtypeStruct(s.shape, F32) for s in srcs],
        scratch_shapes=[pltpu.VMEM((2, tm * TOK_ROWS, LANES), F32), pltpu.SemaphoreType.DMA(())],
        compiler_params=_cparams(("arbitrary",)),
        name="moe_combine",
    )(dest, ys, *srcs, wsel, mod3, *extra_args)


def _router_weights(w_group, b_group, w_expert, b_expert):
    d = w_group.shape[0]
    we = jnp.transpose(w_expert, (1, 0, 2)).reshape(d, N_EXPERTS)
    wr = jnp.concatenate([we, w_group, jnp.zeros((d, LANES - N_EXPERTS - N_GROUPS), F32)], axis=1)
    br = jnp.concatenate([b_expert.reshape(N_EXPERTS), b_group,
                          jnp.zeros((LANES - N_EXPERTS - N_GROUPS,), F32)]).reshape(1, LANES)
    return wr, br


def _moe_layer(srcs, g, mod3, row_of_tile, chunks, router_w, layer, w1, w3, w2, final_g, tm, bm):
    sh_chunk, sc_chunk, gate_chunk = chunks
    wr, br = _router_weights(*router_w)
    h, oh, wd, cnt = _router(srcs, g, mod3, row_of_tile, sh_chunk, sc_chunk, wr, br, tm)
    n = oh.shape[0]
    start, visits = _visit_plan(cnt[0, :N_EXPERTS], 2 * n, bm)
    start_row = jnp.zeros((1, LANES), F32).at[0, :N_EXPERTS].set(start.astype(F32))
    dest, wsel = _dest(start_row, oh, wd, tm)
    xs = _scatter_rows(dest, h, tm)
    ys = _gmm(visits, xs.reshape(2 * n * TOK_ROWS, LANES), layer, w1, w3, w2, bm)
    return _combine(dest, wsel, ys.reshape(2 * n, TOK_ROWS, LANES), srcs, mod3, row_of_tile, gate_chunk,
                    final_g, tm)


SUB = 8
HALO = 8


def _lru_kernel(u_ref, up_ref, un_ref, cw_ref, cb_ref, wg_ref, bg_ref, lam_ref, h0_ref,
                h_ref, hend_ref, ubuf, a_s, b_s, carry, *, tt, n_t, reverse):
    i = pl.program_id(2)
    ti = (n_t - 1 - i) if reverse else i
    cb = u_ref.shape[2]
    groups = tt // SUB

    @pl.when(i == 0)
    def _():
        carry[...] = h0_ref[0]

    u = u_ref[0]
    ubuf[0:HALO] = jnp.where(ti > 0, up_ref[0], 0.0)
    ubuf[HALO:HALO + tt] = u
    ubuf[HALO + tt:2 * HALO + tt] = jnp.where(ti < n_t - 1, un_ref[0], 0.0)
    cw = cw_ref[...]
    left = CONV_W // 2
    uc = cb_ref[...] + cw[left:left + 1] * u
    for kk in range(CONV_W):
        if kk != left:
            o = HALO + kk - left
            uc = uc + cw[kk:kk + 1] * ubuf[o:o + tt]

    ub = uc.astype(BF16)
    r = jax.nn.sigmoid(jnp.dot(ub, wg_ref[0, 0], preferred_element_type=F32) + bg_ref[0])
    ig = jax.nn.sigmoid(jnp.dot(ub, wg_ref[1, 0], preferred_element_type=F32) + bg_ref[1])
    lam = lam_ref[...]
    log_a = (LRU_C * r) * (-jnp.log1p(jnp.exp(-lam)))
    a = jnp.exp(log_a)
    b = jnp.sqrt(1.0 - a * a) * (ig * uc)

    a3 = a.reshape(groups, SUB, cb)
    b3 = b.reshape(groups, SUB, cb)
    sub = lax.broadcasted_iota(jnp.int32, (groups, SUB, cb), 1)
    s = 1
    while s < SUB:
        shift = SUB - s if reverse else s
        keep = (sub < SUB - s) if reverse else (sub >= s)
        a_sh = pltpu.roll(a3, shift, 1)
        b_sh = pltpu.roll(b3, shift, 1)
        b3 = jnp.where(keep, a3 * b_sh + b3, b3)
        a3 = jnp.where(keep, a3 * a_sh, a3)
        s *= 2
    a_s[...] = a3
    b_s[...] = b3

    def chain(gi, h):
        g = (groups - 1 - gi) if reverse else gi
        hg = b_s[g] + a_s[g] * h
        h_ref[0, pl.ds(pl.multiple_of(g * SUB, SUB), SUB), :] = hg
        return hg[0:1] if reverse else hg[SUB - 1:SUB]

    h_last = lax.fori_loop(0, groups, chain, carry[...], unroll=8)
    carry[...] = h_last

    @pl.when(i == n_t - 1)
    def _():
        hend_ref[0] = h_last


def _lru_scan(u, h0, conv_w, conv_b, wg, bg, lam, reverse, tt):
    b, t, w = u.shape
    cb = w // LRU_BLOCKS
    n_t = t // tt
    hb = tt // HALO
    tmap = (lambda i: n_t - 1 - i) if reverse else (lambda i: i)
    return pl.pallas_call(
        functools.partial(_lru_kernel, tt=tt, n_t=n_t, reverse=reverse),
        grid=(b, LRU_BLOCKS, n_t),
        in_specs=[
            pl.BlockSpec((1, tt, cb), lambda bi, n, i: (bi, tmap(i), n)),
            pl.BlockSpec((1, HALO, cb), lambda bi, n, i: (bi, jnp.maximum(tmap(i) * hb - 1, 0), n)),
            pl.BlockSpec((1, HALO, cb), lambda bi, n, i: (bi, jnp.minimum((tmap(i) + 1) * hb, t // HALO - 1), n)),
            pl.BlockSpec((CONV_W, cb), lambda bi, n, i: (0, n)),
            pl.BlockSpec((1, cb), lambda bi, n, i: (0, n)),
            pl.BlockSpec((2, 1, cb, cb), lambda bi, n, i: (0, n, 0, 0)),
            pl.BlockSpec((2, 1, cb), lambda bi, n, i: (0, 0, n)),
            pl.BlockSpec((1, cb), lambda bi, n, i: (0, n)),
            pl.BlockSpec((1, 1, cb), lambda bi, n, i: (bi, 0, n)),
        ],
        out_specs=[pl.BlockSpec((1, tt, cb), lambda bi, n, i: (bi, tmap(i), n)),
                   pl.BlockSpec((1, 1, cb), lambda bi, n, i: (bi, 0, n))],
        out_shape=[jax.ShapeDtypeStruct((b, t, w), F32), jax.ShapeDtypeStruct((b, 1, w), F32)],
        scratch_shapes=[pltpu.VMEM((tt + 2 * HALO, cb), F32), pltpu.VMEM((tt // SUB, SUB, cb), F32),
                        pltpu.VMEM((tt // SUB, SUB, cb), F32), pltpu.VMEM((1, cb), F32)],
        compiler_params=_cparams(("parallel", "parallel", "arbitrary")),
        name="rglru_scan_rev" if reverse else "rglru_scan_fwd",
    )(u, u, u, conv_w, conv_b.reshape(1, w), wg, bg.reshape(2, 1, w), lam.reshape(1, w), h0)


def _gelu_tanh(x):
    return x * (0.5 * (1.0 + jnp.tanh(0.7978845608028654 * (x + 0.044715 * (x * x * x)))))


def _lru_out_kernel(y_ref, hf_ref, hb_ref, w_ref, b_ref, res_ref, gate_ref, o_ref):
    a = (_gelu_tanh(y_ref[...].astype(F32)) * (hf_ref[...] + hb_ref[...])).astype(BF16)
    y = jnp.dot(a, w_ref[...], preferred_element_type=F32) + b_ref[...]
    o_ref[...] = res_ref[...] + gate_ref[0] * y


def _lru_out_proj(y, hf, hb, w, b, res, mod3, row_of_tile, gate_chunk, tm):
    n_tok, k = y.shape
    d = w.shape[1]
    tok = pl.BlockSpec((tm, k), lambda i: (i, 0))
    return pl.pallas_call(
        _lru_out_kernel,
        grid=(n_tok // tm,),
        in_specs=[
            tok, tok, tok,
            pl.BlockSpec((k, d), lambda i: (0, 0)),
            pl.BlockSpec((1, d), lambda i: (0, 0)),
            pl.BlockSpec((tm, d), lambda i: (i, 0)),
            pl.BlockSpec((1, 1, d), lambda i: (row_of_tile(i), 0, gate_chunk)),
        ],
        out_specs=pl.BlockSpec((tm, d), lambda i: (i, 0)),
        out_shape=jax.ShapeDtypeStruct((n_tok, d), F32),
        compiler_params=_cparams(("parallel",)),
        name="rglru_out_proj",
    )(y, hf, hb, w, b.reshape(1, d), res, mod3)


def _lru_states(ul, uc, conv_w, conv_b, w_gate, b_gate, lam, tt):
    wg = w_gate.astype(BF16)
    b, _, w = ul.shape
    zero = jnp.zeros((b, 1, w), F32)
    hs = []
    for dirn in range(2):
        rev = dirn == 1
        args = (conv_w, conv_b, wg[dirn], b_gate[dirn], lam[dirn], rev)
        _, h_end = _lru_scan(uc, zero, *args, uc.shape[1])
        h, _ = _lru_scan(ul, h_end, *args, tt)
        hs.append(h)
    return hs


TM = 512
BM = 256
TT = 512


def kernel(x, c, ctx, c_ctx, ada_w, ada_b, norm_g, na_w_qkv, na_b_qkv, na_rpb, na_w_o, na_b_o, lru_w_in, lru_b_in, lru_conv_w, lru_conv_b, lru_w_gate, lru_b_gate, lru_lambda, lru_w_o, lru_b_o, moe_w_group, moe_b_group, moe_w_expert, moe_b_expert, moe_w1, moe_w3, moe_w2, final_g):
    b, l, d = x.shape
    n_ctx = ctx.shape[1]
    n_l, n_c = b * l, b * n_ctx
    assert l % TM == 0 and n_ctx <= TM and TM % n_ctx == 0 and b + 1 <= MOD_ROWS
    ctx_row = b
    tiles_per_batch = l // TM
    lat_row = lambda i: i // tiles_per_batch
    ctx_tile_row = lambda i: ctx_row
    nl_tiles = n_l // TM
    both_row = lambda i: jnp.where(i < nl_tiles, i // tiles_per_batch, ctx_row)

    cvec = jnp.concatenate([c, c_ctx[None], jnp.zeros((MOD_ROWS - b - 1, d), F32)], axis=0)
    mod = _modulation(cvec, ada_w, ada_b)
    mod3 = [mod[i].reshape(MOD_ROWS, 1, N_MOD * d) for i in range(mod.shape[0])]
    xl = x.reshape(n_l, d)
    xc = ctx.reshape(n_c, d)

    w_qkv = na_w_qkv[0].astype(BF16)
    qk_scale = (d // NA_HEADS) ** -0.5
    q, k, v = _proj(xl, norm_g[0, 0], mod3[0], lat_row, 0, 1, w_qkv, na_b_qkv[0],
                    (d, d, d), (BF16,) * 3, (qk_scale, 1.0, 1.0), TM)
    qc, kc, vc = _proj(xc, norm_g[0, 0], mod3[0], ctx_tile_row, 0, 1, w_qkv, na_b_qkv[0],
                       (d, d, d), (BF16,) * 3, (qk_scale, 1.0, 1.0), n_ctx)
    to3 = lambda a, s: a.reshape(b, s, d)
    mb = _window_bias(na_rpb[0], l // GRID_W)
    o_l = _na_attention(to3(q, l), to3(k, l), to3(v, l), to3(kc, n_ctx), to3(vc, n_ctx), mb)
    o_c = _ctx_attention(to3(qc, n_ctx), to3(kc, n_ctx), to3(vc, n_ctx))
    w_o = na_w_o[0].astype(BF16)
    xl = _resid_proj(o_l.reshape(n_l, d), w_o, na_b_o[0], xl, mod3[0], lat_row, 2, TM)
    xc = _resid_proj(o_c.reshape(n_c, d), w_o, na_b_o[0], xc, mod3[0], ctx_tile_row, 2, n_ctx)
    xl, xc = _moe_layer([xl, xc], norm_g[0, 1], mod3[0], both_row, (3, 4, 5),
                        (moe_w_group[0], moe_b_group[0], moe_w_expert[0], moe_b_expert[0]),
                        0, moe_w1, moe_w3, moe_w2, None, TM, BM)

    w_in = lru_w_in[0].astype(BF16)
    lw = w_in.shape[1] // 2
    y_l, u_l = _proj(xl, norm_g[1, 0], mod3[1], lat_row, 0, 1, w_in, lru_b_in[0],
                     (lw, lw), (BF16, F32), (1.0, 1.0), TM)
    (u_c,) = _proj(xc, norm_g[1, 0], mod3[1], ctx_tile_row, 0, 1, w_in[:, lw:], lru_b_in[0, lw:],
                   (lw,), (F32,), (1.0,), n_ctx)
    hf, hb = _lru_states(u_l.reshape(b, l, lw), u_c.reshape(b, n_ctx, lw), lru_conv_w[0], lru_conv_b[0],
                         lru_w_gate[0], lru_b_gate[0], lru_lambda[0], TT)
    xl = _lru_out_proj(y_l, hf.reshape(n_l, lw), hb.reshape(n_l, lw), lru_w_o[0].astype(BF16), lru_b_o[0],
                       xl, mod3[1], lat_row, 2, TM)
    (out,) = _moe_layer([xl], norm_g[1, 1], mod3[1], lat_row, (3, 4, 5),
                        (moe_w_group[1], moe_b_group[1], moe_w_expert[1], moe_b_expert[1]),
                        1, moe_w1, moe_w3, moe_w2, final_g, TM, BM)
    return out.reshape(b, l, d)
```

```python
import functools

import jax
import jax.numpy as jnp
import numpy as np
from jax import lax
from jax.experimental import pallas as pl
from jax.experimental.pallas import tpu as pltpu

F32 = jnp.float32
BF16 = jnp.bfloat16

GRID_W = 64
N_MOD = 6
NA_HEADS = 16
WIN_H = 8
WIN_W = 16
LRU_BLOCKS = 4
CONV_W = 4
LRU_C = 8.0
N_GROUPS = 4
EXPERTS_PER_GROUP = 8
N_EXPERTS = N_GROUPS * EXPERTS_PER_GROUP
EPS = 1e-6

LANES = 128
MOD_ROWS = 16
NEG = -1e30
VMEM_LIMIT = 56 * 1024 * 1024
HIGHEST = lax.Precision.HIGHEST


def _cparams(sem, vmem=VMEM_LIMIT):
    return pltpu.CompilerParams(dimension_semantics=sem, vmem_limit_bytes=vmem)


def _mod_kernel(c_ref, w_ref, b_ref, o_ref):
    c = c_ref[...]
    s = c * jax.nn.sigmoid(c)
    o_ref[0] = jnp.dot(s, w_ref[0], precision=HIGHEST, preferred_element_type=F32) + b_ref[0]


def _modulation(cvec, ada_w, ada_b):
    depth, d, n = ada_w.shape
    tn = 1536
    return pl.pallas_call(
        _mod_kernel,
        grid=(depth, n // tn),
        in_specs=[
            pl.BlockSpec((MOD_ROWS, d), lambda l, j: (0, 0)),
            pl.BlockSpec((1, d, tn), lambda l, j: (l, 0, j)),
            pl.BlockSpec((1, 1, tn), lambda l, j: (l, 0, j)),
        ],
        out_specs=pl.BlockSpec((1, MOD_ROWS, tn), lambda l, j: (l, 0, j)),
        out_shape=jax.ShapeDtypeStruct((depth, MOD_ROWS, n), F32),
        compiler_params=_cparams(("arbitrary", "arbitrary")),
        name="adaln_mod",
    )(cvec, ada_w, ada_b.reshape(depth, 1, n))


def _norm_mod(x, g, sh, sc):
    ms = jnp.mean(x * x, axis=-1, keepdims=True)
    y = x * lax.rsqrt(ms + EPS) * g
    return y * (1.0 + sc) + sh


def _proj_kernel(x_ref, g_ref, sh_ref, sc_ref, w_ref, b_ref, *o_refs, splits, scales):
    h = _norm_mod(x_ref[...], g_ref[...], sh_ref[0], sc_ref[0]).astype(BF16)
    off = 0
    for o_ref, n, s in zip(o_refs, splits, scales):
        y = jnp.dot(h, w_ref[:, off:off + n], preferred_element_type=F32) + b_ref[:, off:off + n]
        if s != 1.0:
            y = y * s
        o_ref[...] = y.astype(o_ref.dtype)
        off += n


def _proj(x2d, g, mod3, row_of_tile, sh_chunk, sc_chunk, w, b, splits, dtypes, scales, tm):
    n_tok, d = x2d.shape
    n_out = w.shape[1]
    assert sum(splits) == n_out and n_tok % tm == 0
    return pl.pallas_call(
        functools.partial(_proj_kernel, splits=tuple(splits), scales=tuple(scales)),
        grid=(n_tok // tm,),
        in_specs=[
            pl.BlockSpec((tm, d), lambda i: (i, 0)),
            pl.BlockSpec((1, d), lambda i: (0, 0)),
            pl.BlockSpec((1, 1, d), lambda i: (row_of_tile(i), 0, sh_chunk)),
            pl.BlockSpec((1, 1, d), lambda i: (row_of_tile(i), 0, sc_chunk)),
            pl.BlockSpec((d, n_out), lambda i: (0, 0)),
            pl.BlockSpec((1, n_out), lambda i: (0, 0)),
        ],
        out_specs=[pl.BlockSpec((tm, n), lambda i: (i, 0)) for n in splits],
        out_shape=[jax.ShapeDtypeStruct((n_tok, n), dt) for n, dt in zip(splits, dtypes)],
        compiler_params=_cparams(("parallel",)),
        name="norm_mod_proj",
    )(x2d, g.reshape(1, d), mod3, mod3, w, b.reshape(1, n_out))


def _resid_kernel(a_ref, w_ref, b_ref, res_ref, gate_ref, o_ref):
    y = jnp.dot(a_ref[...], w_ref[...], preferred_element_type=F32) + b_ref[...]
    o_ref[...] = res_ref[...] + gate_ref[0] * y


def _resid_proj(a, w, b, res, mod3, row_of_tile, gate_chunk, tm):
    n_tok, k = a.shape
    d = w.shape[1]
    return pl.pallas_call(
        _resid_kernel,
        grid=(n_tok // tm,),
        in_specs=[
            pl.BlockSpec((tm, k), lambda i: (i, 0)),
            pl.BlockSpec((k, d), lambda i: (0, 0)),
            pl.BlockSpec((1, d), lambda i: (0, 0)),
            pl.BlockSpec((tm, d), lambda i: (i, 0)),
            pl.BlockSpec((1, 1, d), lambda i: (row_of_tile(i), 0, gate_chunk)),
        ],
        out_specs=pl.BlockSpec((tm, d), lambda i: (i, 0)),
        out_shape=jax.ShapeDtypeStruct((n_tok, d), F32),
        compiler_params=_cparams(("parallel",)),
        name="proj_residual",
    )(a, w, b.reshape(1, d), res, mod3)


HEAD_GROUP = 4


def _head_group_attention(qg, keys, vals, biases):
    n_q, width = qg.shape
    dh = width // HEAD_GROUP
    head_of_lane = lax.broadcasted_iota(jnp.int32, (n_q, width), 1) // dh
    qs = jnp.concatenate([jnp.where(head_of_lane == h, qg, jnp.zeros_like(qg)) for h in range(HEAD_GROUP)],
                         axis=0)
    s_parts = []
    for kk, bb in zip(keys, biases):
        s = lax.dot_general(qs, kk, (((1,), (1,)), ((), ())), preferred_element_type=F32)
        s_parts.append(s if bb is None else s + bb)
    m = s_parts[0].max(axis=-1, keepdims=True)
    for s in s_parts[1:]:
        m = jnp.maximum(m, s.max(axis=-1, keepdims=True))
    p_parts = [jnp.exp(s - m) for s in s_parts]
    l = p_parts[0].sum(axis=-1, keepdims=True)
    for p in p_parts[1:]:
        l = l + p.sum(axis=-1, keepdims=True)
    o = None
    for p, vv in zip(p_parts, vals):
        t = jnp.dot(p.astype(BF16), vv, preferred_element_type=F32)
        o = t if o is None else o + t
    o = o / l
    out = jnp.where(head_of_lane == 0, o[0:n_q], 0.0)
    for h in range(1, HEAD_GROUP):
        out = jnp.where(head_of_lane == h, o[h * n_q:(h + 1) * n_q], out)
    return out


NA_ROWS_PER_STEP = 4


def _na_kernel(q_ref, k_ref, v_ref, kc_ref, vc_ref, *rest, rows, kh):
    mb_refs, o_ref = rest[:-1], rest[-1]
    n_win = kh * GRID_W
    width = HEAD_GROUP * (q_ref.shape[2] // NA_HEADS)

    for j, mb_ref in enumerate(mb_refs):
        r = pl.program_id(1) * len(mb_refs) + j
        rs = jnp.clip(r - kh // 2, 0, rows - kh)
        k0 = pl.multiple_of(rs * GRID_W, GRID_W)
        q0 = j * GRID_W
        for g in range(NA_HEADS // HEAD_GROUP):
            c0 = g * width
            qg = q_ref[0, q0:q0 + GRID_W, c0:c0 + width]
            kw = k_ref[0, pl.ds(k0, n_win), c0:c0 + width]
            vw = v_ref[0, pl.ds(k0, n_win), c0:c0 + width]
            kc = kc_ref[0, :, c0:c0 + width]
            vc = vc_ref[0, :, c0:c0 + width]
            bias = mb_ref[0, g * HEAD_GROUP:(g + 1) * HEAD_GROUP].reshape(HEAD_GROUP * GRID_W, n_win)
            o = _head_group_attention(qg, [kw, kc], [vw, vc], [bias, None])
            o_ref[0, q0:q0 + GRID_W, c0:c0 + width] = o.astype(o_ref.dtype)


def _window_bias(rpb, rows):
    kh = min(WIN_H, rows)
    cols = np.arange(GRID_W)
    col_start = np.clip(cols - WIN_W // 2, 0, GRID_W - WIN_W)
    ck = np.arange(GRID_W)
    in_win = (ck[None, :] >= col_start[:, None]) & (ck[None, :] < col_start[:, None] + WIN_W)
    dc = ck[None, :] - cols[:, None] + (WIN_W - 1)
    n_dc = 2 * WIN_W - 1
    sel = ((dc[None] == np.arange(n_dc)[:, None, None]) & in_win[None]).astype(np.float32)
    t = jnp.einsum("hrc,cqk->hrqk", rpb.astype(F32), jnp.asarray(sel), precision=HIGHEST)
    t = jnp.where(jnp.asarray(in_win)[None, None], t, NEG)
    per_delta = []
    for delta in range(kh):
        r0 = WIN_H - 1 - delta
        per_delta.append(jnp.transpose(t[:, r0:r0 + kh], (0, 2, 1, 3)).reshape(NA_HEADS, GRID_W, kh * GRID_W))
    return jnp.stack(per_delta)


def _na_attention(q, k, v, kc, vc, mb):
    b, l, d = q.shape
    c = kc.shape[1]
    rows = l // GRID_W
    kh = mb.shape[0]

    rps = NA_ROWS_PER_STEP
    assert rows % rps == 0

    def bias_spec(j):
        def index(bi, s):
            r = s * rps + j
            return (r - jnp.clip(r - kh // 2, 0, rows - kh), 0, 0, 0)
        return pl.BlockSpec((1, NA_HEADS, GRID_W, kh * GRID_W), index)

    return pl.pallas_call(
        functools.partial(_na_kernel, rows=rows, kh=kh),
        grid=(b, rows // rps),
        in_specs=[
            pl.BlockSpec((1, rps * GRID_W, d), lambda bi, s: (bi, s, 0)),
            pl.BlockSpec((1, l, d), lambda bi, s: (bi, 0, 0), pipeline_mode=pl.Buffered(1)),
            pl.BlockSpec((1, l, d), lambda bi, s: (bi, 0, 0), pipeline_mode=pl.Buffered(1)),
            pl.BlockSpec((1, c, d), lambda bi, s: (bi, 0, 0)),
            pl.BlockSpec((1, c, d), lambda bi, s: (bi, 0, 0)),
        ] + [bias_spec(j) for j in range(rps)],
        out_specs=pl.BlockSpec((1, rps * GRID_W, d), lambda bi, s: (bi, s, 0)),
        out_shape=jax.ShapeDtypeStruct((b, l, d), BF16),
        compiler_params=_cparams(("parallel", "arbitrary")),
        name="na_attention",
    )(q, k, v, kc, vc, *([mb] * rps))


def _ctx_attn_kernel(q_ref, k_ref, v_ref, o_ref):
    width = HEAD_GROUP * (q_ref.shape[2] // NA_HEADS)

    def group(g, carry):
        c0 = pl.multiple_of(g * width, width)
        o = _head_group_attention(q_ref[0, :, pl.ds(c0, width)], [k_ref[0, :, pl.ds(c0, width)]],
                                  [v_ref[0, :, pl.ds(c0, width)]], [None])
        o_ref[0, :, pl.ds(c0, width)] = o.astype(o_ref.dtype)
        return carry

    lax.fori_loop(0, NA_HEADS // HEAD_GROUP, group, 0)


def _ctx_attention(qc, kc, vc):
    b, c, d = qc.shape
    spec = pl.BlockSpec((1, c, d), lambda bi: (bi, 0, 0))
    return pl.pallas_call(
        _ctx_attn_kernel,
        grid=(b,),
        in_specs=[spec, spec, spec],
        out_specs=spec,
        out_shape=jax.ShapeDtypeStruct((b, c, d), BF16),
        compiler_params=_cparams(("parallel",)),
        name="ctx_attention",
    )(qc, kc, vc)


GROUP_LANE0 = N_EXPERTS


def _src_specs(srcs, tm):
    d = srcs[0].shape[1]
    tiles = [s.shape[0] // tm for s in srcs]
    specs, first = [], 0
    for t in tiles:
        specs.append(pl.BlockSpec((tm, d), functools.partial(
            lambda i, first, t: (jnp.clip(i - first, 0, t - 1), 0), first=first, t=t)))
        first += t
    return specs, tiles


def _select_src(i, refs, tiles):
    x = refs[-1][...]
    first = sum(tiles[:-1])
    for ref, t in zip(refs[-2::-1], tiles[-2::-1]):
        x = jnp.where(i < first, ref[...], x)
        first -= t
    return x


TOK_ROWS = 8


def _store_token_major(ref, x):
    n, d = x.shape
    assert d == TOK_ROWS * LANES
    for j in range(TOK_ROWS):
        ref[pl.ds(j, n, stride=TOK_ROWS), :] = x[:, j * LANES:(j + 1) * LANES]


def _load_token_major(ref, n):
    return jnp.concatenate([ref[pl.ds(j, n, stride=TOK_ROWS), :] for j in range(TOK_ROWS)], axis=1)


def _router_kernel(*refs, tiles):
    n_src = len(tiles)
    x_refs = refs[:n_src]
    g_ref, sh_ref, sc_ref, wr_ref, br_ref, h_ref, oh_ref, wd_ref, cnt_ref = refs[n_src:]
    i = pl.program_id(0)
    h = _norm_mod(_select_src(i, x_refs, tiles), g_ref[...], sh_ref[0], sc_ref[0])
    _store_token_major(h_ref, h)
    logits = jnp.dot(h, wr_ref[...], precision=HIGHEST, preferred_element_type=F32) + br_ref[...]
    tm = logits.shape[0]
    lane = lax.broadcasted_iota(jnp.int32, (tm, LANES), 1)
    big = jnp.int32(LANES)

    gmask = (lane >= GROUP_LANE0) & (lane < GROUP_LANE0 + N_GROUPS)
    gl = jnp.where(gmask, logits, NEG)
    gmax = gl.max(axis=-1, keepdims=True)
    gsel = jnp.where(gmask & (gl == gmax), lane, big).min(axis=-1, keepdims=True) - GROUP_LANE0
    g_w = 1.0 / jnp.where(gmask, jnp.exp(gl - gmax), 0.0).sum(axis=-1, keepdims=True)

    e0 = gsel * EXPERTS_PER_GROUP
    emask = (lane >= e0) & (lane < e0 + EXPERTS_PER_GROUP)
    el = jnp.where(emask, logits, NEG)
    v1 = el.max(axis=-1, keepdims=True)
    i1 = jnp.where(emask & (el == v1), lane, big).min(axis=-1, keepdims=True)
    el2 = jnp.where(lane == i1, NEG, el)
    v2 = el2.max(axis=-1, keepdims=True)
    i2 = jnp.where(emask & (lane != i1) & (el2 == v2), lane, big).min(axis=-1, keepdims=True)
    t = jnp.exp(v2 - v1)
    w1 = g_w / (1.0 + t)
    w2 = g_w * t / (1.0 + t)

    sel1 = lane == i1
    sel2 = lane == i2
    oh = jnp.where(sel1 | sel2, 1.0, 0.0)
    oh_ref[...] = oh.astype(oh_ref.dtype)
    wd_ref[...] = jnp.where(sel1, w1, jnp.where(sel2, w2, 0.0))

    @pl.when(i == 0)
    def _():
        cnt_ref[...] = jnp.zeros_like(cnt_ref)

    cnt_ref[...] += oh.sum(axis=0, keepdims=True)


def _router(srcs, g, mod3, row_of_tile, sh_chunk, sc_chunk, wr, br, tm):
    d = srcs[0].shape[1]
    src_specs, tiles = _src_specs(srcs, tm)
    n = sum(tiles) * tm
    tok = pl.BlockSpec((tm, LANES), lambda i: (i, 0))
    return pl.pallas_call(
        functools.partial(_router_kernel, tiles=tuple(tiles)),
        grid=(sum(tiles),),
        in_specs=src_specs + [
            pl.BlockSpec((1, d), lambda i: (0, 0)),
            pl.BlockSpec((1, 1, d), lambda i: (row_of_tile(i), 0, sh_chunk)),
            pl.BlockSpec((1, 1, d), lambda i: (row_of_tile(i), 0, sc_chunk)),
            pl.BlockSpec((d, LANES), lambda i: (0, 0)),
            pl.BlockSpec((1, LANES), lambda i: (0, 0)),
        ],
        out_specs=[pl.BlockSpec((tm * TOK_ROWS, LANES), lambda i: (i, 0)), tok, tok,
                   pl.BlockSpec((1, LANES), lambda i: (0, 0))],
        out_shape=[jax.ShapeDtypeStruct((n * TOK_ROWS, LANES), F32), jax.ShapeDtypeStruct((n, LANES), BF16),
                   jax.ShapeDtypeStruct((n, LANES), F32), jax.ShapeDtypeStruct((1, LANES), F32)],
        compiler_params=_cparams(("arbitrary",)),
        name="moe_router",
    )(*srcs, g.reshape(1, d), mod3, mod3, wr, br)


def _dest_kernel(start_ref, oh_ref, wd_ref, dest_ref, wsel_ref, carry_ref):
    i = pl.program_id(0)

    @pl.when(i == 0)
    def _():
        carry_ref[...] = jnp.zeros_like(carry_ref)

    oh = oh_ref[...]
    tm = oh.shape[0]
    row = lax.broadcasted_iota(jnp.int32, (tm, tm), 0)
    col = lax.broadcasted_iota(jnp.int32, (tm, tm), 1)
    tri = jnp.where(row > col, 1.0, 0.0).astype(BF16)
    base = jnp.dot(tri, oh, preferred_element_type=F32) + carry_ref[...] + start_ref[...]
    ohf = oh.astype(F32)
    lane = lax.broadcasted_iota(jnp.int32, (tm, LANES), 1)
    sel = ohf > 0.0
    ea = jnp.where(sel, lane, LANES).min(axis=-1, keepdims=True)
    eb = jnp.where(sel, lane, -1).max(axis=-1, keepdims=True)
    wd = wd_ref[...]
    pick = lambda e, val: jnp.where(lane == e, val, 0.0).sum(axis=-1, keepdims=True)
    dcols = jnp.where(lane == 0, pick(ea, base), jnp.where(lane == 1, pick(eb, base), 0.0))
    dest_ref[0] = dcols.T[0:2, :].astype(jnp.int32)
    two = lax.broadcasted_iota(jnp.int32, (tm, 2), 1) == 0
    wsel_ref[...] = jnp.where(two, pick(ea, wd), pick(eb, wd))
    carry_ref[...] += ohf.sum(axis=0, keepdims=True)


def _dest(start, oh, wd, tm):
    n = oh.shape[0]
    tok = pl.BlockSpec((tm, LANES), lambda i: (i, 0))
    two = pl.BlockSpec((tm, 2), lambda i: (i, 0))
    return pl.pallas_call(
        _dest_kernel,
        grid=(n // tm,),
        in_specs=[pl.BlockSpec((1, LANES), lambda i: (0, 0)), tok, tok],
        out_specs=[pl.BlockSpec((1, 2, tm), lambda i: (i, 0, 0)), two],
        out_shape=[jax.ShapeDtypeStruct((n // tm, 2, tm), jnp.int32), jax.ShapeDtypeStruct((n, 2), F32)],
        scratch_shapes=[pltpu.VMEM((1, LANES), F32)],
        compiler_params=_cparams(("arbitrary",)),
        name="moe_dest",
    )(start, oh, wd)


def _tok_rows(t):
    return pl.ds(t * TOK_ROWS, TOK_ROWS)


def _wait_tokens(buf, n, sem):
    whole = buf.at[pl.ds(0, n * TOK_ROWS), :]
    pltpu.make_async_copy(whole, whole, sem).wait()


def _scatter_kernel(dest_ref, h_ref, xs_ref, sem):
    tm = h_ref.shape[0] // TOK_ROWS
    for r in range(tm):
        for k in range(2):
            pltpu.make_async_copy(h_ref.at[_tok_rows(r), :], xs_ref.at[dest_ref[0, k, r]], sem).start(priority=k)
    for k in range(2):
        _wait_tokens(h_ref, tm, sem)


def _scatter_rows(dest, h, tm):
    n = h.shape[0] // TOK_ROWS
    return pl.pallas_call(
        _scatter_kernel,
        grid=(n // tm,),
        in_specs=[
            pl.BlockSpec((1, 2, tm), lambda i: (i, 0, 0), memory_space=pltpu.SMEM),
            pl.BlockSpec((tm * TOK_ROWS, LANES), lambda i: (i, 0)),
        ],
        out_specs=pl.BlockSpec(memory_space=pl.ANY),
        out_shape=jax.ShapeDtypeStruct((2 * n, TOK_ROWS, LANES), F32),
        scratch_shapes=[pltpu.SemaphoreType.DMA(())],
        compiler_params=_cparams(("arbitrary",)),
        name="moe_scatter",
    )(dest, h)


def _gmm_kernel(vb_ref, ve_ref, lo_ref, hi_ref, x_ref, w1_ref, w3_ref, w2_ref, o_ref, w1b, w3b, w2b, *, bm):
    v = pl.program_id(0)
    pv = jnp.maximum(v - 1, 0)

    @pl.when((v == 0) | (ve_ref[v] != ve_ref[pv]))
    def _():
        w1b[...] = w1_ref[0, 0].astype(BF16)
        w3b[...] = w3_ref[0, 0].astype(BF16)
        w2b[...] = w2_ref[0, 0].astype(BF16)

    lo = lo_ref[v]
    hi = hi_ref[v]

    @pl.when(hi > lo)
    def _():
        xb = _load_token_major(x_ref, bm).astype(BF16)
        a = jnp.dot(xb, w1b[...], preferred_element_type=F32)
        g = jnp.dot(xb, w3b[...], preferred_element_type=F32)
        hdn = (a * jax.nn.sigmoid(a) * g).astype(BF16)
        y = jnp.dot(hdn, w2b[...], preferred_element_type=F32)
        rows = lax.broadcasted_iota(jnp.int32, (bm, 1), 0)
        mine = (rows >= lo) & (rows < hi)
        first = (v == 0) | (vb_ref[v] != vb_ref[pv])

        @pl.when(first)
        def _():
            _store_token_major(o_ref, jnp.where(mine, y, 0.0))

        @pl.when(jnp.logical_not(first))
        def _():
            _store_token_major(o_ref, jnp.where(mine, y, _load_token_major(o_ref, bm)))


def _gmm(visits, xs, layer, w1, w3, w2, bm):
    d, f = w1.shape[2:]
    n_vis = visits[0].shape[0]
    blk = pl.BlockSpec((bm * TOK_ROWS, LANES), lambda v, vb, ve, lo, hi: (vb[v], 0))
    grid_spec = pltpu.PrefetchScalarGridSpec(
        num_scalar_prefetch=4,
        grid=(n_vis,),
        in_specs=[
            blk,
            pl.BlockSpec((1, 1, d, f), lambda v, vb, ve, lo, hi: (layer, ve[v], 0, 0)),
            pl.BlockSpec((1, 1, d, f), lambda v, vb, ve, lo, hi: (layer, ve[v], 0, 0)),
            pl.BlockSpec((1, 1, f, d), lambda v, vb, ve, lo, hi: (layer, ve[v], 0, 0)),
        ],
        out_specs=blk,
        scratch_shapes=[pltpu.VMEM((d, f), BF16), pltpu.VMEM((d, f), BF16), pltpu.VMEM((f, d), BF16)],
    )
    return pl.pallas_call(
        functools.partial(_gmm_kernel, bm=bm),
        grid_spec=grid_spec,
        out_shape=jax.ShapeDtypeStruct(xs.shape, F32),
        compiler_params=_cparams(("arbitrary",)),
        name="moe_experts",
    )(*visits, xs, w1, w3, w2)


def _visit_plan(counts, n_rows, bm):
    counts = counts.astype(jnp.int32)
    end = jnp.cumsum(counts)
    start = end - counts
    n_blocks = n_rows // bm
    n_vis = n_blocks + N_EXPERTS
    tiles = jnp.where(counts > 0, (end - 1) // bm - start // bm + 1, 0)
    vend = jnp.cumsum(tiles)
    vstart = vend - tiles
    v = jnp.arange(n_vis, dtype=jnp.int32)[:, None]
    owns = (vstart[None, :] <= v) & (v < vend[None, :])
    take = lambda a: jnp.sum(jnp.where(owns, a[None, :], 0), axis=1)
    valid = v[:, 0] < vend[-1]
    last_e = jnp.max(jnp.where(counts > 0, jnp.arange(N_EXPERTS, dtype=jnp.int32), 0))
    e = jnp.where(valid, take(jnp.arange(N_EXPERTS, dtype=jnp.int32)), last_e)
    blk = jnp.where(valid, take(start // bm - vstart) + v[:, 0], n_blocks - 1)
    lo = jnp.where(valid, jnp.clip(take(start) - blk * bm, 0, bm), 0)
    hi = jnp.where(valid, jnp.clip(take(end) - blk * bm, 0, bm), 0)
    return start, (blk.astype(jnp.int32), e.astype(jnp.int32), lo.astype(jnp.int32), hi.astype(jnp.int32))


def _combine_kernel(dest_ref, dest_next_ref, ys_ref, *refs, tiles, final_norm):
    n_src = len(tiles)
    x_refs = refs[:n_src]
    w_ref, gate_ref = refs[n_src:n_src + 2]
    rest = refs[n_src + 2:]
    fg_ref = rest[0] if final_norm else None
    o_refs = rest[-n_src - 2:-2]
    buf, sems = rest[-2:]
    i = pl.program_id(0)
    n_steps = pl.num_programs(0)
    tm = x_refs[0].shape[0]

    def gather(d_ref, slot):
        for r in range(tm):
            for k in range(2):
                pltpu.make_async_copy(ys_ref.at[d_ref[0, k, r]], buf.at[slot, k, _tok_rows(r), :],
                                      sems.at[slot]).start(priority=k)

    @pl.when(i == 0)
    def _():
        gather(dest_ref, 0)

    @pl.when(i + 1 < n_steps)
    def _():
        gather(dest_next_ref, (i + 1) % 2)

    slot = i % 2
    for k in range(2):
        _wait_tokens(buf.at[slot, k], tm, sems.at[slot])
    w = w_ref[...]
    moe = (w[:, 0:1] * _load_token_major(buf.at[slot, 0], tm)
           + w[:, 1:2] * _load_token_major(buf.at[slot, 1], tm))
    y = _select_src(i, x_refs, tiles) + gate_ref[0] * moe
    if final_norm:
        y = y * lax.rsqrt(jnp.mean(y * y, axis=-1, keepdims=True) + EPS) * fg_ref[...]

    first = 0
    for o_ref, t in zip(o_refs, tiles):
        @pl.when((i >= first) & (i < first + t))
        def _(o_ref=o_ref):
            o_ref[...] = y
        first += t


def _combine(dest, wsel, ys, srcs, mod3, row_of_tile, gate_chunk, final_g, tm):
    d = srcs[0].shape[1]
    src_specs, tiles = _src_specs(srcs, tm)
    final_norm = final_g is not None
    extra_specs = [pl.BlockSpec((1, d), lambda i: (0, 0))] if final_norm else []
    extra_args = [final_g.reshape(1, d)] if final_norm else []
    n_steps = sum(tiles)
    return pl.pallas_call(
        functools.partial(_combine_kernel, tiles=tuple(tiles), final_norm=final_norm),
        grid=(n_steps,),
        in_specs=[
            pl.BlockSpec((1, 2, tm), lambda i: (i, 0, 0), memory_space=pltpu.SMEM),
            pl.BlockSpec((1, 2, tm), lambda i: (jnp.minimum(i + 1, n_steps - 1), 0, 0), memory_space=pltpu.SMEM),
            pl.BlockSpec(memory_space=pl.ANY),
        ] + src_specs + [
            pl.BlockSpec((tm, 2), lambda i: (i, 0)),
            pl.BlockSpec((1, 1, d), lambda i: (row_of_tile(i), 0, gate_chunk)),
        ] + extra_specs,
        out_specs=list(src_specs),
        out_shape=[jax.ShapeDtypeStruct(s.shape, F32) for s in srcs],
        scratch_shapes=[pltpu.VMEM((2, 2, tm * TOK_ROWS, LANES), F32), pltpu.SemaphoreType.DMA((2,))],
        compiler_params=_cparams(("arbitrary",)),
        name="moe_combine",
    )(dest, dest, ys, *srcs, wsel, mod3, *extra_args)


def _router_weights(w_group, b_group, w_expert, b_expert):
    d = w_group.shape[0]
    we = jnp.transpose(w_expert, (1, 0, 2)).reshape(d, N_EXPERTS)
    wr = jnp.concatenate([we, w_group, jnp.zeros((d, LANES - N_EXPERTS - N_GROUPS), F32)], axis=1)
    br = jnp.concatenate([b_expert.reshape(N_EXPERTS), b_group,
                          jnp.zeros((LANES - N_EXPERTS - N_GROUPS,), F32)]).reshape(1, LANES)
    return wr, br


def _moe_layer(srcs, g, mod3, row_of_tile, chunks, router_w, layer, w1, w3, w2, final_g, tm, bm):
    sh_chunk, sc_chunk, gate_chunk = chunks
    wr, br = _router_weights(*router_w)
    h, oh, wd, cnt = _router(srcs, g, mod3, row_of_tile, sh_chunk, sc_chunk, wr, br, tm)
    n = oh.shape[0]
    start, visits = _visit_plan(cnt[0, :N_EXPERTS], 2 * n, bm)
    start_row = jnp.zeros((1, LANES), F32).at[0, :N_EXPERTS].set(start.astype(F32))
    dest, wsel = _dest(start_row, oh, wd, tm)
    xs = _scatter_rows(dest, h, tm)
    ys = _gmm(visits, xs.reshape(2 * n * TOK_ROWS, LANES), layer, w1, w3, w2, bm)
    return _combine(dest, wsel, ys.reshape(2 * n, TOK_ROWS, LANES), srcs, mod3, row_of_tile, gate_chunk,
                    final_g, tm)


SUB = 8
HALO = 8


def _lru_kernel(u_ref, up_ref, un_ref, cw_ref, cb_ref, wg_ref, bg_ref, lam_ref, h0_ref,
                h_ref, hend_ref, ubuf, a_s, b_s, carry, *, tt, n_t, reverse):
    i = pl.program_id(2)
    ti = (n_t - 1 - i) if reverse else i
    cb = u_ref.shape[2]
    groups = tt // SUB

    @pl.when(i == 0)
    def _():
        carry[...] = h0_ref[0]

    u = u_ref[0]
    ubuf[0:HALO] = jnp.where(ti > 0, up_ref[0], 0.0)
    ubuf[HALO:HALO + tt] = u
    ubuf[HALO + tt:2 * HALO + tt] = jnp.where(ti < n_t - 1, un_ref[0], 0.0)
    cw = cw_ref[...]
    left = CONV_W // 2
    uc = cb_ref[...] + cw[left:left + 1] * u
    for kk in range(CONV_W):
        if kk != left:
            o = HALO + kk - left
            uc = uc + cw[kk:kk + 1] * ubuf[o:o + tt]

    ub = uc.astype(BF16)
    r = jax.nn.sigmoid(jnp.dot(ub, wg_ref[0, 0], preferred_element_type=F32) + bg_ref[0])
    ig = jax.nn.sigmoid(jnp.dot(ub, wg_ref[1, 0], preferred_element_type=F32) + bg_ref[1])
    lam = lam_ref[...]
    log_a = (LRU_C * r) * (-jnp.log1p(jnp.exp(-lam)))
    a = jnp.exp(log_a)
    b = jnp.sqrt(1.0 - a * a) * (ig * uc)

    a3 = a.reshape(groups, SUB, cb)
    b3 = b.reshape(groups, SUB, cb)
    sub = lax.broadcasted_iota(jnp.int32, (groups, SUB, cb), 1)
    s = 1
    while s < SUB:
        shift = SUB - s if reverse else s
        keep = (sub < SUB - s) if reverse else (sub >= s)
        a_sh = pltpu.roll(a3, shift, 1)
        b_sh = pltpu.roll(b3, shift, 1)
        b3 = jnp.where(keep, a3 * b_sh + b3, b3)
        a3 = jnp.where(keep, a3 * a_sh, a3)
        s *= 2
    a_s[...] = a3
    b_s[...] = b3

    def chain(gi, h):
        g = (groups - 1 - gi) if reverse else gi
        hg = b_s[g] + a_s[g] * h
        h_ref[0, pl.ds(pl.multiple_of(g * SUB, SUB), SUB), :] = hg
        return hg[0:1] if reverse else hg[SUB - 1:SUB]

    h_last = lax.fori_loop(0, groups, chain, carry[...], unroll=8)
    carry[...] = h_last

    @pl.when(i == n_t - 1)
    def _():
        hend_ref[0] = h_last


def _lru_scan(u, h0, conv_w, conv_b, wg, bg, lam, reverse, tt):
    b, t, w = u.shape
    cb = w // LRU_BLOCKS
    n_t = t // tt
    hb = tt // HALO
    tmap = (lambda i: n_t - 1 - i) if reverse else (lambda i: i)
    return pl.pallas_call(
        functools.partial(_lru_kernel, tt=tt, n_t=n_t, reverse=reverse),
        grid=(b, LRU_BLOCKS, n_t),
        in_specs=[
            pl.BlockSpec((1, tt, cb), lambda bi, n, i: (bi, tmap(i), n)),
            pl.BlockSpec((1, HALO, cb), lambda bi, n, i: (bi, jnp.maximum(tmap(i) * hb - 1, 0), n)),
            pl.BlockSpec((1, HALO, cb), lambda bi, n, i: (bi, jnp.minimum((tmap(i) + 1) * hb, t // HALO - 1), n)),
            pl.BlockSpec((CONV_W, cb), lambda bi, n, i: (0, n)),
            pl.BlockSpec((1, cb), lambda bi, n, i: (0, n)),
            pl.BlockSpec((2, 1, cb, cb), lambda bi, n, i: (0, n, 0, 0)),
            pl.BlockSpec((2, 1, cb), lambda bi, n, i: (0, 0, n)),
            pl.BlockSpec((1, cb), lambda bi, n, i: (0, n)),
            pl.BlockSpec((1, 1, cb), lambda bi, n, i: (bi, 0, n)),
        ],
        out_specs=[pl.BlockSpec((1, tt, cb), lambda bi, n, i: (bi, tmap(i), n)),
                   pl.BlockSpec((1, 1, cb), lambda bi, n, i: (bi, 0, n))],
        out_shape=[jax.ShapeDtypeStruct((b, t, w), F32), jax.ShapeDtypeStruct((b, 1, w), F32)],
        scratch_shapes=[pltpu.VMEM((tt + 2 * HALO, cb), F32), pltpu.VMEM((tt // SUB, SUB, cb), F32),
                        pltpu.VMEM((tt // SUB, SUB, cb), F32), pltpu.VMEM((1, cb), F32)],
        compiler_params=_cparams(("parallel", "parallel", "arbitrary")),
        name="rglru_scan_rev" if reverse else "rglru_scan_fwd",
    )(u, u, u, conv_w, conv_b.reshape(1, w), wg, bg.reshape(2, 1, w), lam.reshape(1, w), h0)


def _gelu_tanh(x):
    return x * (0.5 * (1.0 + jnp.tanh(0.7978845608028654 * (x + 0.044715 * (x * x * x)))))


def _lru_out_kernel(y_ref, hf_ref, hb_ref, w_ref, b_ref, res_ref, gate_ref, o_ref):
    a = (_gelu_tanh(y_ref[...].astype(F32)) * (hf_ref[...] + hb_ref[...])).astype(BF16)
    y = jnp.dot(a, w_ref[...], preferred_element_type=F32) + b_ref[...]
    o_ref[...] = res_ref[...] + gate_ref[0] * y


def _lru_out_proj(y, hf, hb, w, b, res, mod3, row_of_tile, gate_chunk, tm):
    n_tok, k = y.shape
    d = w.shape[1]
    tok = pl.BlockSpec((tm, k), lambda i: (i, 0))
    return pl.pallas_call(
        _lru_out_kernel,
        grid=(n_tok // tm,),
        in_specs=[
            tok, tok, tok,
            pl.BlockSpec((k, d), lambda i: (0, 0)),
            pl.BlockSpec((1, d), lambda i: (0, 0)),
            pl.BlockSpec((tm, d), lambda i: (i, 0)),
            pl.BlockSpec((1, 1, d), lambda i: (row_of_tile(i), 0, gate_chunk)),
        ],
        out_specs=pl.BlockSpec((tm, d), lambda i: (i, 0)),
        out_shape=jax.ShapeDtypeStruct((n_tok, d), F32),
        compiler_params=_cparams(("parallel",)),
        name="rglru_out_proj",
    )(y, hf, hb, w, b.reshape(1, d), res, mod3)


def _lru_states(ul, uc, conv_w, conv_b, w_gate, b_gate, lam, tt):
    wg = w_gate.astype(BF16)
    b, _, w = ul.shape
    zero = jnp.zeros((b, 1, w), F32)
    hs = []
    for dirn in range(2):
        rev = dirn == 1
        args = (conv_w, conv_b, wg[dirn], b_gate[dirn], lam[dirn], rev)
        _, h_end = _lru_scan(uc, zero, *args, uc.shape[1])
        h, _ = _lru_scan(ul, h_end, *args, tt)
        hs.append(h)
    return hs


TM = 512
BM = 512
TT = 512


def kernel(x, c, ctx, c_ctx, ada_w, ada_b, norm_g, na_w_qkv, na_b_qkv, na_rpb, na_w_o, na_b_o, lru_w_in, lru_b_in, lru_conv_w, lru_conv_b, lru_w_gate, lru_b_gate, lru_lambda, lru_w_o, lru_b_o, moe_w_group, moe_b_group, moe_w_expert, moe_b_expert, moe_w1, moe_w3, moe_w2, final_g):
    b, l, d = x.shape
    n_ctx = ctx.shape[1]
    n_l, n_c = b * l, b * n_ctx
    assert l % TM == 0 and n_ctx <= TM and TM % n_ctx == 0 and b + 1 <= MOD_ROWS
    ctx_row = b
    tiles_per_batch = l // TM
    lat_row = lambda i: i // tiles_per_batch
    ctx_tile_row = lambda i: ctx_row
    nl_tiles = n_l // TM
    both_row = lambda i: jnp.where(i < nl_tiles, i // tiles_per_batch, ctx_row)

    cvec = jnp.concatenate([c, c_ctx[None], jnp.zeros((MOD_ROWS - b - 1, d), F32)], axis=0)
    mod = _modulation(cvec, ada_w, ada_b)
    mod3 = [mod[i].reshape(MOD_ROWS, 1, N_MOD * d) for i in range(mod.shape[0])]
    xl = x.reshape(n_l, d)
    xc = ctx.reshape(n_c, d)

    w_qkv = na_w_qkv[0].astype(BF16)
    qk_scale = (d // NA_HEADS) ** -0.5
    q, k, v = _proj(xl, norm_g[0, 0], mod3[0], lat_row, 0, 1, w_qkv, na_b_qkv[0],
                    (d, d, d), (BF16,) * 3, (qk_scale, 1.0, 1.0), TM)
    qc, kc, vc = _proj(xc, norm_g[0, 0], mod3[0], ctx_tile_row, 0, 1, w_qkv, na_b_qkv[0],
                       (d, d, d), (BF16,) * 3, (qk_scale, 1.0, 1.0), n_ctx)
    to3 = lambda a, s: a.reshape(b, s, d)
    mb = _window_bias(na_rpb[0], l // GRID_W)
    o_l = _na_attention(to3(q, l), to3(k, l), to3(v, l), to3(kc, n_ctx), to3(vc, n_ctx), mb)
    o_c = _ctx_attention(to3(qc, n_ctx), to3(kc, n_ctx), to3(vc, n_ctx))
    w_o = na_w_o[0].astype(BF16)
    xl = _resid_proj(o_l.reshape(n_l, d), w_o, na_b_o[0], xl, mod3[0], lat_row, 2, TM)
    xc = _resid_proj(o_c.reshape(n_c, d), w_o, na_b_o[0], xc, mod3[0], ctx_tile_row, 2, n_ctx)
    xl, xc = _moe_layer([xl, xc], norm_g[0, 1], mod3[0], both_row, (3, 4, 5),
                        (moe_w_group[0], moe_b_group[0], moe_w_expert[0], moe_b_expert[0]),
                        0, moe_w1, moe_w3, moe_w2, None, TM, BM)

    w_in = lru_w_in[0].astype(BF16)
    lw = w_in.shape[1] // 2
    y_l, u_l = _proj(xl, norm_g[1, 0], mod3[1], lat_row, 0, 1, w_in, lru_b_in[0],
                     (lw, lw), (BF16, F32), (1.0, 1.0), TM)
    (u_c,) = _proj(xc, norm_g[1, 0], mod3[1], ctx_tile_row, 0, 1, w_in[:, lw:], lru_b_in[0, lw:],
                   (lw,), (F32,), (1.0,), n_ctx)
    hf, hb = _lru_states(u_l.reshape(b, l, lw), u_c.reshape(b, n_ctx, lw), lru_conv_w[0], lru_conv_b[0],
                         lru_w_gate[0], lru_b_gate[0], lru_lambda[0], TT)
    xl = _lru_out_proj(y_l, hf.reshape(n_l, lw), hb.reshape(n_l, lw), lru_w_o[0].astype(BF16), lru_b_o[0],
                       xl, mod3[1], lat_row, 2, TM)
    (out,) = _moe_layer([xl], norm_g[1, 1], mod3[1], lat_row, (3, 4, 5),
                        (moe_w_group[1], moe_b_group[1], moe_w_expert[1], moe_b_expert[1]),
                        1, moe_w1, moe_w3, moe_w2, final_g, TM, BM)
    return out.reshape(b, l, d)
```

```python
import functools

import jax
import jax.numpy as jnp
import numpy as np
from jax import lax
from jax.experimental import pallas as pl
from jax.experimental.pallas import tpu as pltpu

F32 = jnp.float32
BF16 = jnp.bfloat16

GRID_W = 64
N_MOD = 6
NA_HEADS = 16
WIN_H = 8
WIN_W = 16
LRU_BLOCKS = 4
CONV_W = 4
LRU_C = 8.0
N_GROUPS = 4
EXPERTS_PER_GROUP = 8
N_EXPERTS = N_GROUPS * EXPERTS_PER_GROUP
EPS = 1e-6

LANES = 128
MOD_ROWS = 16
NEG = -1e30
VMEM_LIMIT = 56 * 1024 * 1024
HIGHEST = lax.Precision.HIGHEST
LOG2_E = 1.4426950408889634


def _cparams(sem, vmem=VMEM_LIMIT):
    return pltpu.CompilerParams(dimension_semantics=sem, vmem_limit_bytes=vmem)


def _mod_kernel(c_ref, w_ref, b_ref, o_ref):
    c = c_ref[...]
    s = c * jax.nn.sigmoid(c)
    o_ref[0] = jnp.dot(s, w_ref[0], precision=HIGHEST, preferred_element_type=F32) + b_ref[0]


def _modulation(cvec, ada_w, ada_b):
    depth, d, n = ada_w.shape
    tn = 1536
    return pl.pallas_call(
        _mod_kernel,
        grid=(depth, n // tn),
        in_specs=[
            pl.BlockSpec((MOD_ROWS, d), lambda l, j: (0, 0)),
            pl.BlockSpec((1, d, tn), lambda l, j: (l, 0, j)),
            pl.BlockSpec((1, 1, tn), lambda l, j: (l, 0, j)),
        ],
        out_specs=pl.BlockSpec((1, MOD_ROWS, tn), lambda l, j: (l, 0, j)),
        out_shape=jax.ShapeDtypeStruct((depth, MOD_ROWS, n), F32),
        compiler_params=_cparams(("arbitrary", "arbitrary")),
        name="adaln_mod",
    )(cvec, ada_w, ada_b.reshape(depth, 1, n))


def _norm_mod(x, g, sh, sc):
    ms = jnp.mean(x * x, axis=-1, keepdims=True)
    y = x * lax.rsqrt(ms + EPS) * g
    return y * (1.0 + sc) + sh


def _proj_kernel(x_ref, g_ref, sh_ref, sc_ref, w_ref, b_ref, *o_refs, splits, scales):
    h = _norm_mod(x_ref[...], g_ref[...], sh_ref[0], sc_ref[0]).astype(BF16)
    off = 0
    for o_ref, n, s in zip(o_refs, splits, scales):
        y = jnp.dot(h, w_ref[:, off:off + n], preferred_element_type=F32) + b_ref[:, off:off + n]
        if s != 1.0:
            y = y * s
        o_ref[...] = y.astype(o_ref.dtype)
        off += n


def _proj(x2d, g, mod3, row_of_tile, sh_chunk, sc_chunk, w, b, splits, dtypes, scales, tm):
    n_tok, d = x2d.shape
    n_out = w.shape[1]
    assert sum(splits) == n_out and n_tok % tm == 0
    return pl.pallas_call(
        functools.partial(_proj_kernel, splits=tuple(splits), scales=tuple(scales)),
        grid=(n_tok // tm,),
        in_specs=[
            pl.BlockSpec((tm, d), lambda i: (i, 0)),
            pl.BlockSpec((1, d), lambda i: (0, 0)),
            pl.BlockSpec((1, 1, d), lambda i: (row_of_tile(i), 0, sh_chunk)),
            pl.BlockSpec((1, 1, d), lambda i: (row_of_tile(i), 0, sc_chunk)),
            pl.BlockSpec((d, n_out), lambda i: (0, 0)),
            pl.BlockSpec((1, n_out), lambda i: (0, 0)),
        ],
        out_specs=[pl.BlockSpec((tm, n), lambda i: (i, 0)) for n in splits],
        out_shape=[jax.ShapeDtypeStruct((n_tok, n), dt) for n, dt in zip(splits, dtypes)],
        compiler_params=_cparams(("parallel",)),
        name="norm_mod_proj",
    )(x2d, g.reshape(1, d), mod3, mod3, w, b.reshape(1, n_out))


def _resid_kernel(a_ref, w_ref, b_ref, res_ref, gate_ref, o_ref):
    y = jnp.dot(a_ref[...], w_ref[...], preferred_element_type=F32) + b_ref[...]
    o_ref[...] = res_ref[...] + gate_ref[0] * y


def _resid_proj(a, w, b, res, mod3, row_of_tile, gate_chunk, tm):
    n_tok, k = a.shape
    d = w.shape[1]
    return pl.pallas_call(
        _resid_kernel,
        grid=(n_tok // tm,),
        in_specs=[
            pl.BlockSpec((tm, k), lambda i: (i, 0)),
            pl.BlockSpec((k, d), lambda i: (0, 0)),
            pl.BlockSpec((1, d), lambda i: (0, 0)),
            pl.BlockSpec((tm, d), lambda i: (i, 0)),
            pl.BlockSpec((1, 1, d), lambda i: (row_of_tile(i), 0, gate_chunk)),
        ],
        out_specs=pl.BlockSpec((tm, d), lambda i: (i, 0)),
        out_shape=jax.ShapeDtypeStruct((n_tok, d), F32),
        compiler_params=_cparams(("parallel",)),
        name="proj_residual",
    )(a, w, b.reshape(1, d), res, mod3)


HEAD_GROUP = 4


def _head_group_attention(qg, keys, vals, biases):
    n_q, width = qg.shape
    dh = width // HEAD_GROUP
    head_of_lane = lax.broadcasted_iota(jnp.int32, (n_q, width), 1) // dh
    qs = jnp.concatenate([jnp.where(head_of_lane == h, qg, jnp.zeros_like(qg)) for h in range(HEAD_GROUP)],
                         axis=0)
    s_parts = []
    for kk, bb in zip(keys, biases):
        s = lax.dot_general(qs, kk, (((1,), (1,)), ((), ())), preferred_element_type=F32)
        s_parts.append(s if bb is None else s + bb)
    m = s_parts[0].max(axis=-1, keepdims=True)
    for s in s_parts[1:]:
        m = jnp.maximum(m, s.max(axis=-1, keepdims=True))
    p_parts = [jnp.exp(s - m) for s in s_parts]
    l = p_parts[0].sum(axis=-1, keepdims=True)
    for p in p_parts[1:]:
        l = l + p.sum(axis=-1, keepdims=True)
    o = None
    for p, vv in zip(p_parts, vals):
        t = jnp.dot(p.astype(BF16), vv, preferred_element_type=F32)
        o = t if o is None else o + t
    o = o / l
    out = jnp.where(head_of_lane == 0, o[0:n_q], 0.0)
    for h in range(1, HEAD_GROUP):
        out = jnp.where(head_of_lane == h, o[h * n_q:(h + 1) * n_q], out)
    return out


NA_ROWS_PER_STEP = 4


def _na_kernel(q_ref, k_ref, v_ref, kc_ref, vc_ref, *rest, rows, kh):
    mb_refs, o_ref = rest[:-1], rest[-1]
    n_win = kh * GRID_W
    width = HEAD_GROUP * (q_ref.shape[2] // NA_HEADS)

    for j, mb_ref in enumerate(mb_refs):
        r = pl.program_id(1) * len(mb_refs) + j
        rs = jnp.clip(r - kh // 2, 0, rows - kh)
        k0 = pl.multiple_of(rs * GRID_W, GRID_W)
        q0 = j * GRID_W
        for g in range(NA_HEADS // HEAD_GROUP):
            c0 = g * width
            qg = q_ref[0, q0:q0 + GRID_W, c0:c0 + width]
            kw = k_ref[0, pl.ds(k0, n_win), c0:c0 + width]
            vw = v_ref[0, pl.ds(k0, n_win), c0:c0 + width]
            kc = kc_ref[0, :, c0:c0 + width]
            vc = vc_ref[0, :, c0:c0 + width]
            bias = mb_ref[0, g * HEAD_GROUP:(g + 1) * HEAD_GROUP].reshape(HEAD_GROUP * GRID_W, n_win)
            o = _head_group_attention(qg, [kw, kc], [vw, vc], [bias, None])
            o_ref[0, q0:q0 + GRID_W, c0:c0 + width] = o.astype(o_ref.dtype)


def _window_bias(rpb, rows):
    kh = min(WIN_H, rows)
    cols = np.arange(GRID_W)
    col_start = np.clip(cols - WIN_W // 2, 0, GRID_W - WIN_W)
    ck = np.arange(GRID_W)
    in_win = (ck[None, :] >= col_start[:, None]) & (ck[None, :] < col_start[:, None] + WIN_W)
    dc = ck[None, :] - cols[:, None] + (WIN_W - 1)
    n_dc = 2 * WIN_W - 1
    sel = ((dc[None] == np.arange(n_dc)[:, None, None]) & in_win[None]).astype(np.float32)
    t = jnp.einsum("hrc,cqk->hrqk", rpb.astype(F32), jnp.asarray(sel), precision=HIGHEST)
    t = jnp.where(jnp.asarray(in_win)[None, None], t, NEG)
    per_delta = []
    for delta in range(kh):
        r0 = WIN_H - 1 - delta
        per_delta.append(jnp.transpose(t[:, r0:r0 + kh], (0, 2, 1, 3)).reshape(NA_HEADS, GRID_W, kh * GRID_W))
    return jnp.stack(per_delta)


def _na_attention(q, k, v, kc, vc, mb):
    b, l, d = q.shape
    c = kc.shape[1]
    rows = l // GRID_W
    kh = mb.shape[0]

    rps = NA_ROWS_PER_STEP
    assert rows % rps == 0

    def bias_spec(j):
        def index(bi, s):
            r = s * rps + j
            return (r - jnp.clip(r - kh // 2, 0, rows - kh), 0, 0, 0)
        return pl.BlockSpec((1, NA_HEADS, GRID_W, kh * GRID_W), index)

    return pl.pallas_call(
        functools.partial(_na_kernel, rows=rows, kh=kh),
        grid=(b, rows // rps),
        in_specs=[
            pl.BlockSpec((1, rps * GRID_W, d), lambda bi, s: (bi, s, 0)),
            pl.BlockSpec((1, l, d), lambda bi, s: (bi, 0, 0), pipeline_mode=pl.Buffered(1)),
            pl.BlockSpec((1, l, d), lambda bi, s: (bi, 0, 0), pipeline_mode=pl.Buffered(1)),
            pl.BlockSpec((1, c, d), lambda bi, s: (bi, 0, 0)),
            pl.BlockSpec((1, c, d), lambda bi, s: (bi, 0, 0)),
        ] + [bias_spec(j) for j in range(rps)],
        out_specs=pl.BlockSpec((1, rps * GRID_W, d), lambda bi, s: (bi, s, 0)),
        out_shape=jax.ShapeDtypeStruct((b, l, d), BF16),
        compiler_params=_cparams(("parallel", "arbitrary")),
        name="na_attention",
    )(q, k, v, kc, vc, *([mb] * rps))


def _ctx_attn_kernel(q_ref, k_ref, v_ref, o_ref):
    width = HEAD_GROUP * (q_ref.shape[2] // NA_HEADS)

    def group(g, carry):
        c0 = pl.multiple_of(g * width, width)
        o = _head_group_attention(q_ref[0, :, pl.ds(c0, width)], [k_ref[0, :, pl.ds(c0, width)]],
                                  [v_ref[0, :, pl.ds(c0, width)]], [None])
        o_ref[0, :, pl.ds(c0, width)] = o.astype(o_ref.dtype)
        return carry

    lax.fori_loop(0, NA_HEADS // HEAD_GROUP, group, 0)


def _ctx_attention(qc, kc, vc):
    b, c, d = qc.shape
    spec = pl.BlockSpec((1, c, d), lambda bi: (bi, 0, 0))
    return pl.pallas_call(
        _ctx_attn_kernel,
        grid=(b,),
        in_specs=[spec, spec, spec],
        out_specs=spec,
        out_shape=jax.ShapeDtypeStruct((b, c, d), BF16),
        compiler_params=_cparams(("parallel",)),
        name="ctx_attention",
    )(qc, kc, vc)


GROUP_LANE0 = N_EXPERTS


def _src_specs(srcs, tm):
    d = srcs[0].shape[1]
    tiles = [s.shape[0] // tm for s in srcs]
    specs, first = [], 0
    for t in tiles:
        specs.append(pl.BlockSpec((tm, d), functools.partial(
            lambda i, first, t: (jnp.clip(i - first, 0, t - 1), 0), first=first, t=t)))
        first += t
    return specs, tiles


def _select_src(i, refs, tiles):
    x = refs[-1][...]
    first = sum(tiles[:-1])
    for ref, t in zip(refs[-2::-1], tiles[-2::-1]):
        x = jnp.where(i < first, ref[...], x)
        first -= t
    return x


TOK_ROWS = 8


def _store_token_major(ref, x):
    n, d = x.shape
    assert d == TOK_ROWS * LANES
    for j in range(TOK_ROWS):
        ref[pl.ds(j, n, stride=TOK_ROWS), :] = x[:, j * LANES:(j + 1) * LANES]


def _load_token_major(ref, n):
    return jnp.concatenate([ref[pl.ds(j, n, stride=TOK_ROWS), :] for j in range(TOK_ROWS)], axis=1)


def _router_kernel(*refs, tiles):
    n_src = len(tiles)
    x_refs = refs[:n_src]
    g_ref, sh_ref, sc_ref, wr_ref, br_ref, h_ref, oh_ref, wd_ref, cnt_ref = refs[n_src:]
    i = pl.program_id(0)
    h = _norm_mod(_select_src(i, x_refs, tiles), g_ref[...], sh_ref[0], sc_ref[0])
    _store_token_major(h_ref, h)
    logits = jnp.dot(h, wr_ref[...], precision=HIGHEST, preferred_element_type=F32) + br_ref[...]
    tm = logits.shape[0]
    lane = lax.broadcasted_iota(jnp.int32, (tm, LANES), 1)
    big = jnp.int32(LANES)

    gmask = (lane >= GROUP_LANE0) & (lane < GROUP_LANE0 + N_GROUPS)
    gl = jnp.where(gmask, logits, NEG)
    gmax = gl.max(axis=-1, keepdims=True)
    gsel = jnp.where(gmask & (gl == gmax), lane, big).min(axis=-1, keepdims=True) - GROUP_LANE0
    g_w = 1.0 / jnp.where(gmask, jnp.exp(gl - gmax), 0.0).sum(axis=-1, keepdims=True)

    e0 = gsel * EXPERTS_PER_GROUP
    emask = (lane >= e0) & (lane < e0 + EXPERTS_PER_GROUP)
    el = jnp.where(emask, logits, NEG)
    v1 = el.max(axis=-1, keepdims=True)
    i1 = jnp.where(emask & (el == v1), lane, big).min(axis=-1, keepdims=True)
    el2 = jnp.where(lane == i1, NEG, el)
    v2 = el2.max(axis=-1, keepdims=True)
    i2 = jnp.where(emask & (lane != i1) & (el2 == v2), lane, big).min(axis=-1, keepdims=True)
    t = jnp.exp(v2 - v1)
    w1 = g_w / (1.0 + t)
    w2 = g_w * t / (1.0 + t)

    sel1 = lane == i1
    sel2 = lane == i2
    oh = jnp.where(sel1 | sel2, 1.0, 0.0)
    oh_ref[...] = oh.astype(oh_ref.dtype)
    wd_ref[...] = jnp.where(sel1, w1, jnp.where(sel2, w2, 0.0))

    @pl.when(i == 0)
    def _():
        cnt_ref[...] = jnp.zeros_like(cnt_ref)

    cnt_ref[...] += oh.sum(axis=0, keepdims=True)


def _router(srcs, g, mod3, row_of_tile, sh_chunk, sc_chunk, wr, br, tm):
    d = srcs[0].shape[1]
    src_specs, tiles = _src_specs(srcs, tm)
    n = sum(tiles) * tm
    tok = pl.BlockSpec((tm, LANES), lambda i: (i, 0))
    return pl.pallas_call(
        functools.partial(_router_kernel, tiles=tuple(tiles)),
        grid=(sum(tiles),),
        in_specs=src_specs + [
            pl.BlockSpec((1, d), lambda i: (0, 0)),
            pl.BlockSpec((1, 1, d), lambda i: (row_of_tile(i), 0, sh_chunk)),
            pl.BlockSpec((1, 1, d), lambda i: (row_of_tile(i), 0, sc_chunk)),
            pl.BlockSpec((d, LANES), lambda i: (0, 0)),
            pl.BlockSpec((1, LANES), lambda i: (0, 0)),
        ],
        out_specs=[pl.BlockSpec((tm * TOK_ROWS, LANES), lambda i: (i, 0)), tok, tok,
                   pl.BlockSpec((1, LANES), lambda i: (0, 0))],
        out_shape=[jax.ShapeDtypeStruct((n * TOK_ROWS, LANES), F32), jax.ShapeDtypeStruct((n, LANES), BF16),
                   jax.ShapeDtypeStruct((n, LANES), F32), jax.ShapeDtypeStruct((1, LANES), F32)],
        compiler_params=_cparams(("arbitrary",)),
        name="moe_router",
    )(*srcs, g.reshape(1, d), mod3, mod3, wr, br)


def _dest_kernel(start_ref, oh_ref, wd_ref, dest_ref, wsel_ref, carry_ref):
    i = pl.program_id(0)

    @pl.when(i == 0)
    def _():
        carry_ref[...] = jnp.zeros_like(carry_ref)

    oh = oh_ref[...]
    tm = oh.shape[0]
    row = lax.broadcasted_iota(jnp.int32, (tm, tm), 0)
    col = lax.broadcasted_iota(jnp.int32, (tm, tm), 1)
    tri = jnp.where(row > col, 1.0, 0.0).astype(BF16)
    base = jnp.dot(tri, oh, preferred_element_type=F32) + carry_ref[...] + start_ref[...]
    ohf = oh.astype(F32)
    lane = lax.broadcasted_iota(jnp.int32, (tm, LANES), 1)
    sel = ohf > 0.0
    ea = jnp.where(sel, lane, LANES).min(axis=-1, keepdims=True)
    eb = jnp.where(sel, lane, -1).max(axis=-1, keepdims=True)
    wd = wd_ref[...]
    pick = lambda e, val: jnp.where(lane == e, val, 0.0).sum(axis=-1, keepdims=True)
    dcols = jnp.where(lane == 0, pick(ea, base), jnp.where(lane == 1, pick(eb, base), 0.0))
    dest_ref[0] = dcols.T[0:2, :].astype(jnp.int32)
    two = lax.broadcasted_iota(jnp.int32, (tm, 2), 1) == 0
    wsel_ref[...] = jnp.where(two, pick(ea, wd), pick(eb, wd))
    carry_ref[...] += ohf.sum(axis=0, keepdims=True)


def _dest(start, oh, wd, tm):
    n = oh.shape[0]
    tok = pl.BlockSpec((tm, LANES), lambda i: (i, 0))
    two = pl.BlockSpec((tm, 2), lambda i: (i, 0))
    return pl.pallas_call(
        _dest_kernel,
        grid=(n // tm,),
        in_specs=[pl.BlockSpec((1, LANES), lambda i: (0, 0)), tok, tok],
        out_specs=[pl.BlockSpec((1, 2, tm), lambda i: (i, 0, 0)), two],
        out_shape=[jax.ShapeDtypeStruct((n // tm, 2, tm), jnp.int32), jax.ShapeDtypeStruct((n, 2), F32)],
        scratch_shapes=[pltpu.VMEM((1, LANES), F32)],
        compiler_params=_cparams(("arbitrary",)),
        name="moe_dest",
    )(start, oh, wd)


def _tok_rows(t):
    return pl.ds(t * TOK_ROWS, TOK_ROWS)


def _wait_tokens(buf, n, sem):
    whole = buf.at[pl.ds(0, n * TOK_ROWS), :]
    pltpu.make_async_copy(whole, whole, sem).wait()


def _scatter_kernel(dest_ref, h_ref, xs_ref, sem):
    tm = h_ref.shape[0] // TOK_ROWS
    for r in range(tm):
        for k in range(2):
            pltpu.make_async_copy(h_ref.at[_tok_rows(r), :], xs_ref.at[dest_ref[0, k, r]], sem).start(priority=k)
    for k in range(2):
        _wait_tokens(h_ref, tm, sem)


def _scatter_rows(dest, h, tm):
    n = h.shape[0] // TOK_ROWS
    return pl.pallas_call(
        _scatter_kernel,
        grid=(n // tm,),
        in_specs=[
            pl.BlockSpec((1, 2, tm), lambda i: (i, 0, 0), memory_space=pltpu.SMEM),
            pl.BlockSpec((tm * TOK_ROWS, LANES), lambda i: (i, 0)),
        ],
        out_specs=pl.BlockSpec(memory_space=pl.ANY),
        out_shape=jax.ShapeDtypeStruct((2 * n, TOK_ROWS, LANES), F32),
        scratch_shapes=[pltpu.SemaphoreType.DMA(())],
        compiler_params=_cparams(("arbitrary",)),
        name="moe_scatter",
    )(dest, h)


def _gmm_kernel(vb_ref, ve_ref, lo_ref, hi_ref, x_ref, w1_ref, w3_ref, w2_ref, o_ref, w1b, w3b, w2b, *, bm):
    v = pl.program_id(0)
    pv = jnp.maximum(v - 1, 0)

    @pl.when((v == 0) | (ve_ref[v] != ve_ref[pv]))
    def _():
        w1b[...] = w1_ref[0, 0].astype(BF16)
        w3b[...] = w3_ref[0, 0].astype(BF16)
        w2b[...] = w2_ref[0, 0].astype(BF16)

    lo = lo_ref[v]
    hi = hi_ref[v]

    @pl.when(hi > lo)
    def _():
        xb = _load_token_major(x_ref, bm).astype(BF16)
        a = jnp.dot(xb, w1b[...], preferred_element_type=F32)
        g = jnp.dot(xb, w3b[...], preferred_element_type=F32)
        hdn = (a * jax.nn.sigmoid(a) * g).astype(BF16)
        y = jnp.dot(hdn, w2b[...], preferred_element_type=F32)
        rows = lax.broadcasted_iota(jnp.int32, (bm, 1), 0)
        mine = (rows >= lo) & (rows < hi)
        first = (v == 0) | (vb_ref[v] != vb_ref[pv])

        @pl.when(first)
        def _():
            _store_token_major(o_ref, jnp.where(mine, y, 0.0))

        @pl.when(jnp.logical_not(first))
        def _():
            _store_token_major(o_ref, jnp.where(mine, y, _load_token_major(o_ref, bm)))


def _gmm(visits, xs, layer, w1, w3, w2, bm):
    d, f = w1.shape[2:]
    n_vis = visits[0].shape[0]
    blk = pl.BlockSpec((bm * TOK_ROWS, LANES), lambda v, vb, ve, lo, hi: (vb[v], 0))
    grid_spec = pltpu.PrefetchScalarGridSpec(
        num_scalar_prefetch=4,
        grid=(n_vis,),
        in_specs=[
            blk,
            pl.BlockSpec((1, 1, d, f), lambda v, vb, ve, lo, hi: (layer, ve[v], 0, 0)),
            pl.BlockSpec((1, 1, d, f), lambda v, vb, ve, lo, hi: (layer, ve[v], 0, 0)),
            pl.BlockSpec((1, 1, f, d), lambda v, vb, ve, lo, hi: (layer, ve[v], 0, 0)),
        ],
        out_specs=blk,
        scratch_shapes=[pltpu.VMEM((d, f), BF16), pltpu.VMEM((d, f), BF16), pltpu.VMEM((f, d), BF16)],
    )
    return pl.pallas_call(
        functools.partial(_gmm_kernel, bm=bm),
        grid_spec=grid_spec,
        out_shape=jax.ShapeDtypeStruct(xs.shape, F32),
        compiler_params=_cparams(("arbitrary",)),
        name="moe_experts",
    )(*visits, xs, w1, w3, w2)


def _visit_plan(counts, n_rows, bm):
    counts = counts.astype(jnp.int32)
    end = jnp.cumsum(counts)
    start = end - counts
    n_blocks = n_rows // bm
    n_vis = n_blocks + N_EXPERTS
    tiles = jnp.where(counts > 0, (end - 1) // bm - start // bm + 1, 0)
    vend = jnp.cumsum(tiles)
    vstart = vend - tiles
    v = jnp.arange(n_vis, dtype=jnp.int32)[:, None]
    owns = (vstart[None, :] <= v) & (v < vend[None, :])
    take = lambda a: jnp.sum(jnp.where(owns, a[None, :], 0), axis=1)
    valid = v[:, 0] < vend[-1]
    last_e = jnp.max(jnp.where(counts > 0, jnp.arange(N_EXPERTS, dtype=jnp.int32), 0))
    e = jnp.where(valid, take(jnp.arange(N_EXPERTS, dtype=jnp.int32)), last_e)
    blk = jnp.where(valid, take(start // bm - vstart) + v[:, 0], n_blocks - 1)
    lo = jnp.where(valid, jnp.clip(take(start) - blk * bm, 0, bm), 0)
    hi = jnp.where(valid, jnp.clip(take(end) - blk * bm, 0, bm), 0)
    return start, (blk.astype(jnp.int32), e.astype(jnp.int32), lo.astype(jnp.int32), hi.astype(jnp.int32))


def _combine_kernel(dest_ref, dest_next_ref, ys_ref, *refs, tiles, final_norm):
    n_src = len(tiles)
    x_refs = refs[:n_src]
    w_ref, gate_ref = refs[n_src:n_src + 2]
    rest = refs[n_src + 2:]
    fg_ref = rest[0] if final_norm else None
    o_refs = rest[-n_src - 2:-2]
    buf, sems = rest[-2:]
    i = pl.program_id(0)
    n_steps = pl.num_programs(0)
    tm = x_refs[0].shape[0]

    def gather(d_ref, slot):
        for r in range(tm):
            for k in range(2):
                pltpu.make_async_copy(ys_ref.at[d_ref[0, k, r]], buf.at[slot, k, _tok_rows(r), :],
                                      sems.at[slot]).start(priority=k)

    @pl.when(i == 0)
    def _():
        gather(dest_ref, 0)

    @pl.when(i + 1 < n_steps)
    def _():
        gather(dest_next_ref, (i + 1) % 2)

    slot = i % 2
    for k in range(2):
        _wait_tokens(buf.at[slot, k], tm, sems.at[slot])
    w = w_ref[...]
    moe = (w[:, 0:1] * _load_token_major(buf.at[slot, 0], tm)
           + w[:, 1:2] * _load_token_major(buf.at[slot, 1], tm))
    y = _select_src(i, x_refs, tiles) + gate_ref[0] * moe
    if final_norm:
        y = y * lax.rsqrt(jnp.mean(y * y, axis=-1, keepdims=True) + EPS) * fg_ref[...]

    first = 0
    for o_ref, t in zip(o_refs, tiles):
        @pl.when((i >= first) & (i < first + t))
        def _(o_ref=o_ref):
            o_ref[...] = y
        first += t


def _combine(dest, wsel, ys, srcs, mod3, row_of_tile, gate_chunk, final_g, tm):
    d = srcs[0].shape[1]
    src_specs, tiles = _src_specs(srcs, tm)
    final_norm = final_g is not None
    extra_specs = [pl.BlockSpec((1, d), lambda i: (0, 0))] if final_norm else []
    extra_args = [final_g.reshape(1, d)] if final_norm else []
    n_steps = sum(tiles)
    return pl.pallas_call(
        functools.partial(_combine_kernel, tiles=tuple(tiles), final_norm=final_norm),
        grid=(n_steps,),
        in_specs=[
            pl.BlockSpec((1, 2, tm), lambda i: (i, 0, 0), memory_space=pltpu.SMEM),
            pl.BlockSpec((1, 2, tm), lambda i: (jnp.minimum(i + 1, n_steps - 1), 0, 0), memory_space=pltpu.SMEM),
            pl.BlockSpec(memory_space=pl.ANY),
        ] + src_specs + [
            pl.BlockSpec((tm, 2), lambda i: (i, 0)),
            pl.BlockSpec((1, 1, d), lambda i: (row_of_tile(i), 0, gate_chunk)),
        ] + extra_specs,
        out_specs=list(src_specs),
        out_shape=[jax.ShapeDtypeStruct(s.shape, F32) for s in srcs],
        scratch_shapes=[pltpu.VMEM((2, 2, tm * TOK_ROWS, LANES), F32), pltpu.SemaphoreType.DMA((2,))],
        compiler_params=_cparams(("arbitrary",)),
        name="moe_combine",
    )(dest, dest, ys, *srcs, wsel, mod3, *extra_args)


def _router_weights(w_group, b_group, w_expert, b_expert):
    d = w_group.shape[0]
    we = jnp.transpose(w_expert, (1, 0, 2)).reshape(d, N_EXPERTS)
    wr = jnp.concatenate([we, w_group, jnp.zeros((d, LANES - N_EXPERTS - N_GROUPS), F32)], axis=1)
    br = jnp.concatenate([b_expert.reshape(N_EXPERTS), b_group,
                          jnp.zeros((LANES - N_EXPERTS - N_GROUPS,), F32)]).reshape(1, LANES)
    return wr, br


def _moe_layer(srcs, g, mod3, row_of_tile, chunks, router_w, layer, w1, w3, w2, final_g, tm, bm):
    sh_chunk, sc_chunk, gate_chunk = chunks
    wr, br = _router_weights(*router_w)
    h, oh, wd, cnt = _router(srcs, g, mod3, row_of_tile, sh_chunk, sc_chunk, wr, br, tm)
    n = oh.shape[0]
    start, visits = _visit_plan(cnt[0, :N_EXPERTS], 2 * n, bm)
    start_row = jnp.zeros((1, LANES), F32).at[0, :N_EXPERTS].set(start.astype(F32))
    dest, wsel = _dest(start_row, oh, wd, tm)
    xs = _scatter_rows(dest, h, tm)
    ys = _gmm(visits, xs.reshape(2 * n * TOK_ROWS, LANES), layer, w1, w3, w2, bm)
    return _combine(dest, wsel, ys.reshape(2 * n, TOK_ROWS, LANES), srcs, mod3, row_of_tile, gate_chunk,
                    final_g, tm)


def _gelu_tanh(x):
    return x * (0.5 * (1.0 + jnp.tanh(0.7978845608028654 * (x + 0.044715 * (x * x * x)))))


def _lru_in_kernel(x_ref, g_ref, sh_ref, sc_ref, w_ref, b_ref, *o_refs, with_y):
    nb, tt, d = x_ref.shape
    h = _norm_mod(x_ref[...], g_ref[...], sh_ref[...], sc_ref[...]).reshape(nb * tt, d).astype(BF16)
    lw = o_refs[-1].shape[2]
    if with_y:
        y = jnp.dot(h, w_ref[:, 0:lw], preferred_element_type=F32) + b_ref[:, 0:lw]
        o_refs[0][...] = y.reshape(nb, tt, lw).astype(o_refs[0].dtype)
    off = w_ref.shape[1] - lw
    u = jnp.dot(h, w_ref[:, off:off + lw], preferred_element_type=F32) + b_ref[:, off:off + lw]
    o_refs[-1][...] = jnp.swapaxes(u.reshape(nb, tt, lw), 0, 1)


def _lru_in_proj(x3, g, mod3, mod_rows, w, b, with_y, tt):
    nb, t, d = x3.shape
    lw = w.shape[1] // 2 if with_y else w.shape[1]
    r0, nr = mod_rows
    mod_spec = lambda chunk: pl.BlockSpec((nr, 1, d), lambda i: (r0 // nr, 0, chunk))
    out_specs = [pl.BlockSpec((tt, nb, lw), lambda i: (i, 0, 0))]
    out_shape = [jax.ShapeDtypeStruct((t, nb, lw), F32)]
    if with_y:
        out_specs.insert(0, pl.BlockSpec((nb, tt, lw), lambda i: (0, i, 0)))
        out_shape.insert(0, jax.ShapeDtypeStruct((nb, t, lw), BF16))
    return pl.pallas_call(
        functools.partial(_lru_in_kernel, with_y=with_y),
        grid=(t // tt,),
        in_specs=[
            pl.BlockSpec((nb, tt, d), lambda i: (0, i, 0)),
            pl.BlockSpec((1, d), lambda i: (0, 0)),
            mod_spec(0), mod_spec(1),
            pl.BlockSpec(w.shape, lambda i: (0, 0)),
            pl.BlockSpec((1, w.shape[1]), lambda i: (0, 0)),
        ],
        out_specs=out_specs,
        out_shape=out_shape,
        compiler_params=_cparams(("parallel",)),
        name="rglru_in_proj",
    )(x3, g.reshape(1, d), mod3, mod3, w, b.reshape(1, w.shape[1]))


def _lru_tm_kernel(u_ref, up_ref, un_ref, cw_ref, cb_ref, wg_ref, bg_ref, lam_ref, h0_ref,
                   h_ref, hend_ref, a_s, b_s, carry, *, tt, n_t, reverse):
    i = pl.program_id(0)
    ti = (n_t - 1 - i) if reverse else i
    _, nb, w = u_ref.shape
    cb = w // LRU_BLOCKS
    left = CONV_W // 2
    right = CONV_W - 1 - left

    @pl.when(i == 0)
    def _():
        carry[...] = h0_ref[...]

    prev = jnp.where(ti > 0, up_ref[...], 0.0)
    nxt = jnp.where(ti < n_t - 1, un_ref[...], 0.0)
    for n in range(LRU_BLOCKS):
        c0 = n * cb
        ext = jnp.concatenate([prev[:, :, c0:c0 + cb], u_ref[:, :, c0:c0 + cb], nxt[:, :, c0:c0 + cb]], axis=0)
        uc = cb_ref[:, c0:c0 + cb] + cw_ref[0:1, c0:c0 + cb] * ext[0:tt]
        for kk in range(1, CONV_W):
            uc = uc + cw_ref[kk:kk + 1, c0:c0 + cb] * ext[kk:kk + tt]
        ub = uc.reshape(tt * nb, cb).astype(BF16)
        r = jax.nn.sigmoid(jnp.dot(ub, wg_ref[0, n], preferred_element_type=F32) + bg_ref[0, :, c0:c0 + cb])
        ig = jax.nn.sigmoid(jnp.dot(ub, wg_ref[1, n], preferred_element_type=F32) + bg_ref[1, :, c0:c0 + cb])
        lam = lam_ref[:, c0:c0 + cb]
        rate = (-LRU_C * LOG2_E) * jnp.log1p(jnp.exp(-lam))
        a = jnp.exp2(r * rate)
        z = 1.0 - a * a
        root = jnp.where(z > 0.0, z * lax.rsqrt(z), 0.0)
        b = root * (ig * uc.reshape(tt * nb, cb))
        a_s[:, :, c0:c0 + cb] = a.reshape(tt, nb, cb)
        b_s[:, :, c0:c0 + cb] = b.reshape(tt, nb, cb)

    def step(s, h):
        t = (tt - 1 - s) if reverse else s
        h = a_s[t] * h + b_s[t]
        h_ref[t] = h
        return h

    h_last = lax.fori_loop(0, tt, step, carry[...], unroll=8)
    carry[...] = h_last

    @pl.when(i == n_t - 1)
    def _():
        hend_ref[...] = h_last


def _lru_tm_scan(u, h0, conv_w, conv_b, wg, bg, lam, reverse, tt):
    t, nb, w = u.shape
    n_t = t // tt
    left = CONV_W // 2
    right = CONV_W - 1 - left
    assert tt % left == 0 and t % tt == 0
    tmap = (lambda i: n_t - 1 - i) if reverse else (lambda i: i)
    full = lambda a: pl.BlockSpec(a.shape, lambda i: (0,) * a.ndim)
    conv_b = conv_b.reshape(1, w)
    bg = bg.reshape(2, 1, w)
    lam = lam.reshape(1, w)
    return pl.pallas_call(
        functools.partial(_lru_tm_kernel, tt=tt, n_t=n_t, reverse=reverse),
        grid=(n_t,),
        in_specs=[
            pl.BlockSpec((tt, nb, w), lambda i: (tmap(i), 0, 0)),
            pl.BlockSpec((left, nb, w), lambda i: (jnp.maximum(tmap(i) * (tt // left) - 1, 0), 0, 0)),
            pl.BlockSpec((right, nb, w), lambda i: (jnp.minimum((tmap(i) + 1) * (tt // right), t // right - 1), 0, 0)),
            full(conv_w), full(conv_b), full(wg), full(bg), full(lam), full(h0),
        ],
        out_specs=[pl.BlockSpec((tt, nb, w), lambda i: (tmap(i), 0, 0)), full(h0)],
        out_shape=[jax.ShapeDtypeStruct((t, nb, w), F32), jax.ShapeDtypeStruct(h0.shape, F32)],
        scratch_shapes=[pltpu.VMEM((tt, nb, w), F32), pltpu.VMEM((tt, nb, w), F32), pltpu.VMEM(h0.shape, F32)],
        compiler_params=_cparams(("arbitrary",)),
        name="rglru_scan_rev" if reverse else "rglru_scan_fwd",
    )(u, u, u, conv_w, conv_b, wg, bg, lam, h0)


def _lru_out_tm_kernel(y_ref, hf_ref, hb_ref, w_ref, b_ref, res_ref, gate_ref, o_ref):
    nb, tt, lw = y_ref.shape
    y = jnp.swapaxes(y_ref[...].astype(F32), 0, 1)
    a = (_gelu_tanh(y) * (hf_ref[...] + hb_ref[...])).reshape(tt * nb, lw).astype(BF16)
    z = jnp.dot(a, w_ref[...], preferred_element_type=F32) + b_ref[...]
    z = jnp.swapaxes(z.reshape(tt, nb, z.shape[1]), 0, 1)
    o_ref[...] = res_ref[...] + gate_ref[...] * z


def _lru_out_tm_proj(y, hf, hb, w, b, res3, mod3, gate_chunk, tt):
    nb, t, lw = y.shape
    d = w.shape[1]
    tm_spec = pl.BlockSpec((tt, nb, lw), lambda i: (i, 0, 0))
    return pl.pallas_call(
        _lru_out_tm_kernel,
        grid=(t // tt,),
        in_specs=[
            pl.BlockSpec((nb, tt, lw), lambda i: (0, i, 0)),
            tm_spec, tm_spec,
            pl.BlockSpec((lw, d), lambda i: (0, 0)),
            pl.BlockSpec((1, d), lambda i: (0, 0)),
            pl.BlockSpec((nb, tt, d), lambda i: (0, i, 0)),
            pl.BlockSpec((nb, 1, d), lambda i: (0, 0, gate_chunk)),
        ],
        out_specs=pl.BlockSpec((nb, tt, d), lambda i: (0, i, 0)),
        out_shape=jax.ShapeDtypeStruct((nb, t, d), F32),
        compiler_params=_cparams(("parallel",)),
        name="rglru_out_proj",
    )(y, hf, hb, w, b.reshape(1, d), res3, mod3)


def _lru_states(ul, uc, conv_w, conv_b, w_gate, b_gate, lam, tt):
    wg = w_gate.astype(BF16)
    _, nb, w = ul.shape
    zero = jnp.zeros((nb, w), F32)
    hs = []
    for dirn in range(2):
        rev = dirn == 1
        args = (conv_w, conv_b, wg[dirn], b_gate[dirn], lam[dirn], rev, tt)
        _, h_end = _lru_tm_scan(uc, zero, *args)
        h, _ = _lru_tm_scan(ul, h_end, *args)
        hs.append(h)
    return hs


TM = 512
BM = 512
TT = 64


def kernel(x, c, ctx, c_ctx, ada_w, ada_b, norm_g, na_w_qkv, na_b_qkv, na_rpb, na_w_o, na_b_o, lru_w_in, lru_b_in, lru_conv_w, lru_conv_b, lru_w_gate, lru_b_gate, lru_lambda, lru_w_o, lru_b_o, moe_w_group, moe_b_group, moe_w_expert, moe_b_expert, moe_w1, moe_w3, moe_w2, final_g):
    b, l, d = x.shape
    n_ctx = ctx.shape[1]
    n_l, n_c = b * l, b * n_ctx
    assert l % TM == 0 and n_ctx <= TM and TM % n_ctx == 0 and b + 1 <= MOD_ROWS
    ctx_row = b
    tiles_per_batch = l // TM
    lat_row = lambda i: i // tiles_per_batch
    ctx_tile_row = lambda i: ctx_row
    nl_tiles = n_l // TM
    both_row = lambda i: jnp.where(i < nl_tiles, i // tiles_per_batch, ctx_row)

    cvec = jnp.concatenate([c, c_ctx[None], jnp.zeros((MOD_ROWS - b - 1, d), F32)], axis=0)
    mod = _modulation(cvec, ada_w, ada_b)
    mod3 = [mod[i].reshape(MOD_ROWS, 1, N_MOD * d) for i in range(mod.shape[0])]
    xl = x.reshape(n_l, d)
    xc = ctx.reshape(n_c, d)

    w_qkv = na_w_qkv[0].astype(BF16)
    qk_scale = (d // NA_HEADS) ** -0.5
    q, k, v = _proj(xl, norm_g[0, 0], mod3[0], lat_row, 0, 1, w_qkv, na_b_qkv[0],
                    (d, d, d), (BF16,) * 3, (qk_scale, 1.0, 1.0), TM)
    qc, kc, vc = _proj(xc, norm_g[0, 0], mod3[0], ctx_tile_row, 0, 1, w_qkv, na_b_qkv[0],
                       (d, d, d), (BF16,) * 3, (qk_scale, 1.0, 1.0), n_ctx)
    to3 = lambda a, s: a.reshape(b, s, d)
    mb = _window_bias(na_rpb[0], l // GRID_W)
    o_l = _na_attention(to3(q, l), to3(k, l), to3(v, l), to3(kc, n_ctx), to3(vc, n_ctx), mb)
    o_c = _ctx_attention(to3(qc, n_ctx), to3(kc, n_ctx), to3(vc, n_ctx))
    w_o = na_w_o[0].astype(BF16)
    xl = _resid_proj(o_l.reshape(n_l, d), w_o, na_b_o[0], xl, mod3[0], lat_row, 2, TM)
    xc = _resid_proj(o_c.reshape(n_c, d), w_o, na_b_o[0], xc, mod3[0], ctx_tile_row, 2, n_ctx)
    xl, xc = _moe_layer([xl, xc], norm_g[0, 1], mod3[0], both_row, (3, 4, 5),
                        (moe_w_group[0], moe_b_group[0], moe_w_expert[0], moe_b_expert[0]),
                        0, moe_w1, moe_w3, moe_w2, None, TM, BM)

    w_in = lru_w_in[0].astype(BF16)
    lw = w_in.shape[1] // 2
    xl3 = xl.reshape(b, l, d)
    y_l, u_l = _lru_in_proj(xl3, norm_g[1, 0], mod3[1], (0, b), w_in, lru_b_in[0], True, TT)
    (u_c,) = _lru_in_proj(xc.reshape(b, n_ctx, d), norm_g[1, 0], mod3[1], (ctx_row, 1), w_in[:, lw:],
                          lru_b_in[0, lw:], False, TT)
    hf, hb = _lru_states(u_l, u_c, lru_conv_w[0], lru_conv_b[0], lru_w_gate[0], lru_b_gate[0],
                         lru_lambda[0], TT)
    xl = _lru_out_tm_proj(y_l, hf, hb, lru_w_o[0].astype(BF16), lru_b_o[0], xl3, mod3[1], 2, TT)
    xl = xl.reshape(n_l, d)
    (out,) = _moe_layer([xl], norm_g[1, 1], mod3[1], lat_row, (3, 4, 5),
                        (moe_w_group[1], moe_b_group[1], moe_w_expert[1], moe_b_expert[1]),
                        1, moe_w1, moe_w3, moe_w2, final_g, TM, BM)
    return out.reshape(b, l, d)
```

```python
import functools

import jax
import jax.numpy as jnp
import numpy as np
from jax import lax
from jax.experimental import pallas as pl
from jax.experimental.pallas import tpu as pltpu

F32 = jnp.float32
BF16 = jnp.bfloat16

GRID_W = 64
N_MOD = 6
NA_HEADS = 16
WIN_H = 8
WIN_W = 16
LRU_BLOCKS = 4
CONV_W = 4
LRU_C = 8.0
N_GROUPS = 4
EXPERTS_PER_GROUP = 8
N_EXPERTS = N_GROUPS * EXPERTS_PER_GROUP
EPS = 1e-6

LANES = 128
MOD_ROWS = 16
NEG = -1e30
VMEM_LIMIT = 56 * 1024 * 1024
HIGHEST = lax.Precision.HIGHEST
LOG2_E = 1.4426950408889634


def _cparams(sem, vmem=VMEM_LIMIT):
    return pltpu.CompilerParams(dimension_semantics=sem, vmem_limit_bytes=vmem)


def _mod_kernel(c_ref, w_ref, b_ref, o_ref):
    c = c_ref[...]
    s = c * jax.nn.sigmoid(c)
    o_ref[0] = jnp.dot(s, w_ref[0], precision=HIGHEST, preferred_element_type=F32) + b_ref[0]


def _modulation(cvec, ada_w, ada_b):
    depth, d, n = ada_w.shape
    tn = 1536
    return pl.pallas_call(
        _mod_kernel,
        grid=(depth, n // tn),
        in_specs=[
            pl.BlockSpec((MOD_ROWS, d), lambda l, j: (0, 0)),
            pl.BlockSpec((1, d, tn), lambda l, j: (l, 0, j)),
            pl.BlockSpec((1, 1, tn), lambda l, j: (l, 0, j)),
        ],
        out_specs=pl.BlockSpec((1, MOD_ROWS, tn), lambda l, j: (l, 0, j)),
        out_shape=jax.ShapeDtypeStruct((depth, MOD_ROWS, n), F32),
        compiler_params=_cparams(("arbitrary", "arbitrary")),
        name="adaln_mod",
    )(cvec, ada_w, ada_b.reshape(depth, 1, n))


def _norm_mod(x, g, sh, sc):
    ms = jnp.mean(x * x, axis=-1, keepdims=True)
    y = x * lax.rsqrt(ms + EPS) * g
    return y * (1.0 + sc) + sh


def _proj_kernel(x_ref, g_ref, sh_ref, sc_ref, w_ref, b_ref, *o_refs, splits, scales):
    h = _norm_mod(x_ref[...], g_ref[...], sh_ref[0], sc_ref[0]).astype(BF16)
    off = 0
    for o_ref, n, s in zip(o_refs, splits, scales):
        y = jnp.dot(h, w_ref[:, off:off + n], preferred_element_type=F32) + b_ref[:, off:off + n]
        if s != 1.0:
            y = y * s
        o_ref[...] = y.astype(o_ref.dtype)
        off += n


def _proj(x2d, g, mod3, row_of_tile, sh_chunk, sc_chunk, w, b, splits, dtypes, scales, tm):
    n_tok, d = x2d.shape
    n_out = w.shape[1]
    assert sum(splits) == n_out and n_tok % tm == 0
    return pl.pallas_call(
        functools.partial(_proj_kernel, splits=tuple(splits), scales=tuple(scales)),
        grid=(n_tok // tm,),
        in_specs=[
            pl.BlockSpec((tm, d), lambda i: (i, 0)),
            pl.BlockSpec((1, d), lambda i: (0, 0)),
            pl.BlockSpec((1, 1, d), lambda i: (row_of_tile(i), 0, sh_chunk)),
            pl.BlockSpec((1, 1, d), lambda i: (row_of_tile(i), 0, sc_chunk)),
            pl.BlockSpec((d, n_out), lambda i: (0, 0)),
            pl.BlockSpec((1, n_out), lambda i: (0, 0)),
        ],
        out_specs=[pl.BlockSpec((tm, n), lambda i: (i, 0)) for n in splits],
        out_shape=[jax.ShapeDtypeStruct((n_tok, n), dt) for n, dt in zip(splits, dtypes)],
        compiler_params=_cparams(("parallel",)),
        name="norm_mod_proj",
    )(x2d, g.reshape(1, d), mod3, mod3, w, b.reshape(1, n_out))


def _resid_kernel(a_ref, w_ref, b_ref, res_ref, gate_ref, o_ref):
    y = jnp.dot(a_ref[...], w_ref[...], preferred_element_type=F32) + b_ref[...]
    o_ref[...] = res_ref[...] + gate_ref[0] * y


def _resid_proj(a, w, b, res, mod3, row_of_tile, gate_chunk, tm):
    n_tok, k = a.shape
    d = w.shape[1]
    return pl.pallas_call(
        _resid_kernel,
        grid=(n_tok // tm,),
        in_specs=[
            pl.BlockSpec((tm, k), lambda i: (i, 0)),
            pl.BlockSpec((k, d), lambda i: (0, 0)),
            pl.BlockSpec((1, d), lambda i: (0, 0)),
            pl.BlockSpec((tm, d), lambda i: (i, 0)),
            pl.BlockSpec((1, 1, d), lambda i: (row_of_tile(i), 0, gate_chunk)),
        ],
        out_specs=pl.BlockSpec((tm, d), lambda i: (i, 0)),
        out_shape=jax.ShapeDtypeStruct((n_tok, d), F32),
        compiler_params=_cparams(("parallel",)),
        name="proj_residual",
    )(a, w, b.reshape(1, d), res, mod3)


HEAD_GROUP = 4


def _head_group_attention(qg, keys, vals, biases):
    n_q, width = qg.shape
    dh = width // HEAD_GROUP
    head_of_lane = lax.broadcasted_iota(jnp.int32, (n_q, width), 1) // dh
    qs = jnp.concatenate([jnp.where(head_of_lane == h, qg, jnp.zeros_like(qg)) for h in range(HEAD_GROUP)],
                         axis=0)
    s_parts = []
    for kk, bb in zip(keys, biases):
        s = lax.dot_general(qs, kk, (((1,), (1,)), ((), ())), preferred_element_type=F32)
        s_parts.append(s if bb is None else s + bb)
    m = s_parts[0].max(axis=-1, keepdims=True)
    for s in s_parts[1:]:
        m = jnp.maximum(m, s.max(axis=-1, keepdims=True))
    p_parts = [jnp.exp2(s - m) for s in s_parts]
    l = p_parts[0].sum(axis=-1, keepdims=True)
    for p in p_parts[1:]:
        l = l + p.sum(axis=-1, keepdims=True)
    o = None
    for p, vv in zip(p_parts, vals):
        t = jnp.dot(p.astype(BF16), vv, preferred_element_type=F32)
        o = t if o is None else o + t
    o = o / l
    out = jnp.where(head_of_lane == 0, o[0:n_q], 0.0)
    for h in range(1, HEAD_GROUP):
        out = jnp.where(head_of_lane == h, o[h * n_q:(h + 1) * n_q], out)
    return out


NA_ROWS_PER_STEP = 4


def _na_kernel(q_ref, k_ref, v_ref, kc_ref, vc_ref, *rest, rows, kh):
    mb_refs, o_ref = rest[:-1], rest[-1]
    n_win = kh * GRID_W
    width = HEAD_GROUP * (q_ref.shape[2] // NA_HEADS)

    for j, mb_ref in enumerate(mb_refs):
        r = pl.program_id(1) * len(mb_refs) + j
        rs = jnp.clip(r - kh // 2, 0, rows - kh)
        k0 = pl.multiple_of(rs * GRID_W, GRID_W)
        q0 = j * GRID_W
        for g in range(NA_HEADS // HEAD_GROUP):
            c0 = g * width
            qg = q_ref[0, q0:q0 + GRID_W, c0:c0 + width]
            kw = k_ref[0, pl.ds(k0, n_win), c0:c0 + width]
            vw = v_ref[0, pl.ds(k0, n_win), c0:c0 + width]
            kc = kc_ref[0, :, c0:c0 + width]
            vc = vc_ref[0, :, c0:c0 + width]
            bias = mb_ref[0, g * HEAD_GROUP:(g + 1) * HEAD_GROUP].reshape(HEAD_GROUP * GRID_W, n_win)
            o = _head_group_attention(qg, [kw, kc], [vw, vc], [bias, None])
            o_ref[0, q0:q0 + GRID_W, c0:c0 + width] = o.astype(o_ref.dtype)


def _window_bias(rpb, rows):
    kh = min(WIN_H, rows)
    cols = np.arange(GRID_W)
    col_start = np.clip(cols - WIN_W // 2, 0, GRID_W - WIN_W)
    ck = np.arange(GRID_W)
    in_win = (ck[None, :] >= col_start[:, None]) & (ck[None, :] < col_start[:, None] + WIN_W)
    dc = ck[None, :] - cols[:, None] + (WIN_W - 1)
    n_dc = 2 * WIN_W - 1
    sel = ((dc[None] == np.arange(n_dc)[:, None, None]) & in_win[None]).astype(np.float32)
    t = jnp.einsum("hrc,cqk->hrqk", rpb.astype(F32), jnp.asarray(sel), precision=HIGHEST)
    t = jnp.where(jnp.asarray(in_win)[None, None], t, NEG)
    per_delta = []
    for delta in range(kh):
        r0 = WIN_H - 1 - delta
        per_delta.append(jnp.transpose(t[:, r0:r0 + kh], (0, 2, 1, 3)).reshape(NA_HEADS, GRID_W, kh * GRID_W))
    return jnp.stack(per_delta) * LOG2_E


def _na_attention(q, k, v, kc, vc, mb):
    b, l, d = q.shape
    c = kc.shape[1]
    rows = l // GRID_W
    kh = mb.shape[0]

    rps = NA_ROWS_PER_STEP
    assert rows % rps == 0

    def bias_spec(j):
        def index(bi, s):
            r = s * rps + j
            return (r - jnp.clip(r - kh // 2, 0, rows - kh), 0, 0, 0)
        return pl.BlockSpec((1, NA_HEADS, GRID_W, kh * GRID_W), index, pipeline_mode=pl.Buffered(1))

    return pl.pallas_call(
        functools.partial(_na_kernel, rows=rows, kh=kh),
        grid=(b, rows // rps),
        in_specs=[
            pl.BlockSpec((1, rps * GRID_W, d), lambda bi, s: (bi, s, 0)),
            pl.BlockSpec((1, l, d), lambda bi, s: (bi, 0, 0)),
            pl.BlockSpec((1, l, d), lambda bi, s: (bi, 0, 0)),
            pl.BlockSpec((1, c, d), lambda bi, s: (bi, 0, 0)),
            pl.BlockSpec((1, c, d), lambda bi, s: (bi, 0, 0)),
        ] + [bias_spec(j) for j in range(rps)],
        out_specs=pl.BlockSpec((1, rps * GRID_W, d), lambda bi, s: (bi, s, 0)),
        out_shape=jax.ShapeDtypeStruct((b, l, d), BF16),
        compiler_params=_cparams(("parallel", "arbitrary")),
        name="na_attention",
    )(q, k, v, kc, vc, *([mb] * rps))


def _ctx_attn_kernel(q_ref, k_ref, v_ref, o_ref):
    width = HEAD_GROUP * (q_ref.shape[2] // NA_HEADS)

    def group(g, carry):
        c0 = pl.multiple_of(g * width, width)
        o = _head_group_attention(q_ref[0, :, pl.ds(c0, width)], [k_ref[0, :, pl.ds(c0, width)]],
                                  [v_ref[0, :, pl.ds(c0, width)]], [None])
        o_ref[0, :, pl.ds(c0, width)] = o.astype(o_ref.dtype)
        return carry

    lax.fori_loop(0, NA_HEADS // HEAD_GROUP, group, 0)


def _ctx_attention(qc, kc, vc):
    b, c, d = qc.shape
    spec = pl.BlockSpec((1, c, d), lambda bi: (bi, 0, 0))
    return pl.pallas_call(
        _ctx_attn_kernel,
        grid=(b,),
        in_specs=[spec, spec, spec],
        out_specs=spec,
        out_shape=jax.ShapeDtypeStruct((b, c, d), BF16),
        compiler_params=_cparams(("parallel",)),
        name="ctx_attention",
    )(qc, kc, vc)


GROUP_LANE0 = N_EXPERTS


def _src_specs(srcs, tm):
    d = srcs[0].shape[1]
    tiles = [s.shape[0] // tm for s in srcs]
    specs, first = [], 0
    for t in tiles:
        specs.append(pl.BlockSpec((tm, d), functools.partial(
            lambda i, first, t: (jnp.clip(i - first, 0, t - 1), 0), first=first, t=t)))
        first += t
    return specs, tiles


def _select_src(i, refs, tiles):
    x = refs[-1][...]
    first = sum(tiles[:-1])
    for ref, t in zip(refs[-2::-1], tiles[-2::-1]):
        x = jnp.where(i < first, ref[...], x)
        first -= t
    return x


TOK_ROWS = 8


def _store_token_major(ref, x):
    n, d = x.shape
    assert d == TOK_ROWS * LANES
    for j in range(TOK_ROWS):
        ref[pl.ds(j, n, stride=TOK_ROWS), :] = x[:, j * LANES:(j + 1) * LANES]


def _load_token_major(ref, n):
    return jnp.concatenate([ref[pl.ds(j, n, stride=TOK_ROWS), :] for j in range(TOK_ROWS)], axis=1)


def _router_kernel(*refs, tiles):
    n_src = len(tiles)
    x_refs = refs[:n_src]
    g_ref, sh_ref, sc_ref, wr_ref, br_ref, h_ref, oh_ref, wd_ref, cnt_ref = refs[n_src:]
    i = pl.program_id(0)
    h = _norm_mod(_select_src(i, x_refs, tiles), g_ref[...], sh_ref[0], sc_ref[0])
    _store_token_major(h_ref, h)
    h_hi = h.astype(BF16)
    h_lo = (h - h_hi.astype(F32)).astype(BF16)
    wr = wr_ref[...]
    w_hi = wr.astype(BF16)
    w_lo = (wr - w_hi.astype(F32)).astype(BF16)
    logits = (jnp.dot(h_hi, w_hi, preferred_element_type=F32)
              + jnp.dot(h_lo, w_hi, preferred_element_type=F32)
              + jnp.dot(h_hi, w_lo, preferred_element_type=F32)) + br_ref[...]
    tm = logits.shape[0]
    lane = lax.broadcasted_iota(jnp.int32, (tm, LANES), 1)
    big = jnp.int32(LANES)

    gmask = (lane >= GROUP_LANE0) & (lane < GROUP_LANE0 + N_GROUPS)
    gl = jnp.where(gmask, logits, NEG)
    gmax = gl.max(axis=-1, keepdims=True)
    gsel = jnp.where(gmask & (gl == gmax), lane, big).min(axis=-1, keepdims=True) - GROUP_LANE0
    g_w = 1.0 / jnp.where(gmask, jnp.exp(gl - gmax), 0.0).sum(axis=-1, keepdims=True)

    e0 = gsel * EXPERTS_PER_GROUP
    emask = (lane >= e0) & (lane < e0 + EXPERTS_PER_GROUP)
    el = jnp.where(emask, logits, NEG)
    v1 = el.max(axis=-1, keepdims=True)
    i1 = jnp.where(emask & (el == v1), lane, big).min(axis=-1, keepdims=True)
    el2 = jnp.where(lane == i1, NEG, el)
    v2 = el2.max(axis=-1, keepdims=True)
    i2 = jnp.where(emask & (lane != i1) & (el2 == v2), lane, big).min(axis=-1, keepdims=True)
    t = jnp.exp(v2 - v1)
    w1 = g_w / (1.0 + t)
    w2 = g_w * t / (1.0 + t)

    sel1 = lane == i1
    sel2 = lane == i2
    oh = jnp.where(sel1 | sel2, 1.0, 0.0)
    oh_ref[...] = oh.astype(oh_ref.dtype)
    wd_ref[...] = jnp.where(sel1, w1, jnp.where(sel2, w2, 0.0))

    @pl.when(i == 0)
    def _():
        cnt_ref[...] = jnp.zeros_like(cnt_ref)

    cnt_ref[...] += oh.sum(axis=0, keepdims=True)


def _router(srcs, g, mod3, row_of_tile, sh_chunk, sc_chunk, wr, br, tm):
    d = srcs[0].shape[1]
    src_specs, tiles = _src_specs(srcs, tm)
    n = sum(tiles) * tm
    tok = pl.BlockSpec((tm, LANES), lambda i: (i, 0))
    return pl.pallas_call(
        functools.partial(_router_kernel, tiles=tuple(tiles)),
        grid=(sum(tiles),),
        in_specs=src_specs + [
            pl.BlockSpec((1, d), lambda i: (0, 0)),
            pl.BlockSpec((1, 1, d), lambda i: (row_of_tile(i), 0, sh_chunk)),
            pl.BlockSpec((1, 1, d), lambda i: (row_of_tile(i), 0, sc_chunk)),
            pl.BlockSpec((d, LANES), lambda i: (0, 0)),
            pl.BlockSpec((1, LANES), lambda i: (0, 0)),
        ],
        out_specs=[pl.BlockSpec((tm * TOK_ROWS, LANES), lambda i: (i, 0)), tok, tok,
                   pl.BlockSpec((1, LANES), lambda i: (0, 0))],
        out_shape=[jax.ShapeDtypeStruct((n * TOK_ROWS, LANES), F32), jax.ShapeDtypeStruct((n, LANES), BF16),
                   jax.ShapeDtypeStruct((n, LANES), F32), jax.ShapeDtypeStruct((1, LANES), F32)],
        compiler_params=_cparams(("arbitrary",)),
        name="moe_router",
    )(*srcs, g.reshape(1, d), mod3, mod3, wr, br)


def _dest_kernel(start_ref, oh_ref, wd_ref, dest_ref, wsel_ref, carry_ref):
    i = pl.program_id(0)

    @pl.when(i == 0)
    def _():
        carry_ref[...] = jnp.zeros_like(carry_ref)

    oh = oh_ref[...]
    tm = oh.shape[0]
    row = lax.broadcasted_iota(jnp.int32, (tm, tm), 0)
    col = lax.broadcasted_iota(jnp.int32, (tm, tm), 1)
    tri = jnp.where(row > col, 1.0, 0.0).astype(BF16)
    base = jnp.dot(tri, oh, preferred_element_type=F32) + carry_ref[...] + start_ref[...]
    ohf = oh.astype(F32)
    lane = lax.broadcasted_iota(jnp.int32, (tm, LANES), 1)
    sel = ohf > 0.0
    ea = jnp.where(sel, lane, LANES).min(axis=-1, keepdims=True)
    eb = jnp.where(sel, lane, -1).max(axis=-1, keepdims=True)
    wd = wd_ref[...]
    pick = lambda e, val: jnp.where(lane == e, val, 0.0).sum(axis=-1, keepdims=True)
    dcols = jnp.where(lane == 0, pick(ea, base), jnp.where(lane == 1, pick(eb, base), 0.0))
    dest_ref[0] = dcols.T[0:2, :].astype(jnp.int32)
    two = lax.broadcasted_iota(jnp.int32, (tm, 2), 1) == 0
    wsel_ref[...] = jnp.where(two, pick(ea, wd), pick(eb, wd))
    carry_ref[...] += ohf.sum(axis=0, keepdims=True)


def _dest(start, oh, wd, tm):
    n = oh.shape[0]
    tok = pl.BlockSpec((tm, LANES), lambda i: (i, 0))
    two = pl.BlockSpec((tm, 2), lambda i: (i, 0))
    return pl.pallas_call(
        _dest_kernel,
        grid=(n // tm,),
        in_specs=[pl.BlockSpec((1, LANES), lambda i: (0, 0)), tok, tok],
        out_specs=[pl.BlockSpec((1, 2, tm), lambda i: (i, 0, 0)), two],
        out_shape=[jax.ShapeDtypeStruct((n // tm, 2, tm), jnp.int32), jax.ShapeDtypeStruct((n, 2), F32)],
        scratch_shapes=[pltpu.VMEM((1, LANES), F32)],
        compiler_params=_cparams(("arbitrary",)),
        name="moe_dest",
    )(start, oh, wd)


def _tok_rows(t):
    return pl.ds(t * TOK_ROWS, TOK_ROWS)


def _wait_tokens(buf, n, sem):
    whole = buf.at[pl.ds(0, n * TOK_ROWS), :]
    pltpu.make_async_copy(whole, whole, sem).wait()


def _scatter_kernel(dest_ref, h_ref, xs_ref, sem):
    tm = h_ref.shape[0] // TOK_ROWS
    for r in range(tm):
        for k in range(2):
            pltpu.make_async_copy(h_ref.at[_tok_rows(r), :], xs_ref.at[dest_ref[0, k, r]], sem).start(priority=k)
    for k in range(2):
        _wait_tokens(h_ref, tm, sem)


def _scatter_rows(dest, h, tm):
    n = h.shape[0] // TOK_ROWS
    return pl.pallas_call(
        _scatter_kernel,
        grid=(n // tm,),
        in_specs=[
            pl.BlockSpec((1, 2, tm), lambda i: (i, 0, 0), memory_space=pltpu.SMEM),
            pl.BlockSpec((tm * TOK_ROWS, LANES), lambda i: (i, 0)),
        ],
        out_specs=pl.BlockSpec(memory_space=pl.ANY),
        out_shape=jax.ShapeDtypeStruct((2 * n, TOK_ROWS, LANES), F32),
        scratch_shapes=[pltpu.SemaphoreType.DMA(())],
        compiler_params=_cparams(("arbitrary",)),
        name="moe_scatter",
    )(dest, h)


def _gmm_kernel(vb_ref, ve_ref, lo_ref, hi_ref, x_ref, w1_ref, w3_ref, w2_ref, o_ref, w1b, w3b, w2b, *, bm):
    v = pl.program_id(0)
    pv = jnp.maximum(v - 1, 0)

    @pl.when((v == 0) | (ve_ref[v] != ve_ref[pv]))
    def _():
        w1b[...] = w1_ref[0, 0].astype(BF16)
        w3b[...] = w3_ref[0, 0].astype(BF16)
        w2b[...] = w2_ref[0, 0].astype(BF16)

    lo = lo_ref[v]
    hi = hi_ref[v]

    @pl.when(hi > lo)
    def _():
        xb = _load_token_major(x_ref, bm).astype(BF16)
        a = jnp.dot(xb, w1b[...], preferred_element_type=F32)
        g = jnp.dot(xb, w3b[...], preferred_element_type=F32)
        hdn = (a * jax.nn.sigmoid(a) * g).astype(BF16)
        y = jnp.dot(hdn, w2b[...], preferred_element_type=F32)
        rows = lax.broadcasted_iota(jnp.int32, (bm, 1), 0)
        mine = (rows >= lo) & (rows < hi)
        first = (v == 0) | (vb_ref[v] != vb_ref[pv])

        @pl.when(first)
        def _():
            _store_token_major(o_ref, jnp.where(mine, y, 0.0))

        @pl.when(jnp.logical_not(first))
        def _():
            _store_token_major(o_ref, jnp.where(mine, y, _load_token_major(o_ref, bm)))


def _gmm(visits, xs, layer, w1, w3, w2, bm):
    d, f = w1.shape[2:]
    n_vis = visits[0].shape[0]
    blk = pl.BlockSpec((bm * TOK_ROWS, LANES), lambda v, vb, ve, lo, hi: (vb[v], 0))
    grid_spec = pltpu.PrefetchScalarGridSpec(
        num_scalar_prefetch=4,
        grid=(n_vis,),
        in_specs=[
            blk,
            pl.BlockSpec((1, 1, d, f), lambda v, vb, ve, lo, hi: (layer, ve[v], 0, 0)),
            pl.BlockSpec((1, 1, d, f), lambda v, vb, ve, lo, hi: (layer, ve[v], 0, 0)),
            pl.BlockSpec((1, 1, f, d), lambda v, vb, ve, lo, hi: (layer, ve[v], 0, 0)),
        ],
        out_specs=blk,
        scratch_shapes=[pltpu.VMEM((d, f), BF16), pltpu.VMEM((d, f), BF16), pltpu.VMEM((f, d), BF16)],
    )
    return pl.pallas_call(
        functools.partial(_gmm_kernel, bm=bm),
        grid_spec=grid_spec,
        out_shape=jax.ShapeDtypeStruct(xs.shape, F32),
        compiler_params=_cparams(("arbitrary",)),
        name="moe_experts",
    )(*visits, xs, w1, w3, w2)


def _visit_plan(counts, n_rows, bm):
    counts = counts.astype(jnp.int32)
    end = jnp.cumsum(counts)
    start = end - counts
    n_blocks = n_rows // bm
    n_vis = n_blocks + N_EXPERTS
    tiles = jnp.where(counts > 0, (end - 1) // bm - start // bm + 1, 0)
    vend = jnp.cumsum(tiles)
    vstart = vend - tiles
    v = jnp.arange(n_vis, dtype=jnp.int32)[:, None]
    owns = (vstart[None, :] <= v) & (v < vend[None, :])
    take = lambda a: jnp.sum(jnp.where(owns, a[None, :], 0), axis=1)
    valid = v[:, 0] < vend[-1]
    last_e = jnp.max(jnp.where(counts > 0, jnp.arange(N_EXPERTS, dtype=jnp.int32), 0))
    e = jnp.where(valid, take(jnp.arange(N_EXPERTS, dtype=jnp.int32)), last_e)
    blk = jnp.where(valid, take(start // bm - vstart) + v[:, 0], n_blocks - 1)
    lo = jnp.where(valid, jnp.clip(take(start) - blk * bm, 0, bm), 0)
    hi = jnp.where(valid, jnp.clip(take(end) - blk * bm, 0, bm), 0)
    return start, (blk.astype(jnp.int32), e.astype(jnp.int32), lo.astype(jnp.int32), hi.astype(jnp.int32))


def _combine_kernel(dest_ref, dest_next_ref, ys_ref, *refs, tiles, final_norm):
    n_src = len(tiles)
    x_refs = refs[:n_src]
    w_ref, gate_ref = refs[n_src:n_src + 2]
    rest = refs[n_src + 2:]
    fg_ref = rest[0] if final_norm else None
    o_refs = rest[-n_src - 2:-2]
    buf, sems = rest[-2:]
    i = pl.program_id(0)
    n_steps = pl.num_programs(0)
    tm = x_refs[0].shape[0]

    def gather(d_ref, slot):
        for r in range(tm):
            for k in range(2):
                pltpu.make_async_copy(ys_ref.at[d_ref[0, k, r]], buf.at[slot, k, _tok_rows(r), :],
                                      sems.at[slot]).start(priority=k)

    @pl.when(i == 0)
    def _():
        gather(dest_ref, 0)

    @pl.when(i + 1 < n_steps)
    def _():
        gather(dest_next_ref, (i + 1) % 2)

    slot = i % 2
    for k in range(2):
        _wait_tokens(buf.at[slot, k], tm, sems.at[slot])
    w = w_ref[...]
    moe = (w[:, 0:1] * _load_token_major(buf.at[slot, 0], tm)
           + w[:, 1:2] * _load_token_major(buf.at[slot, 1], tm))
    y = _select_src(i, x_refs, tiles) + gate_ref[0] * moe
    if final_norm:
        y = y * lax.rsqrt(jnp.mean(y * y, axis=-1, keepdims=True) + EPS) * fg_ref[...]

    first = 0
    for o_ref, t in zip(o_refs, tiles):
        @pl.when((i >= first) & (i < first + t))
        def _(o_ref=o_ref):
            o_ref[...] = y
        first += t


def _combine(dest, wsel, ys, srcs, mod3, row_of_tile, gate_chunk, final_g, tm):
    d = srcs[0].shape[1]
    src_specs, tiles = _src_specs(srcs, tm)
    final_norm = final_g is not None
    extra_specs = [pl.BlockSpec((1, d), lambda i: (0, 0))] if final_norm else []
    extra_args = [final_g.reshape(1, d)] if final_norm else []
    n_steps = sum(tiles)
    return pl.pallas_call(
        functools.partial(_combine_kernel, tiles=tuple(tiles), final_norm=final_norm),
        grid=(n_steps,),
        in_specs=[
            pl.BlockSpec((1, 2, tm), lambda i: (i, 0, 0), memory_space=pltpu.SMEM),
            pl.BlockSpec((1, 2, tm), lambda i: (jnp.minimum(i + 1, n_steps - 1), 0, 0), memory_space=pltpu.SMEM),
            pl.BlockSpec(memory_space=pl.ANY),
        ] + src_specs + [
            pl.BlockSpec((tm, 2), lambda i: (i, 0)),
            pl.BlockSpec((1, 1, d), lambda i: (row_of_tile(i), 0, gate_chunk)),
        ] + extra_specs,
        out_specs=list(src_specs),
        out_shape=[jax.ShapeDtypeStruct(s.shape, F32) for s in srcs],
        scratch_shapes=[pltpu.VMEM((2, 2, tm * TOK_ROWS, LANES), F32), pltpu.SemaphoreType.DMA((2,))],
        compiler_params=_cparams(("arbitrary",)),
        name="moe_combine",
    )(dest, dest, ys, *srcs, wsel, mod3, *extra_args)


def _router_weights(w_group, b_group, w_expert, b_expert):
    d = w_group.shape[0]
    we = jnp.transpose(w_expert, (1, 0, 2)).reshape(d, N_EXPERTS)
    wr = jnp.concatenate([we, w_group, jnp.zeros((d, LANES - N_EXPERTS - N_GROUPS), F32)], axis=1)
    br = jnp.concatenate([b_expert.reshape(N_EXPERTS), b_group,
                          jnp.zeros((LANES - N_EXPERTS - N_GROUPS,), F32)]).reshape(1, LANES)
    return wr, br


def _moe_layer(srcs, g, mod3, row_of_tile, chunks, router_w, layer, w1, w3, w2, final_g, tm, bm):
    sh_chunk, sc_chunk, gate_chunk = chunks
    wr, br = _router_weights(*router_w)
    h, oh, wd, cnt = _router(srcs, g, mod3, row_of_tile, sh_chunk, sc_chunk, wr, br, tm)
    n = oh.shape[0]
    start, visits = _visit_plan(cnt[0, :N_EXPERTS], 2 * n, bm)
    start_row = jnp.zeros((1, LANES), F32).at[0, :N_EXPERTS].set(start.astype(F32))
    dest, wsel = _dest(start_row, oh, wd, tm)
    xs = _scatter_rows(dest, h, tm)
    ys = _gmm(visits, xs.reshape(2 * n * TOK_ROWS, LANES), layer, w1, w3, w2, bm)
    return _combine(dest, wsel, ys.reshape(2 * n, TOK_ROWS, LANES), srcs, mod3, row_of_tile, gate_chunk,
                    final_g, tm)


def _gelu_tanh(x):
    return x * (0.5 * (1.0 + jnp.tanh(0.7978845608028654 * (x + 0.044715 * (x * x * x)))))


def _lru_in_kernel(x_ref, g_ref, sh_ref, sc_ref, w_ref, b_ref, *o_refs, with_y):
    nb, tt, d = x_ref.shape
    h = _norm_mod(x_ref[...], g_ref[...], sh_ref[...], sc_ref[...]).reshape(nb * tt, d).astype(BF16)
    lw = o_refs[-1].shape[2]
    if with_y:
        y = jnp.dot(h, w_ref[:, 0:lw], preferred_element_type=F32) + b_ref[:, 0:lw]
        o_refs[0][...] = y.reshape(nb, tt, lw).astype(o_refs[0].dtype)
    off = w_ref.shape[1] - lw
    u = jnp.dot(h, w_ref[:, off:off + lw], preferred_element_type=F32) + b_ref[:, off:off + lw]
    o_refs[-1][...] = jnp.swapaxes(u.reshape(nb, tt, lw), 0, 1)


def _lru_in_proj(x3, g, mod3, mod_rows, w, b, with_y, tt):
    nb, t, d = x3.shape
    lw = w.shape[1] // 2 if with_y else w.shape[1]
    r0, nr = mod_rows
    mod_spec = lambda chunk: pl.BlockSpec((nr, 1, d), lambda i: (r0 // nr, 0, chunk))
    out_specs = [pl.BlockSpec((tt, nb, lw), lambda i: (i, 0, 0))]
    out_shape = [jax.ShapeDtypeStruct((t, nb, lw), F32)]
    if with_y:
        out_specs.insert(0, pl.BlockSpec((nb, tt, lw), lambda i: (0, i, 0)))
        out_shape.insert(0, jax.ShapeDtypeStruct((nb, t, lw), BF16))
    return pl.pallas_call(
        functools.partial(_lru_in_kernel, with_y=with_y),
        grid=(t // tt,),
        in_specs=[
            pl.BlockSpec((nb, tt, d), lambda i: (0, i, 0)),
            pl.BlockSpec((1, d), lambda i: (0, 0)),
            mod_spec(0), mod_spec(1),
            pl.BlockSpec(w.shape, lambda i: (0, 0)),
            pl.BlockSpec((1, w.shape[1]), lambda i: (0, 0)),
        ],
        out_specs=out_specs,
        out_shape=out_shape,
        compiler_params=_cparams(("parallel",)),
        name="rglru_in_proj",
    )(x3, g.reshape(1, d), mod3, mod3, w, b.reshape(1, w.shape[1]))


def _lru_tm_kernel(u_ref, up_ref, un_ref, cw_ref, cb_ref, wg_ref, bg_ref, lam_ref, h0_ref, *rest,
                   tt, n_t, reverse, has_other):
    other_ref = rest[0] if has_other else None
    h_ref, hend_ref, a_s, b_s, carry = rest[-5:]
    i = pl.program_id(0)
    ti = (n_t - 1 - i) if reverse else i
    _, nb, w = u_ref.shape
    cb = w // LRU_BLOCKS
    left = CONV_W // 2
    right = CONV_W - 1 - left

    @pl.when(i == 0)
    def _():
        carry[...] = h0_ref[...]

    prev = jnp.where(ti > 0, up_ref[...], 0.0)
    nxt = jnp.where(ti < n_t - 1, un_ref[...], 0.0)
    for n in range(LRU_BLOCKS):
        c0 = n * cb
        ext = jnp.concatenate([prev[:, :, c0:c0 + cb], u_ref[:, :, c0:c0 + cb], nxt[:, :, c0:c0 + cb]], axis=0)
        uc = cb_ref[:, c0:c0 + cb] + cw_ref[0:1, c0:c0 + cb] * ext[0:tt]
        for kk in range(1, CONV_W):
            uc = uc + cw_ref[kk:kk + 1, c0:c0 + cb] * ext[kk:kk + tt]
        ub = uc.reshape(tt * nb, cb).astype(BF16)
        r = jax.nn.sigmoid(jnp.dot(ub, wg_ref[0, n], preferred_element_type=F32) + bg_ref[0, :, c0:c0 + cb])
        ig = jax.nn.sigmoid(jnp.dot(ub, wg_ref[1, n], preferred_element_type=F32) + bg_ref[1, :, c0:c0 + cb])
        lam = lam_ref[:, c0:c0 + cb]
        rate = (-LRU_C * LOG2_E) * jnp.log1p(jnp.exp(-lam))
        a = jnp.exp2(r * rate)
        z = 1.0 - a * a
        root = jnp.where(z > 0.0, z * lax.rsqrt(z), 0.0)
        b = root * (ig * uc.reshape(tt * nb, cb))
        a_s[:, :, c0:c0 + cb] = a.reshape(tt, nb, cb)
        b_s[:, :, c0:c0 + cb] = b.reshape(tt, nb, cb)

    def step(s, h):
        t = (tt - 1 - s) if reverse else s
        h = a_s[t] * h + b_s[t]
        h_ref[t] = (h + other_ref[t]) if has_other else h
        return h

    h_last = lax.fori_loop(0, tt, step, carry[...], unroll=8)
    carry[...] = h_last

    @pl.when(i == n_t - 1)
    def _():
        hend_ref[...] = h_last


def _lru_tm_scan(u, h0, conv_w, conv_b, wg, bg, lam, reverse, tt, other=None):
    t, nb, w = u.shape
    n_t = t // tt
    left = CONV_W // 2
    right = CONV_W - 1 - left
    assert tt % left == 0 and t % tt == 0
    tmap = (lambda i: n_t - 1 - i) if reverse else (lambda i: i)
    full = lambda a: pl.BlockSpec(a.shape, lambda i: (0,) * a.ndim)
    conv_b = conv_b.reshape(1, w)
    bg = bg.reshape(2, 1, w)
    lam = lam.reshape(1, w)
    tile = pl.BlockSpec((tt, nb, w), lambda i: (tmap(i), 0, 0))
    others = [] if other is None else [other]
    return pl.pallas_call(
        functools.partial(_lru_tm_kernel, tt=tt, n_t=n_t, reverse=reverse, has_other=other is not None),
        grid=(n_t,),
        in_specs=[
            tile,
            pl.BlockSpec((left, nb, w), lambda i: (jnp.maximum(tmap(i) * (tt // left) - 1, 0), 0, 0)),
            pl.BlockSpec((right, nb, w), lambda i: (jnp.minimum((tmap(i) + 1) * (tt // right), t // right - 1), 0, 0)),
            full(conv_w), full(conv_b), full(wg), full(bg), full(lam), full(h0),
        ] + [tile] * len(others),
        out_specs=[tile, full(h0)],
        out_shape=[jax.ShapeDtypeStruct((t, nb, w), F32), jax.ShapeDtypeStruct(h0.shape, F32)],
        scratch_shapes=[pltpu.VMEM((tt, nb, w), F32), pltpu.VMEM((tt, nb, w), F32), pltpu.VMEM(h0.shape, F32)],
        compiler_params=_cparams(("arbitrary",)),
        name="rglru_scan_rev" if reverse else "rglru_scan_fwd",
    )(u, u, u, conv_w, conv_b, wg, bg, lam, h0, *others)


def _lru_out_tm_kernel(y_ref, h_ref, w_ref, b_ref, res_ref, gate_ref, o_ref):
    nb, tt, lw = y_ref.shape
    y = jnp.swapaxes(y_ref[...].astype(F32), 0, 1)
    a = (_gelu_tanh(y) * h_ref[...]).reshape(tt * nb, lw).astype(BF16)
    z = jnp.dot(a, w_ref[...], preferred_element_type=F32) + b_ref[...]
    z = jnp.swapaxes(z.reshape(tt, nb, z.shape[1]), 0, 1)
    o_ref[...] = res_ref[...] + gate_ref[...] * z


def _lru_out_tm_proj(y, h, w, b, res3, mod3, gate_chunk, tt):
    nb, t, lw = y.shape
    d = w.shape[1]
    tm_spec = pl.BlockSpec((tt, nb, lw), lambda i: (i, 0, 0))
    return pl.pallas_call(
        _lru_out_tm_kernel,
        grid=(t // tt,),
        in_specs=[
            pl.BlockSpec((nb, tt, lw), lambda i: (0, i, 0)),
            tm_spec,
            pl.BlockSpec((lw, d), lambda i: (0, 0)),
            pl.BlockSpec((1, d), lambda i: (0, 0)),
            pl.BlockSpec((nb, tt, d), lambda i: (0, i, 0)),
            pl.BlockSpec((nb, 1, d), lambda i: (0, 0, gate_chunk)),
        ],
        out_specs=pl.BlockSpec((nb, tt, d), lambda i: (0, i, 0)),
        out_shape=jax.ShapeDtypeStruct((nb, t, d), F32),
        compiler_params=_cparams(("parallel",)),
        name="rglru_out_proj",
    )(y, h, w, b.reshape(1, d), res3, mod3)


def _lru_states(ul, uc, conv_w, conv_b, w_gate, b_gate, lam, tt):
    wg = w_gate.astype(BF16)
    _, nb, w = ul.shape
    zero = jnp.zeros((nb, w), F32)
    h = None
    for dirn in range(2):
        rev = dirn == 1
        args = (conv_w, conv_b, wg[dirn], b_gate[dirn], lam[dirn], rev, tt)
        _, h_end = _lru_tm_scan(uc, zero, *args)
        h, _ = _lru_tm_scan(ul, h_end, *args, other=h)
    return h


TM = 512
BM = 512
TT = 64


def kernel(x, c, ctx, c_ctx, ada_w, ada_b, norm_g, na_w_qkv, na_b_qkv, na_rpb, na_w_o, na_b_o, lru_w_in, lru_b_in, lru_conv_w, lru_conv_b, lru_w_gate, lru_b_gate, lru_lambda, lru_w_o, lru_b_o, moe_w_group, moe_b_group, moe_w_expert, moe_b_expert, moe_w1, moe_w3, moe_w2, final_g):
    b, l, d = x.shape
    n_ctx = ctx.shape[1]
    n_l, n_c = b * l, b * n_ctx
    assert l % TM == 0 and n_ctx <= TM and TM % n_ctx == 0 and b + 1 <= MOD_ROWS
    ctx_row = b
    tiles_per_batch = l // TM
    lat_row = lambda i: i // tiles_per_batch
    ctx_tile_row = lambda i: ctx_row
    nl_tiles = n_l // TM
    both_row = lambda i: jnp.where(i < nl_tiles, i // tiles_per_batch, ctx_row)

    cvec = jnp.concatenate([c, c_ctx[None], jnp.zeros((MOD_ROWS - b - 1, d), F32)], axis=0)
    mod = _modulation(cvec, ada_w, ada_b)
    mod3 = [mod[i].reshape(MOD_ROWS, 1, N_MOD * d) for i in range(mod.shape[0])]
    xl = x.reshape(n_l, d)
    xc = ctx.reshape(n_c, d)

    w_qkv = na_w_qkv[0].astype(BF16)
    qk_scale = (d // NA_HEADS) ** -0.5 * LOG2_E
    q, k, v = _proj(xl, norm_g[0, 0], mod3[0], lat_row, 0, 1, w_qkv, na_b_qkv[0],
                    (d, d, d), (BF16,) * 3, (qk_scale, 1.0, 1.0), TM)
    qc, kc, vc = _proj(xc, norm_g[0, 0], mod3[0], ctx_tile_row, 0, 1, w_qkv, na_b_qkv[0],
                       (d, d, d), (BF16,) * 3, (qk_scale, 1.0, 1.0), n_ctx)
    to3 = lambda a, s: a.reshape(b, s, d)
    mb = _window_bias(na_rpb[0], l // GRID_W)
    o_l = _na_attention(to3(q, l), to3(k, l), to3(v, l), to3(kc, n_ctx), to3(vc, n_ctx), mb)
    o_c = _ctx_attention(to3(qc, n_ctx), to3(kc, n_ctx), to3(vc, n_ctx))
    w_o = na_w_o[0].astype(BF16)
    xl = _resid_proj(o_l.reshape(n_l, d), w_o, na_b_o[0], xl, mod3[0], lat_row, 2, TM)
    xc = _resid_proj(o_c.reshape(n_c, d), w_o, na_b_o[0], xc, mod3[0], ctx_tile_row, 2, n_ctx)
    xl, xc = _moe_layer([xl, xc], norm_g[0, 1], mod3[0], both_row, (3, 4, 5),
                        (moe_w_group[0], moe_b_group[0], moe_w_expert[0], moe_b_expert[0]),
                        0, moe_w1, moe_w3, moe_w2, None, TM, BM)

    w_in = lru_w_in[0].astype(BF16)
    lw = w_in.shape[1] // 2
    xl3 = xl.reshape(b, l, d)
    y_l, u_l = _lru_in_proj(xl3, norm_g[1, 0], mod3[1], (0, b), w_in, lru_b_in[0], True, TT)
    (u_c,) = _lru_in_proj(xc.reshape(b, n_ctx, d), norm_g[1, 0], mod3[1], (ctx_row, 1), w_in[:, lw:],
                          lru_b_in[0, lw:], False, TT)
    h_sum = _lru_states(u_l, u_c, lru_conv_w[0], lru_conv_b[0], lru_w_gate[0], lru_b_gate[0],
                        lru_lambda[0], TT)
    xl = _lru_out_tm_proj(y_l, h_sum, lru_w_o[0].astype(BF16), lru_b_o[0], xl3, mod3[1], 2, TT)
    xl = xl.reshape(n_l, d)
    (out,) = _moe_layer([xl], norm_g[1, 1], mod3[1], lat_row, (3, 4, 5),
                        (moe_w_group[1], moe_b_group[1], moe_w_expert[1], moe_b_expert[1]),
                        1, moe_w1, moe_w3, moe_w2, final_g, TM, BM)
    return out.reshape(b, l, d)
```

```python
import functools

import jax
import jax.numpy as jnp
import numpy as np
from jax import lax
from jax.experimental import pallas as pl
from jax.experimental.pallas import tpu as pltpu

F32 = jnp.float32
BF16 = jnp.bfloat16

GRID_W = 64
N_MOD = 6
NA_HEADS = 16
WIN_H = 8
WIN_W = 16
LRU_BLOCKS = 4
CONV_W = 4
LRU_C = 8.0
N_GROUPS = 4
EXPERTS_PER_GROUP = 8
N_EXPERTS = N_GROUPS * EXPERTS_PER_GROUP
EPS = 1e-6

LANES = 128
MOD_ROWS = 16
NEG = -1e30
VMEM_LIMIT = 56 * 1024 * 1024
HIGHEST = lax.Precision.HIGHEST
LOG2_E = 1.4426950408889634


def _cparams(sem, vmem=VMEM_LIMIT):
    return pltpu.CompilerParams(dimension_semantics=sem, vmem_limit_bytes=vmem)


def _mod_kernel(c_ref, w_ref, b_ref, o_ref):
    c = c_ref[...]
    s = c * jax.nn.sigmoid(c)
    o_ref[0] = jnp.dot(s, w_ref[0], precision=HIGHEST, preferred_element_type=F32) + b_ref[0]


def _modulation(cvec, ada_w, ada_b):
    depth, d, n = ada_w.shape
    tn = 1536
    return pl.pallas_call(
        _mod_kernel,
        grid=(depth, n // tn),
        in_specs=[
            pl.BlockSpec((MOD_ROWS, d), lambda l, j: (0, 0)),
            pl.BlockSpec((1, d, tn), lambda l, j: (l, 0, j)),
            pl.BlockSpec((1, 1, tn), lambda l, j: (l, 0, j)),
        ],
        out_specs=pl.BlockSpec((1, MOD_ROWS, tn), lambda l, j: (l, 0, j)),
        out_shape=jax.ShapeDtypeStruct((depth, MOD_ROWS, n), F32),
        compiler_params=_cparams(("arbitrary", "arbitrary")),
        name="adaln_mod",
    )(cvec, ada_w, ada_b.reshape(depth, 1, n))


def _norm_mod(x, g, sh, sc):
    ms = jnp.mean(x * x, axis=-1, keepdims=True)
    y = x * lax.rsqrt(ms + EPS) * g
    return y * (1.0 + sc) + sh


def _proj_kernel(x_ref, g_ref, sh_ref, sc_ref, w_ref, b_ref, *o_refs, splits, scales):
    h = _norm_mod(x_ref[...], g_ref[...], sh_ref[0], sc_ref[0]).astype(BF16)
    off = 0
    for o_ref, n, s in zip(o_refs, splits, scales):
        y = jnp.dot(h, w_ref[:, off:off + n], preferred_element_type=F32) + b_ref[:, off:off + n]
        if s != 1.0:
            y = y * s
        o_ref[...] = y.astype(o_ref.dtype)
        off += n


def _proj(x2d, g, mod3, row_of_tile, sh_chunk, sc_chunk, w, b, splits, dtypes, scales, tm):
    n_tok, d = x2d.shape
    n_out = w.shape[1]
    assert sum(splits) == n_out and n_tok % tm == 0
    return pl.pallas_call(
        functools.partial(_proj_kernel, splits=tuple(splits), scales=tuple(scales)),
        grid=(n_tok // tm,),
        in_specs=[
            pl.BlockSpec((tm, d), lambda i: (i, 0)),
            pl.BlockSpec((1, d), lambda i: (0, 0)),
            pl.BlockSpec((1, 1, d), lambda i: (row_of_tile(i), 0, sh_chunk)),
            pl.BlockSpec((1, 1, d), lambda i: (row_of_tile(i), 0, sc_chunk)),
            pl.BlockSpec((d, n_out), lambda i: (0, 0)),
            pl.BlockSpec((1, n_out), lambda i: (0, 0)),
        ],
        out_specs=[pl.BlockSpec((tm, n), lambda i: (i, 0)) for n in splits],
        out_shape=[jax.ShapeDtypeStruct((n_tok, n), dt) for n, dt in zip(splits, dtypes)],
        compiler_params=_cparams(("parallel",)),
        name="norm_mod_proj",
    )(x2d, g.reshape(1, d), mod3, mod3, w, b.reshape(1, n_out))


def _resid_kernel(a_ref, w_ref, b_ref, res_ref, gate_ref, o_ref):
    y = jnp.dot(a_ref[...], w_ref[...], preferred_element_type=F32) + b_ref[...]
    o_ref[...] = res_ref[...] + gate_ref[0] * y


def _resid_proj(a, w, b, res, mod3, row_of_tile, gate_chunk, tm):
    n_tok, k = a.shape
    d = w.shape[1]
    return pl.pallas_call(
        _resid_kernel,
        grid=(n_tok // tm,),
        in_specs=[
            pl.BlockSpec((tm, k), lambda i: (i, 0)),
            pl.BlockSpec((k, d), lambda i: (0, 0)),
            pl.BlockSpec((1, d), lambda i: (0, 0)),
            pl.BlockSpec((tm, d), lambda i: (i, 0)),
            pl.BlockSpec((1, 1, d), lambda i: (row_of_tile(i), 0, gate_chunk)),
        ],
        out_specs=pl.BlockSpec((tm, d), lambda i: (i, 0)),
        out_shape=jax.ShapeDtypeStruct((n_tok, d), F32),
        compiler_params=_cparams(("parallel",)),
        name="proj_residual",
    )(a, w, b.reshape(1, d), res, mod3)


HEAD_GROUP = 4


def _head_group_attention(qg, keys, vals, biases):
    n_q, width = qg.shape
    dh = width // HEAD_GROUP
    head_of_lane = lax.broadcasted_iota(jnp.int32, (n_q, width), 1) // dh
    qs = jnp.concatenate([jnp.where(head_of_lane == h, qg, jnp.zeros_like(qg)) for h in range(HEAD_GROUP)],
                         axis=0)
    s_parts = []
    for kk, bb in zip(keys, biases):
        s = lax.dot_general(qs, kk, (((1,), (1,)), ((), ())), preferred_element_type=F32)
        s_parts.append(s if bb is None else s + bb)
    m = s_parts[0].max(axis=-1, keepdims=True)
    for s in s_parts[1:]:
        m = jnp.maximum(m, s.max(axis=-1, keepdims=True))
    p_parts = [jnp.exp2(s - m) for s in s_parts]
    l = p_parts[0].sum(axis=-1, keepdims=True)
    for p in p_parts[1:]:
        l = l + p.sum(axis=-1, keepdims=True)
    o = None
    for p, vv in zip(p_parts, vals):
        t = jnp.dot(p.astype(BF16), vv, preferred_element_type=F32)
        o = t if o is None else o + t
    o = o / l
    out = jnp.where(head_of_lane == 0, o[0:n_q], 0.0)
    for h in range(1, HEAD_GROUP):
        out = jnp.where(head_of_lane == h, o[h * n_q:(h + 1) * n_q], out)
    return out


NA_ROWS_PER_STEP = 4


def _na_kernel(q_ref, k_ref, v_ref, kc_ref, vc_ref, mb_ref, o_ref, *, rows, kh, rps):
    n_win = kh * GRID_W
    width = HEAD_GROUP * (q_ref.shape[2] // NA_HEADS)

    for j in range(rps):
        r = pl.program_id(1) * rps + j
        rs = jnp.clip(r - kh // 2, 0, rows - kh)
        k0 = pl.multiple_of(rs * GRID_W, GRID_W)
        q0 = j * GRID_W
        for g in range(NA_HEADS // HEAD_GROUP):
            c0 = g * width
            qg = q_ref[0, q0:q0 + GRID_W, c0:c0 + width]
            kw = k_ref[0, pl.ds(k0, n_win), c0:c0 + width]
            vw = v_ref[0, pl.ds(k0, n_win), c0:c0 + width]
            kc = kc_ref[0, :, c0:c0 + width]
            vc = vc_ref[0, :, c0:c0 + width]
            bias = mb_ref[r - rs, g * HEAD_GROUP:(g + 1) * HEAD_GROUP].reshape(HEAD_GROUP * GRID_W, n_win)
            o = _head_group_attention(qg, [kw, kc], [vw, vc], [bias, None])
            o_ref[0, q0:q0 + GRID_W, c0:c0 + width] = o.astype(o_ref.dtype)


def _window_bias(rpb, rows):
    kh = min(WIN_H, rows)
    cols = np.arange(GRID_W)
    col_start = np.clip(cols - WIN_W // 2, 0, GRID_W - WIN_W)
    ck = np.arange(GRID_W)
    in_win = (ck[None, :] >= col_start[:, None]) & (ck[None, :] < col_start[:, None] + WIN_W)
    dc = ck[None, :] - cols[:, None] + (WIN_W - 1)
    n_dc = 2 * WIN_W - 1
    sel = ((dc[None] == np.arange(n_dc)[:, None, None]) & in_win[None]).astype(np.float32)
    t = jnp.einsum("hrc,cqk->hrqk", rpb.astype(F32), jnp.asarray(sel), precision=HIGHEST)
    t = jnp.where(jnp.asarray(in_win)[None, None], t, NEG)
    per_delta = []
    for delta in range(kh):
        r0 = WIN_H - 1 - delta
        per_delta.append(jnp.transpose(t[:, r0:r0 + kh], (0, 2, 1, 3)).reshape(NA_HEADS, GRID_W, kh * GRID_W))
    return jnp.stack(per_delta) * LOG2_E


def _na_attention(q, k, v, kc, vc, mb):
    b, l, d = q.shape
    c = kc.shape[1]
    rows = l // GRID_W
    kh = mb.shape[0]

    rps = NA_ROWS_PER_STEP
    assert rows % rps == 0

    return pl.pallas_call(
        functools.partial(_na_kernel, rows=rows, kh=kh, rps=rps),
        grid=(b, rows // rps),
        in_specs=[
            pl.BlockSpec((1, rps * GRID_W, d), lambda bi, s: (bi, s, 0)),
            pl.BlockSpec((1, l, d), lambda bi, s: (bi, 0, 0)),
            pl.BlockSpec((1, l, d), lambda bi, s: (bi, 0, 0), pipeline_mode=pl.Buffered(1)),
            pl.BlockSpec((1, c, d), lambda bi, s: (bi, 0, 0)),
            pl.BlockSpec((1, c, d), lambda bi, s: (bi, 0, 0)),
            pl.BlockSpec(mb.shape, lambda bi, s: (0, 0, 0, 0), pipeline_mode=pl.Buffered(1)),
        ],
        out_specs=pl.BlockSpec((1, rps * GRID_W, d), lambda bi, s: (bi, s, 0)),
        out_shape=jax.ShapeDtypeStruct((b, l, d), BF16),
        compiler_params=_cparams(("parallel", "arbitrary")),
        name="na_attention",
    )(q, k, v, kc, vc, mb)


def _ctx_attn_kernel(q_ref, k_ref, v_ref, o_ref):
    width = HEAD_GROUP * (q_ref.shape[2] // NA_HEADS)

    def group(g, carry):
        c0 = pl.multiple_of(g * width, width)
        o = _head_group_attention(q_ref[0, :, pl.ds(c0, width)], [k_ref[0, :, pl.ds(c0, width)]],
                                  [v_ref[0, :, pl.ds(c0, width)]], [None])
        o_ref[0, :, pl.ds(c0, width)] = o.astype(o_ref.dtype)
        return carry

    lax.fori_loop(0, NA_HEADS // HEAD_GROUP, group, 0)


def _ctx_attention(qc, kc, vc):
    b, c, d = qc.shape
    spec = pl.BlockSpec((1, c, d), lambda bi: (bi, 0, 0))
    return pl.pallas_call(
        _ctx_attn_kernel,
        grid=(b,),
        in_specs=[spec, spec, spec],
        out_specs=spec,
        out_shape=jax.ShapeDtypeStruct((b, c, d), BF16),
        compiler_params=_cparams(("parallel",)),
        name="ctx_attention",
    )(qc, kc, vc)


GROUP_LANE0 = N_EXPERTS


def _src_specs(srcs, tm):
    d = srcs[0].shape[1]
    tiles = [s.shape[0] // tm for s in srcs]
    specs, first = [], 0
    for t in tiles:
        specs.append(pl.BlockSpec((tm, d), functools.partial(
            lambda i, first, t: (jnp.clip(i - first, 0, t - 1), 0), first=first, t=t)))
        first += t
    return specs, tiles


def _select_src(i, refs, tiles):
    x = refs[-1][...]
    first = sum(tiles[:-1])
    for ref, t in zip(refs[-2::-1], tiles[-2::-1]):
        x = jnp.where(i < first, ref[...], x)
        first -= t
    return x


TOK_ROWS = 8


def _store_token_major(ref, x):
    n, d = x.shape
    assert d == TOK_ROWS * LANES
    for j in range(TOK_ROWS):
        ref[pl.ds(j, n, stride=TOK_ROWS), :] = x[:, j * LANES:(j + 1) * LANES]


def _load_token_major(ref, n):
    return jnp.concatenate([ref[pl.ds(j, n, stride=TOK_ROWS), :] for j in range(TOK_ROWS)], axis=1)


def _router_kernel(*refs, tiles):
    n_src = len(tiles)
    x_refs = refs[:n_src]
    g_ref, sh_ref, sc_ref, wr_ref, br_ref, h_ref, oh_ref, wd_ref, cnt_ref = refs[n_src:]
    i = pl.program_id(0)
    h = _norm_mod(_select_src(i, x_refs, tiles), g_ref[...], sh_ref[0], sc_ref[0])
    _store_token_major(h_ref, h)
    h_hi = h.astype(BF16)
    h_lo = (h - h_hi.astype(F32)).astype(BF16)
    wr = wr_ref[...]
    w_hi = wr.astype(BF16)
    w_lo = (wr - w_hi.astype(F32)).astype(BF16)
    logits = (jnp.dot(h_hi, w_hi, preferred_element_type=F32)
              + jnp.dot(h_lo, w_hi, preferred_element_type=F32)
              + jnp.dot(h_hi, w_lo, preferred_element_type=F32)) + br_ref[...]
    tm = logits.shape[0]
    lane = lax.broadcasted_iota(jnp.int32, (tm, LANES), 1)
    big = jnp.int32(LANES)

    gmask = (lane >= GROUP_LANE0) & (lane < GROUP_LANE0 + N_GROUPS)
    gl = jnp.where(gmask, logits, NEG)
    gmax = gl.max(axis=-1, keepdims=True)
    gsel = jnp.where(gmask & (gl == gmax), lane, big).min(axis=-1, keepdims=True) - GROUP_LANE0
    g_w = 1.0 / jnp.where(gmask, jnp.exp(gl - gmax), 0.0).sum(axis=-1, keepdims=True)

    e0 = gsel * EXPERTS_PER_GROUP
    emask = (lane >= e0) & (lane < e0 + EXPERTS_PER_GROUP)
    el = jnp.where(emask, logits, NEG)
    v1 = el.max(axis=-1, keepdims=True)
    i1 = jnp.where(emask & (el == v1), lane, big).min(axis=-1, keepdims=True)
    el2 = jnp.where(lane == i1, NEG, el)
    v2 = el2.max(axis=-1, keepdims=True)
    i2 = jnp.where(emask & (lane != i1) & (el2 == v2), lane, big).min(axis=-1, keepdims=True)
    t = jnp.exp(v2 - v1)
    w1 = g_w / (1.0 + t)
    w2 = g_w * t / (1.0 + t)

    sel1 = lane == i1
    sel2 = lane == i2
    oh = jnp.where(sel1 | sel2, 1.0, 0.0)
    oh_ref[...] = oh.astype(oh_ref.dtype)
    wd_ref[...] = jnp.where(sel1, w1, jnp.where(sel2, w2, 0.0))

    @pl.when(i == 0)
    def _():
        cnt_ref[...] = jnp.zeros_like(cnt_ref)

    cnt_ref[...] += oh.sum(axis=0, keepdims=True)


def _router(srcs, g, mod3, row_of_tile, sh_chunk, sc_chunk, wr, br, tm):
    d = srcs[0].shape[1]
    src_specs, tiles = _src_specs(srcs, tm)
    n = sum(tiles) * tm
    tok = pl.BlockSpec((tm, LANES), lambda i: (i, 0))
    return pl.pallas_call(
        functools.partial(_router_kernel, tiles=tuple(tiles)),
        grid=(sum(tiles),),
        in_specs=src_specs + [
            pl.BlockSpec((1, d), lambda i: (0, 0)),
            pl.BlockSpec((1, 1, d), lambda i: (row_of_tile(i), 0, sh_chunk)),
            pl.BlockSpec((1, 1, d), lambda i: (row_of_tile(i), 0, sc_chunk)),
            pl.BlockSpec((d, LANES), lambda i: (0, 0)),
            pl.BlockSpec((1, LANES), lambda i: (0, 0)),
        ],
        out_specs=[pl.BlockSpec((tm * TOK_ROWS, LANES), lambda i: (i, 0)), tok, tok,
                   pl.BlockSpec((1, LANES), lambda i: (0, 0))],
        out_shape=[jax.ShapeDtypeStruct((n * TOK_ROWS, LANES), F32), jax.ShapeDtypeStruct((n, LANES), BF16),
                   jax.ShapeDtypeStruct((n, LANES), F32), jax.ShapeDtypeStruct((1, LANES), F32)],
        compiler_params=_cparams(("arbitrary",)),
        name="moe_router",
    )(*srcs, g.reshape(1, d), mod3, mod3, wr, br)


def _dest_kernel(start_ref, oh_ref, wd_ref, dest_ref, wsel_ref, carry_ref):
    i = pl.program_id(0)

    @pl.when(i == 0)
    def _():
        carry_ref[...] = jnp.zeros_like(carry_ref)

    oh = oh_ref[...]
    tm = oh.shape[0]
    row = lax.broadcasted_iota(jnp.int32, (tm, tm), 0)
    col = lax.broadcasted_iota(jnp.int32, (tm, tm), 1)
    tri = jnp.where(row > col, 1.0, 0.0).astype(BF16)
    base = jnp.dot(tri, oh, preferred_element_type=F32) + carry_ref[...] + start_ref[...]
    ohf = oh.astype(F32)
    lane = lax.broadcasted_iota(jnp.int32, (tm, LANES), 1)
    sel = ohf > 0.0
    ea = jnp.where(sel, lane, LANES).min(axis=-1, keepdims=True)
    eb = jnp.where(sel, lane, -1).max(axis=-1, keepdims=True)
    wd = wd_ref[...]
    pick = lambda e, val: jnp.where(lane == e, val, 0.0).sum(axis=-1, keepdims=True)
    dcols = jnp.where(lane == 0, pick(ea, base), jnp.where(lane == 1, pick(eb, base), 0.0))
    dest_ref[0] = dcols.T[0:2, :].astype(jnp.int32)
    two = lax.broadcasted_iota(jnp.int32, (tm, 2), 1) == 0
    wsel_ref[...] = jnp.where(two, pick(ea, wd), pick(eb, wd))
    carry_ref[...] += ohf.sum(axis=0, keepdims=True)


def _dest(start, oh, wd, tm):
    n = oh.shape[0]
    tok = pl.BlockSpec((tm, LANES), lambda i: (i, 0))
    two = pl.BlockSpec((tm, 2), lambda i: (i, 0))
    return pl.pallas_call(
        _dest_kernel,
        grid=(n // tm,),
        in_specs=[pl.BlockSpec((1, LANES), lambda i: (0, 0)), tok, tok],
        out_specs=[pl.BlockSpec((1, 2, tm), lambda i: (i, 0, 0)), two],
        out_shape=[jax.ShapeDtypeStruct((n // tm, 2, tm), jnp.int32), jax.ShapeDtypeStruct((n, 2), F32)],
        scratch_shapes=[pltpu.VMEM((1, LANES), F32)],
        compiler_params=_cparams(("arbitrary",)),
        name="moe_dest",
    )(start, oh, wd)


def _tok_rows(t):
    return pl.ds(t * TOK_ROWS, TOK_ROWS)


def _wait_tokens(buf, n, sem):
    whole = buf.at[pl.ds(0, n * TOK_ROWS), :]
    pltpu.make_async_copy(whole, whole, sem).wait()


def _scatter_kernel(dest_ref, h_ref, xs_ref, sem):
    tm = h_ref.shape[0] // TOK_ROWS
    for r in range(tm):
        for k in range(2):
            pltpu.make_async_copy(h_ref.at[_tok_rows(r), :], xs_ref.at[dest_ref[0, k, r]], sem).start(priority=k)
    for k in range(2):
        _wait_tokens(h_ref, tm, sem)


def _scatter_rows(dest, h, tm):
    n = h.shape[0] // TOK_ROWS
    return pl.pallas_call(
        _scatter_kernel,
        grid=(n // tm,),
        in_specs=[
            pl.BlockSpec((1, 2, tm), lambda i: (i, 0, 0), memory_space=pltpu.SMEM),
            pl.BlockSpec((tm * TOK_ROWS, LANES), lambda i: (i, 0)),
        ],
        out_specs=pl.BlockSpec(memory_space=pl.ANY),
        out_shape=jax.ShapeDtypeStruct((2 * n, TOK_ROWS, LANES), F32),
        scratch_shapes=[pltpu.SemaphoreType.DMA(())],
        compiler_params=_cparams(("arbitrary",)),
        name="moe_scatter",
    )(dest, h)


def _gmm_kernel(vb_ref, ve_ref, lo_ref, hi_ref, x_ref, w1_ref, w3_ref, w2_ref, o_ref, w1b, w3b, w2b, *, bm):
    v = pl.program_id(0)
    pv = jnp.maximum(v - 1, 0)

    @pl.when((v == 0) | (ve_ref[v] != ve_ref[pv]))
    def _():
        w1b[...] = w1_ref[0, 0].astype(BF16)
        w3b[...] = w3_ref[0, 0].astype(BF16)
        w2b[...] = w2_ref[0, 0].astype(BF16)

    lo = lo_ref[v]
    hi = hi_ref[v]

    @pl.when(hi > lo)
    def _():
        xb = _load_token_major(x_ref, bm).astype(BF16)
        a = jnp.dot(xb, w1b[...], preferred_element_type=F32)
        g = jnp.dot(xb, w3b[...], preferred_element_type=F32)
        hdn = (a * jax.nn.sigmoid(a) * g).astype(BF16)
        y = jnp.dot(hdn, w2b[...], preferred_element_type=F32)
        rows = lax.broadcasted_iota(jnp.int32, (bm, 1), 0)
        mine = (rows >= lo) & (rows < hi)
        first = (v == 0) | (vb_ref[v] != vb_ref[pv])

        @pl.when(first)
        def _():
            _store_token_major(o_ref, jnp.where(mine, y, 0.0))

        @pl.when(jnp.logical_not(first))
        def _():
            _store_token_major(o_ref, jnp.where(mine, y, _load_token_major(o_ref, bm)))


def _gmm(visits, xs, layer, w1, w3, w2, bm):
    d, f = w1.shape[2:]
    n_vis = visits[0].shape[0]
    blk = pl.BlockSpec((bm * TOK_ROWS, LANES), lambda v, vb, ve, lo, hi: (vb[v], 0))
    grid_spec = pltpu.PrefetchScalarGridSpec(
        num_scalar_prefetch=4,
        grid=(n_vis,),
        in_specs=[
            blk,
            pl.BlockSpec((1, 1, d, f), lambda v, vb, ve, lo, hi: (layer, ve[v], 0, 0)),
            pl.BlockSpec((1, 1, d, f), lambda v, vb, ve, lo, hi: (layer, ve[v], 0, 0)),
            pl.BlockSpec((1, 1, f, d), lambda v, vb, ve, lo, hi: (layer, ve[v], 0, 0)),
        ],
        out_specs=blk,
        scratch_shapes=[pltpu.VMEM((d, f), BF16), pltpu.VMEM((d, f), BF16), pltpu.VMEM((f, d), BF16)],
    )
    return pl.pallas_call(
        functools.partial(_gmm_kernel, bm=bm),
        grid_spec=grid_spec,
        out_shape=jax.ShapeDtypeStruct(xs.shape, F32),
        compiler_params=_cparams(("arbitrary",)),
        name="moe_experts",
    )(*visits, xs, w1, w3, w2)


def _visit_plan(counts, n_rows, bm):
    counts = counts.astype(jnp.int32)
    end = jnp.cumsum(counts)
    start = end - counts
    n_blocks = n_rows // bm
    n_vis = n_blocks + N_EXPERTS
    tiles = jnp.where(counts > 0, (end - 1) // bm - start // bm + 1, 0)
    vend = jnp.cumsum(tiles)
    vstart = vend - tiles
    v = jnp.arange(n_vis, dtype=jnp.int32)[:, None]
    owns = (vstart[None, :] <= v) & (v < vend[None, :])
    take = lambda a: jnp.sum(jnp.where(owns, a[None, :], 0), axis=1)
    valid = v[:, 0] < vend[-1]
    last_e = jnp.max(jnp.where(counts > 0, jnp.arange(N_EXPERTS, dtype=jnp.int32), 0))
    e = jnp.where(valid, take(jnp.arange(N_EXPERTS, dtype=jnp.int32)), last_e)
    blk = jnp.where(valid, take(start // bm - vstart) + v[:, 0], n_blocks - 1)
    lo = jnp.where(valid, jnp.clip(take(start) - blk * bm, 0, bm), 0)
    hi = jnp.where(valid, jnp.clip(take(end) - blk * bm, 0, bm), 0)
    return start, (blk.astype(jnp.int32), e.astype(jnp.int32), lo.astype(jnp.int32), hi.astype(jnp.int32))


def _combine_kernel(dest_ref, dest_next_ref, ys_ref, *refs, tiles, final_norm):
    n_src = len(tiles)
    x_refs = refs[:n_src]
    w_ref, gate_ref = refs[n_src:n_src + 2]
    rest = refs[n_src + 2:]
    fg_ref = rest[0] if final_norm else None
    o_refs = rest[-n_src - 2:-2]
    buf, sems = rest[-2:]
    i = pl.program_id(0)
    n_steps = pl.num_programs(0)
    tm = x_refs[0].shape[0]

    def gather(d_ref, slot):
        for r in range(tm):
            for k in range(2):
                pltpu.make_async_copy(ys_ref.at[d_ref[0, k, r]], buf.at[slot, k, _tok_rows(r), :],
                                      sems.at[slot]).start(priority=k)

    @pl.when(i == 0)
    def _():
        gather(dest_ref, 0)

    @pl.when(i + 1 < n_steps)
    def _():
        gather(dest_next_ref, (i + 1) % 2)

    slot = i % 2
    for k in range(2):
        _wait_tokens(buf.at[slot, k], tm, sems.at[slot])
    w = w_ref[...]
    moe = (w[:, 0:1] * _load_token_major(buf.at[slot, 0], tm)
           + w[:, 1:2] * _load_token_major(buf.at[slot, 1], tm))
    y = _select_src(i, x_refs, tiles) + gate_ref[0] * moe
    if final_norm:
        y = y * lax.rsqrt(jnp.mean(y * y, axis=-1, keepdims=True) + EPS) * fg_ref[...]

    first = 0
    for o_ref, t in zip(o_refs, tiles):
        @pl.when((i >= first) & (i < first + t))
        def _(o_ref=o_ref):
            o_ref[...] = y
        first += t


def _combine(dest, wsel, ys, srcs, mod3, row_of_tile, gate_chunk, final_g, tm):
    d = srcs[0].shape[1]
    src_specs, tiles = _src_specs(srcs, tm)
    final_norm = final_g is not None
    extra_specs = [pl.BlockSpec((1, d), lambda i: (0, 0))] if final_norm else []
    extra_args = [final_g.reshape(1, d)] if final_norm else []
    n_steps = sum(tiles)
    return pl.pallas_call(
        functools.partial(_combine_kernel, tiles=tuple(tiles), final_norm=final_norm),
        grid=(n_steps,),
        in_specs=[
            pl.BlockSpec((1, 2, tm), lambda i: (i, 0, 0), memory_space=pltpu.SMEM),
            pl.BlockSpec((1, 2, tm), lambda i: (jnp.minimum(i + 1, n_steps - 1), 0, 0), memory_space=pltpu.SMEM),
            pl.BlockSpec(memory_space=pl.ANY),
        ] + src_specs + [
            pl.BlockSpec((tm, 2), lambda i: (i, 0)),
            pl.BlockSpec((1, 1, d), lambda i: (row_of_tile(i), 0, gate_chunk)),
        ] + extra_specs,
        out_specs=list(src_specs),
        out_shape=[jax.ShapeDtypeStruct(s.shape, F32) for s in srcs],
        scratch_shapes=[pltpu.VMEM((2, 2, tm * TOK_ROWS, LANES), F32), pltpu.SemaphoreType.DMA((2,))],
        compiler_params=_cparams(("arbitrary",)),
        name="moe_combine",
    )(dest, dest, ys, *srcs, wsel, mod3, *extra_args)


def _router_weights(w_group, b_group, w_expert, b_expert):
    d = w_group.shape[0]
    we = jnp.transpose(w_expert, (1, 0, 2)).reshape(d, N_EXPERTS)
    wr = jnp.concatenate([we, w_group, jnp.zeros((d, LANES - N_EXPERTS - N_GROUPS), F32)], axis=1)
    br = jnp.concatenate([b_expert.reshape(N_EXPERTS), b_group,
                          jnp.zeros((LANES - N_EXPERTS - N_GROUPS,), F32)]).reshape(1, LANES)
    return wr, br


def _moe_layer(srcs, g, mod3, row_of_tile, chunks, router_w, layer, w1, w3, w2, final_g, tm, bm):
    sh_chunk, sc_chunk, gate_chunk = chunks
    wr, br = _router_weights(*router_w)
    h, oh, wd, cnt = _router(srcs, g, mod3, row_of_tile, sh_chunk, sc_chunk, wr, br, tm)
    n = oh.shape[0]
    start, visits = _visit_plan(cnt[0, :N_EXPERTS], 2 * n, bm)
    start_row = jnp.zeros((1, LANES), F32).at[0, :N_EXPERTS].set(start.astype(F32))
    dest, wsel = _dest(start_row, oh, wd, tm)
    xs = _scatter_rows(dest, h, tm)
    ys = _gmm(visits, xs.reshape(2 * n * TOK_ROWS, LANES), layer, w1, w3, w2, bm)
    return _combine(dest, wsel, ys.reshape(2 * n, TOK_ROWS, LANES), srcs, mod3, row_of_tile, gate_chunk,
                    final_g, tm)


def _gelu_tanh(x):
    return x * (0.5 * (1.0 + jnp.tanh(0.7978845608028654 * (x + 0.044715 * (x * x * x)))))


def _lru_in_kernel(x_ref, g_ref, sh_ref, sc_ref, w_ref, b_ref, *o_refs, with_y):
    nb, tt, d = x_ref.shape
    h = _norm_mod(x_ref[...], g_ref[...], sh_ref[...], sc_ref[...]).reshape(nb * tt, d).astype(BF16)
    lw = o_refs[-1].shape[2]
    if with_y:
        y = jnp.dot(h, w_ref[:, 0:lw], preferred_element_type=F32) + b_ref[:, 0:lw]
        o_refs[0][...] = y.reshape(nb, tt, lw).astype(o_refs[0].dtype)
    off = w_ref.shape[1] - lw
    u = jnp.dot(h, w_ref[:, off:off + lw], preferred_element_type=F32) + b_ref[:, off:off + lw]
    o_refs[-1][...] = jnp.swapaxes(u.reshape(nb, tt, lw), 0, 1)


def _lru_in_proj(x3, g, mod3, mod_rows, w, b, with_y, tt):
    nb, t, d = x3.shape
    lw = w.shape[1] // 2 if with_y else w.shape[1]
    r0, nr = mod_rows
    mod_spec = lambda chunk: pl.BlockSpec((nr, 1, d), lambda i: (r0 // nr, 0, chunk))
    out_specs = [pl.BlockSpec((tt, nb, lw), lambda i: (i, 0, 0))]
    out_shape = [jax.ShapeDtypeStruct((t, nb, lw), F32)]
    if with_y:
        out_specs.insert(0, pl.BlockSpec((nb, tt, lw), lambda i: (0, i, 0)))
        out_shape.insert(0, jax.ShapeDtypeStruct((nb, t, lw), BF16))
    return pl.pallas_call(
        functools.partial(_lru_in_kernel, with_y=with_y),
        grid=(t // tt,),
        in_specs=[
            pl.BlockSpec((nb, tt, d), lambda i: (0, i, 0)),
            pl.BlockSpec((1, d), lambda i: (0, 0)),
            mod_spec(0), mod_spec(1),
            pl.BlockSpec(w.shape, lambda i: (0, 0)),
            pl.BlockSpec((1, w.shape[1]), lambda i: (0, 0)),
        ],
        out_specs=out_specs,
        out_shape=out_shape,
        compiler_params=_cparams(("parallel",)),
        name="rglru_in_proj",
    )(x3, g.reshape(1, d), mod3, mod3, w, b.reshape(1, w.shape[1]))


def _lru_tm_kernel(u_ref, up_ref, un_ref, cw_ref, cb_ref, wg_ref, bg_ref, lam_ref, h0_ref, *rest,
                   tt, n_t, reverse, has_other):
    other_ref = rest[0] if has_other else None
    h_ref, hend_ref, a_s, b_s, carry = rest[-5:]
    i = pl.program_id(0)
    ti = (n_t - 1 - i) if reverse else i
    _, nb, w = u_ref.shape
    cb = w // LRU_BLOCKS
    left = CONV_W // 2
    right = CONV_W - 1 - left

    @pl.when(i == 0)
    def _():
        carry[...] = h0_ref[...]

    prev = jnp.where(ti > 0, up_ref[...], 0.0)
    nxt = jnp.where(ti < n_t - 1, un_ref[...], 0.0)
    for n in range(LRU_BLOCKS):
        c0 = n * cb
        ext = jnp.concatenate([prev[:, :, c0:c0 + cb], u_ref[:, :, c0:c0 + cb], nxt[:, :, c0:c0 + cb]], axis=0)
        uc = cb_ref[:, c0:c0 + cb] + cw_ref[0:1, c0:c0 + cb] * ext[0:tt]
        for kk in range(1, CONV_W):
            uc = uc + cw_ref[kk:kk + 1, c0:c0 + cb] * ext[kk:kk + tt]
        ub = uc.reshape(tt * nb, cb).astype(BF16)
        r = jax.nn.sigmoid(jnp.dot(ub, wg_ref[0, n], preferred_element_type=F32) + bg_ref[0, :, c0:c0 + cb])
        ig = jax.nn.sigmoid(jnp.dot(ub, wg_ref[1, n], preferred_element_type=F32) + bg_ref[1, :, c0:c0 + cb])
        lam = lam_ref[:, c0:c0 + cb]
        rate = (-LRU_C * LOG2_E) * jnp.log1p(jnp.exp(-lam))
        a = jnp.exp2(r * rate)
        z = 1.0 - a * a
        root = jnp.where(z > 0.0, z * lax.rsqrt(z), 0.0)
        b = root * (ig * uc.reshape(tt * nb, cb))
        a_s[:, :, c0:c0 + cb] = a.reshape(tt, nb, cb)
        b_s[:, :, c0:c0 + cb] = b.reshape(tt, nb, cb)

    def step(s, h):
        t = (tt - 1 - s) if reverse else s
        h = a_s[t] * h + b_s[t]
        h_ref[t] = (h + other_ref[t]) if has_other else h
        return h

    h_last = lax.fori_loop(0, tt, step, carry[...], unroll=8)
    carry[...] = h_last

    @pl.when(i == n_t - 1)
    def _():
        hend_ref[...] = h_last


def _lru_tm_scan(u, h0, conv_w, conv_b, wg, bg, lam, reverse, tt, other=None):
    t, nb, w = u.shape
    n_t = t // tt
    left = CONV_W // 2
    right = CONV_W - 1 - left
    assert tt % left == 0 and t % tt == 0
    tmap = (lambda i: n_t - 1 - i) if reverse else (lambda i: i)
    full = lambda a: pl.BlockSpec(a.shape, lambda i: (0,) * a.ndim)
    conv_b = conv_b.reshape(1, w)
    bg = bg.reshape(2, 1, w)
    lam = lam.reshape(1, w)
    tile = pl.BlockSpec((tt, nb, w), lambda i: (tmap(i), 0, 0))
    others = [] if other is None else [other]
    return pl.pallas_call(
        functools.partial(_lru_tm_kernel, tt=tt, n_t=n_t, reverse=reverse, has_other=other is not None),
        grid=(n_t,),
        in_specs=[
            tile,
            pl.BlockSpec((left, nb, w), lambda i: (jnp.maximum(tmap(i) * (tt // left) - 1, 0), 0, 0)),
            pl.BlockSpec((right, nb, w), lambda i: (jnp.minimum((tmap(i) + 1) * (tt // right), t // right - 1), 0, 0)),
            full(conv_w), full(conv_b), full(wg), full(bg), full(lam), full(h0),
        ] + [tile] * len(others),
        out_specs=[tile, full(h0)],
        out_shape=[jax.ShapeDtypeStruct((t, nb, w), F32), jax.ShapeDtypeStruct(h0.shape, F32)],
        scratch_shapes=[pltpu.VMEM((tt, nb, w), F32), pltpu.VMEM((tt, nb, w), F32), pltpu.VMEM(h0.shape, F32)],
        compiler_params=_cparams(("arbitrary",)),
        name="rglru_scan_rev" if reverse else "rglru_scan_fwd",
    )(u, u, u, conv_w, conv_b, wg, bg, lam, h0, *others)


def _lru_out_tm_kernel(y_ref, h_ref, w_ref, b_ref, res_ref, gate_ref, o_ref):
    nb, tt, lw = y_ref.shape
    y = jnp.swapaxes(y_ref[...].astype(F32), 0, 1)
    a = (_gelu_tanh(y) * h_ref[...]).reshape(tt * nb, lw).astype(BF16)
    z = jnp.dot(a, w_ref[...], preferred_element_type=F32) + b_ref[...]
    z = jnp.swapaxes(z.reshape(tt, nb, z.shape[1]), 0, 1)
    o_ref[...] = res_ref[...] + gate_ref[...] * z


def _lru_out_tm_proj(y, h, w, b, res3, mod3, gate_chunk, tt):
    nb, t, lw = y.shape
    d = w.shape[1]
    tm_spec = pl.BlockSpec((tt, nb, lw), lambda i: (i, 0, 0))
    return pl.pallas_call(
        _lru_out_tm_kernel,
        grid=(t // tt,),
        in_specs=[
            pl.BlockSpec((nb, tt, lw), lambda i: (0, i, 0)),
            tm_spec,
            pl.BlockSpec((lw, d), lambda i: (0, 0)),
            pl.BlockSpec((1, d), lambda i: (0, 0)),
            pl.BlockSpec((nb, tt, d), lambda i: (0, i, 0)),
            pl.BlockSpec((nb, 1, d), lambda i: (0, 0, gate_chunk)),
        ],
        out_specs=pl.BlockSpec((nb, tt, d), lambda i: (0, i, 0)),
        out_shape=jax.ShapeDtypeStruct((nb, t, d), F32),
        compiler_params=_cparams(("parallel",)),
        name="rglru_out_proj",
    )(y, h, w, b.reshape(1, d), res3, mod3)


def _lru_states(ul, uc, conv_w, conv_b, w_gate, b_gate, lam, tt):
    wg = w_gate.astype(BF16)
    _, nb, w = ul.shape
    zero = jnp.zeros((nb, w), F32)
    h = None
    for dirn in range(2):
        rev = dirn == 1
        args = (conv_w, conv_b, wg[dirn], b_gate[dirn], lam[dirn], rev, tt)
        _, h_end = _lru_tm_scan(uc, zero, *args)
        h, _ = _lru_tm_scan(ul, h_end, *args, other=h)
    return h


TM = 512
BM = 512
TT = 128


def kernel(x, c, ctx, c_ctx, ada_w, ada_b, norm_g, na_w_qkv, na_b_qkv, na_rpb, na_w_o, na_b_o, lru_w_in, lru_b_in, lru_conv_w, lru_conv_b, lru_w_gate, lru_b_gate, lru_lambda, lru_w_o, lru_b_o, moe_w_group, moe_b_group, moe_w_expert, moe_b_expert, moe_w1, moe_w3, moe_w2, final_g):
    b, l, d = x.shape
    n_ctx = ctx.shape[1]
    n_l, n_c = b * l, b * n_ctx
    assert l % TM == 0 and n_ctx <= TM and TM % n_ctx == 0 and b + 1 <= MOD_ROWS
    ctx_row = b
    tiles_per_batch = l // TM
    lat_row = lambda i: i // tiles_per_batch
    ctx_tile_row = lambda i: ctx_row
    nl_tiles = n_l // TM
    both_row = lambda i: jnp.where(i < nl_tiles, i // tiles_per_batch, ctx_row)

    cvec = jnp.concatenate([c, c_ctx[None], jnp.zeros((MOD_ROWS - b - 1, d), F32)], axis=0)
    mod = _modulation(cvec, ada_w, ada_b)
    mod3 = [mod[i].reshape(MOD_ROWS, 1, N_MOD * d) for i in range(mod.shape[0])]
    xl = x.reshape(n_l, d)
    xc = ctx.reshape(n_c, d)

    w_qkv = na_w_qkv[0].astype(BF16)
    qk_scale = (d // NA_HEADS) ** -0.5 * LOG2_E
    q, k, v = _proj(xl, norm_g[0, 0], mod3[0], lat_row, 0, 1, w_qkv, na_b_qkv[0],
                    (d, d, d), (BF16,) * 3, (qk_scale, 1.0, 1.0), TM)
    qc, kc, vc = _proj(xc, norm_g[0, 0], mod3[0], ctx_tile_row, 0, 1, w_qkv, na_b_qkv[0],
                       (d, d, d), (BF16,) * 3, (qk_scale, 1.0, 1.0), n_ctx)
    to3 = lambda a, s: a.reshape(b, s, d)
    mb = _window_bias(na_rpb[0], l // GRID_W)
    o_l = _na_attention(to3(q, l), to3(k, l), to3(v, l), to3(kc, n_ctx), to3(vc, n_ctx), mb)
    o_c = _ctx_attention(to3(qc, n_ctx), to3(kc, n_ctx), to3(vc, n_ctx))
    w_o = na_w_o[0].astype(BF16)
    xl = _resid_proj(o_l.reshape(n_l, d), w_o, na_b_o[0], xl, mod3[0], lat_row, 2, TM)
    xc = _resid_proj(o_c.reshape(n_c, d), w_o, na_b_o[0], xc, mod3[0], ctx_tile_row, 2, n_ctx)
    xl, xc = _moe_layer([xl, xc], norm_g[0, 1], mod3[0], both_row, (3, 4, 5),
                        (moe_w_group[0], moe_b_group[0], moe_w_expert[0], moe_b_expert[0]),
                        0, moe_w1, moe_w3, moe_w2, None, TM, BM)

    w_in = lru_w_in[0].astype(BF16)
    lw = w_in.shape[1] // 2
    xl3 = xl.reshape(b, l, d)
    y_l, u_l = _lru_in_proj(xl3, norm_g[1, 0], mod3[1], (0, b), w_in, lru_b_in[0], True, TT)
    (u_c,) = _lru_in_proj(xc.reshape(b, n_ctx, d), norm_g[1, 0], mod3[1], (ctx_row, 1), w_in[:, lw:],
                          lru_b_in[0, lw:], False, TT)
    h_sum = _lru_states(u_l, u_c, lru_conv_w[0], lru_conv_b[0], lru_w_gate[0], lru_b_gate[0],
                        lru_lambda[0], TT)
    xl = _lru_out_tm_proj(y_l, h_sum, lru_w_o[0].astype(BF16), lru_b_o[0], xl3, mod3[1], 2, TT)
    xl = xl.reshape(n_l, d)
    (out,) = _moe_layer([xl], norm_g[1, 1], mod3[1], lat_row, (3, 4, 5),
                        (moe_w_group[1], moe_b_group[1], moe_w_expert[1], moe_b_expert[1]),
                        1, moe_w1, moe_w3, moe_w2, final_g, TM, BM)
    return out.reshape(b, l, d)
```

```python
import functools

import jax
import jax.numpy as jnp
import numpy as np
from jax import lax
from jax.experimental import pallas as pl
from jax.experimental.pallas import tpu as pltpu

F32 = jnp.float32
BF16 = jnp.bfloat16

GRID_W = 64
N_MOD = 6
NA_HEADS = 16
WIN_H = 8
WIN_W = 16
LRU_BLOCKS = 4
CONV_W = 4
LRU_C = 8.0
N_GROUPS = 4
EXPERTS_PER_GROUP = 8
N_EXPERTS = N_GROUPS * EXPERTS_PER_GROUP
EPS = 1e-6

LANES = 128
MOD_ROWS = 16
NEG = -1e30
VMEM_LIMIT = 56 * 1024 * 1024
HIGHEST = lax.Precision.HIGHEST
LOG2_E = 1.4426950408889634


def _cparams(sem, vmem=VMEM_LIMIT):
    return pltpu.CompilerParams(dimension_semantics=sem, vmem_limit_bytes=vmem)


def _mod_kernel(c_ref, w_ref, b_ref, o_ref):
    c = c_ref[...]
    s = c * jax.nn.sigmoid(c)
    o_ref[0] = jnp.dot(s, w_ref[0], precision=HIGHEST, preferred_element_type=F32) + b_ref[0]


def _modulation(cvec, ada_w, ada_b):
    depth, d, n = ada_w.shape
    tn = 1536
    return pl.pallas_call(
        _mod_kernel,
        grid=(depth, n // tn),
        in_specs=[
            pl.BlockSpec((MOD_ROWS, d), lambda l, j: (0, 0)),
            pl.BlockSpec((1, d, tn), lambda l, j: (l, 0, j)),
            pl.BlockSpec((1, 1, tn), lambda l, j: (l, 0, j)),
        ],
        out_specs=pl.BlockSpec((1, MOD_ROWS, tn), lambda l, j: (l, 0, j)),
        out_shape=jax.ShapeDtypeStruct((depth, MOD_ROWS, n), F32),
        compiler_params=_cparams(("arbitrary", "arbitrary")),
        name="adaln_mod",
    )(cvec, ada_w, ada_b.reshape(depth, 1, n))


def _norm_mod(x, g, sh, sc):
    ms = jnp.mean(x * x, axis=-1, keepdims=True)
    y = x * lax.rsqrt(ms + EPS) * g
    return y * (1.0 + sc) + sh


def _proj_kernel(x_ref, g_ref, sh_ref, sc_ref, w_ref, b_ref, *o_refs, splits, scales):
    h = _norm_mod(x_ref[...], g_ref[...], sh_ref[0], sc_ref[0]).astype(BF16)
    off = 0
    for o_ref, n, s in zip(o_refs, splits, scales):
        y = jnp.dot(h, w_ref[:, off:off + n], preferred_element_type=F32) + b_ref[:, off:off + n]
        if s != 1.0:
            y = y * s
        o_ref[...] = y.astype(o_ref.dtype)
        off += n


def _proj(x2d, g, mod3, row_of_tile, sh_chunk, sc_chunk, w, b, splits, dtypes, scales, tm):
    n_tok, d = x2d.shape
    n_out = w.shape[1]
    assert sum(splits) == n_out and n_tok % tm == 0
    return pl.pallas_call(
        functools.partial(_proj_kernel, splits=tuple(splits), scales=tuple(scales)),
        grid=(n_tok // tm,),
        in_specs=[
            pl.BlockSpec((tm, d), lambda i: (i, 0)),
            pl.BlockSpec((1, d), lambda i: (0, 0)),
            pl.BlockSpec((1, 1, d), lambda i: (row_of_tile(i), 0, sh_chunk)),
            pl.BlockSpec((1, 1, d), lambda i: (row_of_tile(i), 0, sc_chunk)),
            pl.BlockSpec((d, n_out), lambda i: (0, 0)),
            pl.BlockSpec((1, n_out), lambda i: (0, 0)),
        ],
        out_specs=[pl.BlockSpec((tm, n), lambda i: (i, 0)) for n in splits],
        out_shape=[jax.ShapeDtypeStruct((n_tok, n), dt) for n, dt in zip(splits, dtypes)],
        compiler_params=_cparams(("parallel",)),
        name="norm_mod_proj",
    )(x2d, g.reshape(1, d), mod3, mod3, w, b.reshape(1, n_out))


def _resid_kernel(a_ref, w_ref, b_ref, res_ref, gate_ref, o_ref):
    y = jnp.dot(a_ref[...], w_ref[...], preferred_element_type=F32) + b_ref[...]
    o_ref[...] = res_ref[...] + gate_ref[0] * y


def _resid_proj(a, w, b, res, mod3, row_of_tile, gate_chunk, tm):
    n_tok, k = a.shape
    d = w.shape[1]
    return pl.pallas_call(
        _resid_kernel,
        grid=(n_tok // tm,),
        in_specs=[
            pl.BlockSpec((tm, k), lambda i: (i, 0)),
            pl.BlockSpec((k, d), lambda i: (0, 0)),
            pl.BlockSpec((1, d), lambda i: (0, 0)),
            pl.BlockSpec((tm, d), lambda i: (i, 0)),
            pl.BlockSpec((1, 1, d), lambda i: (row_of_tile(i), 0, gate_chunk)),
        ],
        out_specs=pl.BlockSpec((tm, d), lambda i: (i, 0)),
        out_shape=jax.ShapeDtypeStruct((n_tok, d), F32),
        compiler_params=_cparams(("parallel",)),
        name="proj_residual",
    )(a, w, b.reshape(1, d), res, mod3)


HEAD_GROUP = 4


def _head_group_attention(qg, keys, vals, biases):
    n_q, width = qg.shape
    dh = width // HEAD_GROUP
    head_of_lane = lax.broadcasted_iota(jnp.int32, (n_q, width), 1) // dh
    qs = jnp.concatenate([jnp.where(head_of_lane == h, qg, jnp.zeros_like(qg)) for h in range(HEAD_GROUP)],
                         axis=0)
    s_parts = []
    for kk, bb in zip(keys, biases):
        s = lax.dot_general(qs, kk, (((1,), (1,)), ((), ())), preferred_element_type=F32)
        s_parts.append(s if bb is None else s + bb)
    m = s_parts[0].max(axis=-1, keepdims=True)
    for s in s_parts[1:]:
        m = jnp.maximum(m, s.max(axis=-1, keepdims=True))
    p_parts = [jnp.exp2(s - m) for s in s_parts]
    l = p_parts[0].sum(axis=-1, keepdims=True)
    for p in p_parts[1:]:
        l = l + p.sum(axis=-1, keepdims=True)
    o = None
    for p, vv in zip(p_parts, vals):
        t = jnp.dot(p.astype(BF16), vv, preferred_element_type=F32)
        o = t if o is None else o + t
    o = o / l
    out = jnp.where(head_of_lane == 0, o[0:n_q], 0.0)
    for h in range(1, HEAD_GROUP):
        out = jnp.where(head_of_lane == h, o[h * n_q:(h + 1) * n_q], out)
    return out


NA_ROWS_PER_STEP = 4


def _na_kernel(q_ref, k_ref, v_ref, kc_ref, vc_ref, *rest, rows, kh):
    mb_refs, o_ref = rest[:-1], rest[-1]
    rps = len(mb_refs)
    n_win = kh * GRID_W
    width = HEAD_GROUP * (q_ref.shape[2] // NA_HEADS)

    for j, mb_ref in enumerate(mb_refs):
        r = pl.program_id(1) * rps + j
        rs = jnp.clip(r - kh // 2, 0, rows - kh)
        k0 = pl.multiple_of(rs * GRID_W, GRID_W)
        q0 = j * GRID_W
        for g in range(NA_HEADS // HEAD_GROUP):
            c0 = g * width
            qg = q_ref[0, q0:q0 + GRID_W, c0:c0 + width]
            kw = k_ref[0, pl.ds(k0, n_win), c0:c0 + width]
            vw = v_ref[0, pl.ds(k0, n_win), c0:c0 + width]
            kc = kc_ref[0, :, c0:c0 + width]
            vc = vc_ref[0, :, c0:c0 + width]
            bias = mb_ref[0, g * HEAD_GROUP:(g + 1) * HEAD_GROUP].reshape(HEAD_GROUP * GRID_W, n_win)
            o = _head_group_attention(qg, [kw, kc], [vw, vc], [bias, None])
            o_ref[0, q0:q0 + GRID_W, c0:c0 + width] = o.astype(o_ref.dtype)


def _window_bias(rpb, rows):
    kh = min(WIN_H, rows)
    cols = np.arange(GRID_W)
    col_start = np.clip(cols - WIN_W // 2, 0, GRID_W - WIN_W)
    ck = np.arange(GRID_W)
    in_win = (ck[None, :] >= col_start[:, None]) & (ck[None, :] < col_start[:, None] + WIN_W)
    dc = ck[None, :] - cols[:, None] + (WIN_W - 1)
    n_dc = 2 * WIN_W - 1
    sel = ((dc[None] == np.arange(n_dc)[:, None, None]) & in_win[None]).astype(np.float32)
    t = jnp.einsum("hrc,cqk->hrqk", rpb.astype(F32), jnp.asarray(sel), precision=HIGHEST)
    t = jnp.where(jnp.asarray(in_win)[None, None], t, NEG)
    per_delta = []
    for delta in range(kh):
        r0 = WIN_H - 1 - delta
        per_delta.append(jnp.transpose(t[:, r0:r0 + kh], (0, 2, 1, 3)).reshape(NA_HEADS, GRID_W, kh * GRID_W))
    return jnp.stack(per_delta) * LOG2_E


def _na_attention(q, k, v, kc, vc, mb):
    b, l, d = q.shape
    c = kc.shape[1]
    rows = l // GRID_W
    kh = mb.shape[0]

    rps = NA_ROWS_PER_STEP
    assert rows % rps == 0

    def bias_spec(j):
        def index(bi, s):
            r = s * rps + j
            return (r - jnp.clip(r - kh // 2, 0, rows - kh), 0, 0, 0)
        return pl.BlockSpec((1, NA_HEADS, GRID_W, kh * GRID_W), index)

    return pl.pallas_call(
        functools.partial(_na_kernel, rows=rows, kh=kh),
        grid=(b, rows // rps),
        in_specs=[
            pl.BlockSpec((1, rps * GRID_W, d), lambda bi, s: (bi, s, 0)),
            pl.BlockSpec((1, l, d), lambda bi, s: (bi, 0, 0)),
            pl.BlockSpec((1, l, d), lambda bi, s: (bi, 0, 0), pipeline_mode=pl.Buffered(1)),
            pl.BlockSpec((1, c, d), lambda bi, s: (bi, 0, 0)),
            pl.BlockSpec((1, c, d), lambda bi, s: (bi, 0, 0)),
        ] + [bias_spec(j) for j in range(rps)],
        out_specs=pl.BlockSpec((1, rps * GRID_W, d), lambda bi, s: (bi, s, 0)),
        out_shape=jax.ShapeDtypeStruct((b, l, d), BF16),
        compiler_params=_cparams(("parallel", "arbitrary")),
        name="na_attention",
    )(q, k, v, kc, vc, *([mb] * rps))


def _ctx_attn_kernel(q_ref, k_ref, v_ref, o_ref):
    width = HEAD_GROUP * (q_ref.shape[2] // NA_HEADS)

    def group(g, carry):
        c0 = pl.multiple_of(g * width, width)
        o = _head_group_attention(q_ref[0, :, pl.ds(c0, width)], [k_ref[0, :, pl.ds(c0, width)]],
                                  [v_ref[0, :, pl.ds(c0, width)]], [None])
        o_ref[0, :, pl.ds(c0, width)] = o.astype(o_ref.dtype)
        return carry

    lax.fori_loop(0, NA_HEADS // HEAD_GROUP, group, 0)


def _ctx_attention(qc, kc, vc):
    b, c, d = qc.shape
    spec = pl.BlockSpec((1, c, d), lambda bi: (bi, 0, 0))
    return pl.pallas_call(
        _ctx_attn_kernel,
        grid=(b,),
        in_specs=[spec, spec, spec],
        out_specs=spec,
        out_shape=jax.ShapeDtypeStruct((b, c, d), BF16),
        compiler_params=_cparams(("parallel",)),
        name="ctx_attention",
    )(qc, kc, vc)


GROUP_LANE0 = N_EXPERTS


def _src_specs(srcs, tm):
    d = srcs[0].shape[1]
    tiles = [s.shape[0] // tm for s in srcs]
    specs, first = [], 0
    for t in tiles:
        specs.append(pl.BlockSpec((tm, d), functools.partial(
            lambda i, first, t: (jnp.clip(i - first, 0, t - 1), 0), first=first, t=t)))
        first += t
    return specs, tiles


def _select_src(i, refs, tiles):
    x = refs[-1][...]
    first = sum(tiles[:-1])
    for ref, t in zip(refs[-2::-1], tiles[-2::-1]):
        x = jnp.where(i < first, ref[...], x)
        first -= t
    return x


TOK_ROWS = 8


def _store_token_major(ref, x):
    n, d = x.shape
    assert d == TOK_ROWS * LANES
    for j in range(TOK_ROWS):
        ref[pl.ds(j, n, stride=TOK_ROWS), :] = x[:, j * LANES:(j + 1) * LANES]


def _load_token_major(ref, n):
    return jnp.concatenate([ref[pl.ds(j, n, stride=TOK_ROWS), :] for j in range(TOK_ROWS)], axis=1)


def _router_kernel(*refs, tiles):
    n_src = len(tiles)
    x_refs = refs[:n_src]
    g_ref, sh_ref, sc_ref, wr_ref, br_ref, h_ref, oh_ref, wd_ref, cnt_ref = refs[n_src:]
    i = pl.program_id(0)
    h = _norm_mod(_select_src(i, x_refs, tiles), g_ref[...], sh_ref[0], sc_ref[0])
    _store_token_major(h_ref, h)
    h_hi = h.astype(BF16)
    h_lo = (h - h_hi.astype(F32)).astype(BF16)
    wr = wr_ref[...]
    w_hi = wr.astype(BF16)
    w_lo = (wr - w_hi.astype(F32)).astype(BF16)
    logits = (jnp.dot(h_hi, w_hi, preferred_element_type=F32)
              + jnp.dot(h_lo, w_hi, preferred_element_type=F32)
              + jnp.dot(h_hi, w_lo, preferred_element_type=F32)) + br_ref[...]
    tm = logits.shape[0]
    lane = lax.broadcasted_iota(jnp.int32, (tm, LANES), 1)
    big = jnp.int32(LANES)

    gmask = (lane >= GROUP_LANE0) & (lane < GROUP_LANE0 + N_GROUPS)
    gl = jnp.where(gmask, logits, NEG)
    gmax = gl.max(axis=-1, keepdims=True)
    gsel = jnp.where(gmask & (gl == gmax), lane, big).min(axis=-1, keepdims=True) - GROUP_LANE0
    g_w = 1.0 / jnp.where(gmask, jnp.exp(gl - gmax), 0.0).sum(axis=-1, keepdims=True)

    e0 = gsel * EXPERTS_PER_GROUP
    emask = (lane >= e0) & (lane < e0 + EXPERTS_PER_GROUP)
    el = jnp.where(emask, logits, NEG)
    v1 = el.max(axis=-1, keepdims=True)
    i1 = jnp.where(emask & (el == v1), lane, big).min(axis=-1, keepdims=True)
    el2 = jnp.where(lane == i1, NEG, el)
    v2 = el2.max(axis=-1, keepdims=True)
    i2 = jnp.where(emask & (lane != i1) & (el2 == v2), lane, big).min(axis=-1, keepdims=True)
    t = jnp.exp(v2 - v1)
    w1 = g_w / (1.0 + t)
    w2 = g_w * t / (1.0 + t)

    sel1 = lane == i1
    sel2 = lane == i2
    oh = jnp.where(sel1 | sel2, 1.0, 0.0)
    oh_ref[...] = oh.astype(oh_ref.dtype)
    wd_ref[...] = jnp.where(sel1, w1, jnp.where(sel2, w2, 0.0))

    @pl.when(i == 0)
    def _():
        cnt_ref[...] = jnp.zeros_like(cnt_ref)

    cnt_ref[...] += oh.sum(axis=0, keepdims=True)


def _router(srcs, g, mod3, row_of_tile, sh_chunk, sc_chunk, wr, br, tm):
    d = srcs[0].shape[1]
    src_specs, tiles = _src_specs(srcs, tm)
    n = sum(tiles) * tm
    tok = pl.BlockSpec((tm, LANES), lambda i: (i, 0))
    return pl.pallas_call(
        functools.partial(_router_kernel, tiles=tuple(tiles)),
        grid=(sum(tiles),),
        in_specs=src_specs + [
            pl.BlockSpec((1, d), lambda i: (0, 0)),
            pl.BlockSpec((1, 1, d), lambda i: (row_of_tile(i), 0, sh_chunk)),
            pl.BlockSpec((1, 1, d), lambda i: (row_of_tile(i), 0, sc_chunk)),
            pl.BlockSpec((d, LANES), lambda i: (0, 0)),
            pl.BlockSpec((1, LANES), lambda i: (0, 0)),
        ],
        out_specs=[pl.BlockSpec((tm * TOK_ROWS, LANES), lambda i: (i, 0)), tok, tok,
                   pl.BlockSpec((1, LANES), lambda i: (0, 0))],
        out_shape=[jax.ShapeDtypeStruct((n * TOK_ROWS, LANES), F32), jax.ShapeDtypeStruct((n, LANES), BF16),
                   jax.ShapeDtypeStruct((n, LANES), F32), jax.ShapeDtypeStruct((1, LANES), F32)],
        compiler_params=_cparams(("arbitrary",)),
        name="moe_router",
    )(*srcs, g.reshape(1, d), mod3, mod3, wr, br)


def _dest_kernel(start_ref, oh_ref, wd_ref, dest_ref, wsel_ref, carry_ref):
    i = pl.program_id(0)

    @pl.when(i == 0)
    def _():
        carry_ref[...] = jnp.zeros_like(carry_ref)

    oh = oh_ref[...]
    tm = oh.shape[0]
    row = lax.broadcasted_iota(jnp.int32, (tm, tm), 0)
    col = lax.broadcasted_iota(jnp.int32, (tm, tm), 1)
    tri = jnp.where(row > col, 1.0, 0.0).astype(BF16)
    base = jnp.dot(tri, oh, preferred_element_type=F32) + carry_ref[...] + start_ref[...]
    ohf = oh.astype(F32)
    lane = lax.broadcasted_iota(jnp.int32, (tm, LANES), 1)
    sel = ohf > 0.0
    ea = jnp.where(sel, lane, LANES).min(axis=-1, keepdims=True)
    eb = jnp.where(sel, lane, -1).max(axis=-1, keepdims=True)
    wd = wd_ref[...]
    pick = lambda e, val: jnp.where(lane == e, val, 0.0).sum(axis=-1, keepdims=True)
    dcols = jnp.where(lane == 0, pick(ea, base), jnp.where(lane == 1, pick(eb, base), 0.0))
    dest_ref[0] = dcols.T[0:2, :].astype(jnp.int32)
    two = lax.broadcasted_iota(jnp.int32, (tm, 2), 1) == 0
    wsel_ref[...] = jnp.where(two, pick(ea, wd), pick(eb, wd))
    carry_ref[...] += ohf.sum(axis=0, keepdims=True)


def _dest(start, oh, wd, tm):
    n = oh.shape[0]
    tok = pl.BlockSpec((tm, LANES), lambda i: (i, 0))
    two = pl.BlockSpec((tm, 2), lambda i: (i, 0))
    return pl.pallas_call(
        _dest_kernel,
        grid=(n // tm,),
        in_specs=[pl.BlockSpec((1, LANES), lambda i: (0, 0)), tok, tok],
        out_specs=[pl.BlockSpec((1, 2, tm), lambda i: (i, 0, 0)), two],
        out_shape=[jax.ShapeDtypeStruct((n // tm, 2, tm), jnp.int32), jax.ShapeDtypeStruct((n, 2), F32)],
        scratch_shapes=[pltpu.VMEM((1, LANES), F32)],
        compiler_params=_cparams(("arbitrary",)),
        name="moe_dest",
    )(start, oh, wd)


def _tok_rows(t):
    return pl.ds(t * TOK_ROWS, TOK_ROWS)


def _wait_tokens(buf, n, sem):
    whole = buf.at[pl.ds(0, n * TOK_ROWS), :]
    pltpu.make_async_copy(whole, whole, sem).wait()


def _scatter_kernel(dest_ref, h_ref, xs_ref, sem):
    tm = h_ref.shape[0] // TOK_ROWS
    for r in range(tm):
        for k in range(2):
            pltpu.make_async_copy(h_ref.at[_tok_rows(r), :], xs_ref.at[dest_ref[0, k, r]], sem).start(priority=k)
    for k in range(2):
        _wait_tokens(h_ref, tm, sem)


def _scatter_rows(dest, h, tm):
    n = h.shape[0] // TOK_ROWS
    return pl.pallas_call(
        _scatter_kernel,
        grid=(n // tm,),
        in_specs=[
            pl.BlockSpec((1, 2, tm), lambda i: (i, 0, 0), memory_space=pltpu.SMEM),
            pl.BlockSpec((tm * TOK_ROWS, LANES), lambda i: (i, 0)),
        ],
        out_specs=pl.BlockSpec(memory_space=pl.ANY),
        out_shape=jax.ShapeDtypeStruct((2 * n, TOK_ROWS, LANES), F32),
        scratch_shapes=[pltpu.SemaphoreType.DMA(())],
        compiler_params=_cparams(("arbitrary",)),
        name="moe_scatter",
    )(dest, h)


def _gmm_kernel(vb_ref, ve_ref, lo_ref, hi_ref, x_ref, w1_ref, w3_ref, w2_ref, o_ref, w1b, w3b, w2b, *, bm):
    v = pl.program_id(0)
    pv = jnp.maximum(v - 1, 0)

    @pl.when((v == 0) | (ve_ref[v] != ve_ref[pv]))
    def _():
        w1b[...] = w1_ref[0, 0].astype(BF16)
        w3b[...] = w3_ref[0, 0].astype(BF16)
        w2b[...] = w2_ref[0, 0].astype(BF16)

    lo = lo_ref[v]
    hi = hi_ref[v]

    @pl.when(hi > lo)
    def _():
        xb = _load_token_major(x_ref, bm).astype(BF16)
        a = jnp.dot(xb, w1b[...], preferred_element_type=F32)
        g = jnp.dot(xb, w3b[...], preferred_element_type=F32)
        hdn = (a * jax.nn.sigmoid(a) * g).astype(BF16)
        y = jnp.dot(hdn, w2b[...], preferred_element_type=F32)
        rows = lax.broadcasted_iota(jnp.int32, (bm, 1), 0)
        mine = (rows >= lo) & (rows < hi)
        first = (v == 0) | (vb_ref[v] != vb_ref[pv])

        @pl.when(first)
        def _():
            _store_token_major(o_ref, jnp.where(mine, y, 0.0))

        @pl.when(jnp.logical_not(first))
        def _():
            _store_token_major(o_ref, jnp.where(mine, y, _load_token_major(o_ref, bm)))


def _gmm(visits, xs, layer, w1, w3, w2, bm):
    d, f = w1.shape[2:]
    n_vis = visits[0].shape[0]
    blk = pl.BlockSpec((bm * TOK_ROWS, LANES), lambda v, vb, ve, lo, hi: (vb[v], 0))
    grid_spec = pltpu.PrefetchScalarGridSpec(
        num_scalar_prefetch=4,
        grid=(n_vis,),
        in_specs=[
            blk,
            pl.BlockSpec((1, 1, d, f), lambda v, vb, ve, lo, hi: (layer, ve[v], 0, 0)),
            pl.BlockSpec((1, 1, d, f), lambda v, vb, ve, lo, hi: (layer, ve[v], 0, 0)),
            pl.BlockSpec((1, 1, f, d), lambda v, vb, ve, lo, hi: (layer, ve[v], 0, 0)),
        ],
        out_specs=blk,
        scratch_shapes=[pltpu.VMEM((d, f), BF16), pltpu.VMEM((d, f), BF16), pltpu.VMEM((f, d), BF16)],
    )
    return pl.pallas_call(
        functools.partial(_gmm_kernel, bm=bm),
        grid_spec=grid_spec,
        out_shape=jax.ShapeDtypeStruct(xs.shape, F32),
        compiler_params=_cparams(("arbitrary",)),
        name="moe_experts",
    )(*visits, xs, w1, w3, w2)


def _visit_plan(counts, n_rows, bm):
    counts = counts.astype(jnp.int32)
    end = jnp.cumsum(counts)
    start = end - counts
    n_blocks = n_rows // bm
    n_vis = n_blocks + N_EXPERTS
    tiles = jnp.where(counts > 0, (end - 1) // bm - start // bm + 1, 0)
    vend = jnp.cumsum(tiles)
    vstart = vend - tiles
    v = jnp.arange(n_vis, dtype=jnp.int32)[:, None]
    owns = (vstart[None, :] <= v) & (v < vend[None, :])
    take = lambda a: jnp.sum(jnp.where(owns, a[None, :], 0), axis=1)
    valid = v[:, 0] < vend[-1]
    last_e = jnp.max(jnp.where(counts > 0, jnp.arange(N_EXPERTS, dtype=jnp.int32), 0))
    e = jnp.where(valid, take(jnp.arange(N_EXPERTS, dtype=jnp.int32)), last_e)
    blk = jnp.where(valid, take(start // bm - vstart) + v[:, 0], n_blocks - 1)
    lo = jnp.where(valid, jnp.clip(take(start) - blk * bm, 0, bm), 0)
    hi = jnp.where(valid, jnp.clip(take(end) - blk * bm, 0, bm), 0)
    return start, (blk.astype(jnp.int32), e.astype(jnp.int32), lo.astype(jnp.int32), hi.astype(jnp.int32))


def _combine_kernel(dest_ref, dest_next_ref, ys_ref, *refs, tiles, final_norm):
    n_src = len(tiles)
    x_refs = refs[:n_src]
    w_ref, gate_ref = refs[n_src:n_src + 2]
    rest = refs[n_src + 2:]
    fg_ref = rest[0] if final_norm else None
    o_refs = rest[-n_src - 2:-2]
    buf, sems = rest[-2:]
    i = pl.program_id(0)
    n_steps = pl.num_programs(0)
    tm = x_refs[0].shape[0]

    def gather(d_ref, slot):
        for r in range(tm):
            for k in range(2):
                pltpu.make_async_copy(ys_ref.at[d_ref[0, k, r]], buf.at[slot, k, _tok_rows(r), :],
                                      sems.at[slot]).start(priority=k)

    @pl.when(i == 0)
    def _():
        gather(dest_ref, 0)

    @pl.when(i + 1 < n_steps)
    def _():
        gather(dest_next_ref, (i + 1) % 2)

    slot = i % 2
    for k in range(2):
        _wait_tokens(buf.at[slot, k], tm, sems.at[slot])
    w = w_ref[...]
    moe = (w[:, 0:1] * _load_token_major(buf.at[slot, 0], tm)
           + w[:, 1:2] * _load_token_major(buf.at[slot, 1], tm))
    y = _select_src(i, x_refs, tiles) + gate_ref[0] * moe
    if final_norm:
        y = y * lax.rsqrt(jnp.mean(y * y, axis=-1, keepdims=True) + EPS) * fg_ref[...]

    first = 0
    for o_ref, t in zip(o_refs, tiles):
        @pl.when((i >= first) & (i < first + t))
        def _(o_ref=o_ref):
            o_ref[...] = y
        first += t


def _combine(dest, wsel, ys, srcs, mod3, row_of_tile, gate_chunk, final_g, tm):
    d = srcs[0].shape[1]
    src_specs, tiles = _src_specs(srcs, tm)
    final_norm = final_g is not None
    extra_specs = [pl.BlockSpec((1, d), lambda i: (0, 0))] if final_norm else []
    extra_args = [final_g.reshape(1, d)] if final_norm else []
    n_steps = sum(tiles)
    return pl.pallas_call(
        functools.partial(_combine_kernel, tiles=tuple(tiles), final_norm=final_norm),
        grid=(n_steps,),
        in_specs=[
            pl.BlockSpec((1, 2, tm), lambda i: (i, 0, 0), memory_space=pltpu.SMEM),
            pl.BlockSpec((1, 2, tm), lambda i: (jnp.minimum(i + 1, n_steps - 1), 0, 0), memory_space=pltpu.SMEM),
            pl.BlockSpec(memory_space=pl.ANY),
        ] + src_specs + [
            pl.BlockSpec((tm, 2), lambda i: (i, 0)),
            pl.BlockSpec((1, 1, d), lambda i: (row_of_tile(i), 0, gate_chunk)),
        ] + extra_specs,
        out_specs=list(src_specs),
        out_shape=[jax.ShapeDtypeStruct(s.shape, F32) for s in srcs],
        scratch_shapes=[pltpu.VMEM((2, 2, tm * TOK_ROWS, LANES), F32), pltpu.SemaphoreType.DMA((2,))],
        compiler_params=_cparams(("arbitrary",)),
        name="moe_combine",
    )(dest, dest, ys, *srcs, wsel, mod3, *extra_args)


def _router_weights(w_group, b_group, w_expert, b_expert):
    d = w_group.shape[0]
    we = jnp.transpose(w_expert, (1, 0, 2)).reshape(d, N_EXPERTS)
    wr = jnp.concatenate([we, w_group, jnp.zeros((d, LANES - N_EXPERTS - N_GROUPS), F32)], axis=1)
    br = jnp.concatenate([b_expert.reshape(N_EXPERTS), b_group,
                          jnp.zeros((LANES - N_EXPERTS - N_GROUPS,), F32)]).reshape(1, LANES)
    return wr, br


def _moe_layer(srcs, g, mod3, row_of_tile, chunks, router_w, layer, w1, w3, w2, final_g, tm, bm):
    sh_chunk, sc_chunk, gate_chunk = chunks
    wr, br = _router_weights(*router_w)
    h, oh, wd, cnt = _router(srcs, g, mod3, row_of_tile, sh_chunk, sc_chunk, wr, br, tm)
    n = oh.shape[0]
    start, visits = _visit_plan(cnt[0, :N_EXPERTS], 2 * n, bm)
    start_row = jnp.zeros((1, LANES), F32).at[0, :N_EXPERTS].set(start.astype(F32))
    dest, wsel = _dest(start_row, oh, wd, tm)
    xs = _scatter_rows(dest, h, tm)
    ys = _gmm(visits, xs.reshape(2 * n * TOK_ROWS, LANES), layer, w1, w3, w2, bm)
    return _combine(dest, wsel, ys.reshape(2 * n, TOK_ROWS, LANES), srcs, mod3, row_of_tile, gate_chunk,
                    final_g, tm)


def _gelu_tanh(x):
    return x * (0.5 * (1.0 + jnp.tanh(0.7978845608028654 * (x + 0.044715 * (x * x * x)))))


def _lru_in_kernel(x_ref, g_ref, sh_ref, sc_ref, w_ref, b_ref, *o_refs, with_y):
    nb, tt, d = x_ref.shape
    h = _norm_mod(x_ref[...], g_ref[...], sh_ref[...], sc_ref[...]).reshape(nb * tt, d).astype(BF16)
    lw = o_refs[-1].shape[2]
    if with_y:
        y = jnp.dot(h, w_ref[:, 0:lw], preferred_element_type=F32) + b_ref[:, 0:lw]
        o_refs[0][...] = y.reshape(nb, tt, lw).astype(o_refs[0].dtype)
    off = w_ref.shape[1] - lw
    u = jnp.dot(h, w_ref[:, off:off + lw], preferred_element_type=F32) + b_ref[:, off:off + lw]
    o_refs[-1][...] = jnp.swapaxes(u.reshape(nb, tt, lw), 0, 1)


def _lru_in_proj(x3, g, mod3, mod_rows, w, b, with_y, tt):
    nb, t, d = x3.shape
    lw = w.shape[1] // 2 if with_y else w.shape[1]
    r0, nr = mod_rows
    mod_spec = lambda chunk: pl.BlockSpec((nr, 1, d), lambda i: (r0 // nr, 0, chunk))
    out_specs = [pl.BlockSpec((tt, nb, lw), lambda i: (i, 0, 0))]
    out_shape = [jax.ShapeDtypeStruct((t, nb, lw), F32)]
    if with_y:
        out_specs.insert(0, pl.BlockSpec((nb, tt, lw), lambda i: (0, i, 0)))
        out_shape.insert(0, jax.ShapeDtypeStruct((nb, t, lw), BF16))
    return pl.pallas_call(
        functools.partial(_lru_in_kernel, with_y=with_y),
        grid=(t // tt,),
        in_specs=[
            pl.BlockSpec((nb, tt, d), lambda i: (0, i, 0)),
            pl.BlockSpec((1, d), lambda i: (0, 0)),
            mod_spec(0), mod_spec(1),
            pl.BlockSpec(w.shape, lambda i: (0, 0)),
            pl.BlockSpec((1, w.shape[1]), lambda i: (0, 0)),
        ],
        out_specs=out_specs,
        out_shape=out_shape,
        compiler_params=_cparams(("parallel",)),
        name="rglru_in_proj",
    )(x3, g.reshape(1, d), mod3, mod3, w, b.reshape(1, w.shape[1]))


def _lru_tm_kernel(u_ref, up_ref, un_ref, cw_ref, cb_ref, wg_ref, bg_ref, lam_ref, h0_ref, *rest,
                   tt, n_t, reverse, has_other):
    other_ref = rest[0] if has_other else None
    h_ref, hend_ref, a_s, b_s, carry = rest[-5:]
    i = pl.program_id(0)
    ti = (n_t - 1 - i) if reverse else i
    _, nb, w = u_ref.shape
    cb = w // LRU_BLOCKS
    left = CONV_W // 2
    right = CONV_W - 1 - left

    @pl.when(i == 0)
    def _():
        carry[...] = h0_ref[...]

    prev = jnp.where(ti > 0, up_ref[...], 0.0)
    nxt = jnp.where(ti < n_t - 1, un_ref[...], 0.0)
    for n in range(LRU_BLOCKS):
        c0 = n * cb
        ext = jnp.concatenate([prev[:, :, c0:c0 + cb], u_ref[:, :, c0:c0 + cb], nxt[:, :, c0:c0 + cb]], axis=0)
        uc = cb_ref[:, c0:c0 + cb] + cw_ref[0:1, c0:c0 + cb] * ext[0:tt]
        for kk in range(1, CONV_W):
            uc = uc + cw_ref[kk:kk + 1, c0:c0 + cb] * ext[kk:kk + tt]
        ub = uc.reshape(tt * nb, cb).astype(BF16)
        r = jax.nn.sigmoid(jnp.dot(ub, wg_ref[0, n], preferred_element_type=F32) + bg_ref[0, :, c0:c0 + cb])
        ig = jax.nn.sigmoid(jnp.dot(ub, wg_ref[1, n], preferred_element_type=F32) + bg_ref[1, :, c0:c0 + cb])
        lam = lam_ref[:, c0:c0 + cb]
        rate = (-LRU_C * LOG2_E) * jnp.log1p(jnp.exp(-lam))
        a = jnp.exp2(r * rate)
        z = 1.0 - a * a
        root = jnp.where(z > 0.0, z * lax.rsqrt(z), 0.0)
        b = root * (ig * uc.reshape(tt * nb, cb))
        a_s[:, :, c0:c0 + cb] = a.reshape(tt, nb, cb)
        b_s[:, :, c0:c0 + cb] = b.reshape(tt, nb, cb)

    def step(s, h):
        t = (tt - 1 - s) if reverse else s
        h = a_s[t] * h + b_s[t]
        h_ref[t] = (h + other_ref[t]) if has_other else h
        return h

    h_last = lax.fori_loop(0, tt, step, carry[...], unroll=8)
    carry[...] = h_last

    @pl.when(i == n_t - 1)
    def _():
        hend_ref[...] = h_last


def _lru_tm_scan(u, h0, conv_w, conv_b, wg, bg, lam, reverse, tt, other=None):
    t, nb, w = u.shape
    n_t = t // tt
    left = CONV_W // 2
    right = CONV_W - 1 - left
    assert tt % left == 0 and t % tt == 0
    tmap = (lambda i: n_t - 1 - i) if reverse else (lambda i: i)
    full = lambda a: pl.BlockSpec(a.shape, lambda i: (0,) * a.ndim)
    conv_b = conv_b.reshape(1, w)
    bg = bg.reshape(2, 1, w)
    lam = lam.reshape(1, w)
    tile = pl.BlockSpec((tt, nb, w), lambda i: (tmap(i), 0, 0))
    others = [] if other is None else [other]
    return pl.pallas_call(
        functools.partial(_lru_tm_kernel, tt=tt, n_t=n_t, reverse=reverse, has_other=other is not None),
        grid=(n_t,),
        in_specs=[
            tile,
            pl.BlockSpec((left, nb, w), lambda i: (jnp.maximum(tmap(i) * (tt // left) - 1, 0), 0, 0)),
            pl.BlockSpec((right, nb, w), lambda i: (jnp.minimum((tmap(i) + 1) * (tt // right), t // right - 1), 0, 0)),
            full(conv_w), full(conv_b), full(wg), full(bg), full(lam), full(h0),
        ] + [tile] * len(others),
        out_specs=[tile, full(h0)],
        out_shape=[jax.ShapeDtypeStruct((t, nb, w), F32), jax.ShapeDtypeStruct(h0.shape, F32)],
        scratch_shapes=[pltpu.VMEM((tt, nb, w), F32), pltpu.VMEM((tt, nb, w), F32), pltpu.VMEM(h0.shape, F32)],
        compiler_params=_cparams(("arbitrary",)),
        name="rglru_scan_rev" if reverse else "rglru_scan_fwd",
    )(u, u, u, conv_w, conv_b, wg, bg, lam, h0, *others)


def _lru_out_tm_kernel(y_ref, h_ref, w_ref, b_ref, res_ref, gate_ref, o_ref):
    nb, tt, lw = y_ref.shape
    y = jnp.swapaxes(y_ref[...].astype(F32), 0, 1)
    a = (_gelu_tanh(y) * h_ref[...]).reshape(tt * nb, lw).astype(BF16)
    z = jnp.dot(a, w_ref[...], preferred_element_type=F32) + b_ref[...]
    z = jnp.swapaxes(z.reshape(tt, nb, z.shape[1]), 0, 1)
    o_ref[...] = res_ref[...] + gate_ref[...] * z


def _lru_out_tm_proj(y, h, w, b, res3, mod3, gate_chunk, tt):
    nb, t, lw = y.shape
    d = w.shape[1]
    tm_spec = pl.BlockSpec((tt, nb, lw), lambda i: (i, 0, 0))
    return pl.pallas_call(
        _lru_out_tm_kernel,
        grid=(t // tt,),
        in_specs=[
            pl.BlockSpec((nb, tt, lw), lambda i: (0, i, 0)),
            tm_spec,
            pl.BlockSpec((lw, d), lambda i: (0, 0)),
            pl.BlockSpec((1, d), lambda i: (0, 0)),
            pl.BlockSpec((nb, tt, d), lambda i: (0, i, 0)),
            pl.BlockSpec((nb, 1, d), lambda i: (0, 0, gate_chunk)),
        ],
        out_specs=pl.BlockSpec((nb, tt, d), lambda i: (0, i, 0)),
        out_shape=jax.ShapeDtypeStruct((nb, t, d), F32),
        compiler_params=_cparams(("parallel",)),
        name="rglru_out_proj",
    )(y, h, w, b.reshape(1, d), res3, mod3)


def _lru_states(ul, uc, conv_w, conv_b, w_gate, b_gate, lam, tt):
    wg = w_gate.astype(BF16)
    _, nb, w = ul.shape
    zero = jnp.zeros((nb, w), F32)
    h = None
    for dirn in range(2):
        rev = dirn == 1
        args = (conv_w, conv_b, wg[dirn], b_gate[dirn], lam[dirn], rev, tt)
        _, h_end = _lru_tm_scan(uc, zero, *args)
        h, _ = _lru_tm_scan(ul, h_end, *args, other=h)
    return h


TM = 512
TD = 1024
BM = 512
TT = 128


def kernel(x, c, ctx, c_ctx, ada_w, ada_b, norm_g, na_w_qkv, na_b_qkv, na_rpb, na_w_o, na_b_o, lru_w_in, lru_b_in, lru_conv_w, lru_conv_b, lru_w_gate, lru_b_gate, lru_lambda, lru_w_o, lru_b_o, moe_w_group, moe_b_group, moe_w_expert, moe_b_expert, moe_w1, moe_w3, moe_w2, final_g):
    b, l, d = x.shape
    n_ctx = ctx.shape[1]
    n_l, n_c = b * l, b * n_ctx
    assert l % TM == 0 and l % TD == 0 and n_ctx <= TM and TM % n_ctx == 0 and b + 1 <= MOD_ROWS
    ctx_row = b
    tiles_per_batch = l // TM
    lat_row = lambda i: i // tiles_per_batch
    dense_row = lambda i: i // (l // TD)
    ctx_tile_row = lambda i: ctx_row
    nl_tiles = n_l // TM
    both_row = lambda i: jnp.where(i < nl_tiles, i // tiles_per_batch, ctx_row)

    cvec = jnp.concatenate([c, c_ctx[None], jnp.zeros((MOD_ROWS - b - 1, d), F32)], axis=0)
    mod = _modulation(cvec, ada_w, ada_b)
    mod3 = [mod[i].reshape(MOD_ROWS, 1, N_MOD * d) for i in range(mod.shape[0])]
    xl = x.reshape(n_l, d)
    xc = ctx.reshape(n_c, d)

    w_qkv = na_w_qkv[0].astype(BF16)
    qk_scale = (d // NA_HEADS) ** -0.5 * LOG2_E
    q, k, v = _proj(xl, norm_g[0, 0], mod3[0], dense_row, 0, 1, w_qkv, na_b_qkv[0],
                    (d, d, d), (BF16,) * 3, (qk_scale, 1.0, 1.0), TD)
    qc, kc, vc = _proj(xc, norm_g[0, 0], mod3[0], ctx_tile_row, 0, 1, w_qkv, na_b_qkv[0],
                       (d, d, d), (BF16,) * 3, (qk_scale, 1.0, 1.0), n_ctx)
    to3 = lambda a, s: a.reshape(b, s, d)
    mb = _window_bias(na_rpb[0], l // GRID_W)
    o_l = _na_attention(to3(q, l), to3(k, l), to3(v, l), to3(kc, n_ctx), to3(vc, n_ctx), mb)
    o_c = _ctx_attention(to3(qc, n_ctx), to3(kc, n_ctx), to3(vc, n_ctx))
    w_o = na_w_o[0].astype(BF16)
    xl = _resid_proj(o_l.reshape(n_l, d), w_o, na_b_o[0], xl, mod3[0], dense_row, 2, TD)
    xc = _resid_proj(o_c.reshape(n_c, d), w_o, na_b_o[0], xc, mod3[0], ctx_tile_row, 2, n_ctx)
    xl, xc = _moe_layer([xl, xc], norm_g[0, 1], mod3[0], both_row, (3, 4, 5),
                        (moe_w_group[0], moe_b_group[0], moe_w_expert[0], moe_b_expert[0]),
                        0, moe_w1, moe_w3, moe_w2, None, TM, BM)

    w_in = lru_w_in[0].astype(BF16)
    lw = w_in.shape[1] // 2
    xl3 = xl.reshape(b, l, d)
    y_l, u_l = _lru_in_proj(xl3, norm_g[1, 0], mod3[1], (0, b), w_in, lru_b_in[0], True, TT)
    (u_c,) = _lru_in_proj(xc.reshape(b, n_ctx, d), norm_g[1, 0], mod3[1], (ctx_row, 1), w_in[:, lw:],
                          lru_b_in[0, lw:], False, TT)
    h_sum = _lru_states(u_l, u_c, lru_conv_w[0], lru_conv_b[0], lru_w_gate[0], lru_b_gate[0],
                        lru_lambda[0], TT)
    xl = _lru_out_tm_proj(y_l, h_sum, lru_w_o[0].astype(BF16), lru_b_o[0], xl3, mod3[1], 2, TT)
    xl = xl.reshape(n_l, d)
    (out,) = _moe_layer([xl], norm_g[1, 1], mod3[1], lat_row, (3, 4, 5),
                        (moe_w_group[1], moe_b_group[1], moe_w_expert[1], moe_b_expert[1]),
                        1, moe_w1, moe_w3, moe_w2, final_g, TM, BM)
    return out.reshape(b, l, d)
```

```python
import functools

import jax
import jax.numpy as jnp
import numpy as np
from jax import lax
from jax.experimental import pallas as pl
from jax.experimental.pallas import tpu as pltpu

F32 = jnp.float32
BF16 = jnp.bfloat16

GRID_W = 64
N_MOD = 6
NA_HEADS = 16
WIN_H = 8
WIN_W = 16
LRU_BLOCKS = 4
CONV_W = 4
LRU_C = 8.0
N_GROUPS = 4
EXPERTS_PER_GROUP = 8
N_EXPERTS = N_GROUPS * EXPERTS_PER_GROUP
EPS = 1e-6

LANES = 128
MOD_ROWS = 16
NEG = -1e30
VMEM_LIMIT = 56 * 1024 * 1024
HIGHEST = lax.Precision.HIGHEST
LOG2_E = 1.4426950408889634


def _cparams(sem, vmem=VMEM_LIMIT):
    return pltpu.CompilerParams(dimension_semantics=sem, vmem_limit_bytes=vmem)


def _mod_kernel(c_ref, w_ref, b_ref, o_ref):
    c = c_ref[...]
    s = c * jax.nn.sigmoid(c)
    o_ref[0] = jnp.dot(s, w_ref[0], precision=HIGHEST, preferred_element_type=F32) + b_ref[0]


def _modulation(cvec, ada_w, ada_b):
    depth, d, n = ada_w.shape
    tn = 1536
    return pl.pallas_call(
        _mod_kernel,
        grid=(depth, n // tn),
        in_specs=[
            pl.BlockSpec((MOD_ROWS, d), lambda l, j: (0, 0)),
            pl.BlockSpec((1, d, tn), lambda l, j: (l, 0, j)),
            pl.BlockSpec((1, 1, tn), lambda l, j: (l, 0, j)),
        ],
        out_specs=pl.BlockSpec((1, MOD_ROWS, tn), lambda l, j: (l, 0, j)),
        out_shape=jax.ShapeDtypeStruct((depth, MOD_ROWS, n), F32),
        compiler_params=_cparams(("arbitrary", "arbitrary")),
        name="adaln_mod",
    )(cvec, ada_w, ada_b.reshape(depth, 1, n))


def _norm_mod(x, g, sh, sc):
    ms = jnp.mean(x * x, axis=-1, keepdims=True)
    y = x * lax.rsqrt(ms + EPS) * g
    return y * (1.0 + sc) + sh


def _proj_kernel(x_ref, g_ref, sh_ref, sc_ref, w_ref, b_ref, *o_refs, splits, scales):
    h = _norm_mod(x_ref[...], g_ref[...], sh_ref[0], sc_ref[0]).astype(BF16)
    off = 0
    for o_ref, n, s in zip(o_refs, splits, scales):
        y = jnp.dot(h, w_ref[:, off:off + n], preferred_element_type=F32) + b_ref[:, off:off + n]
        if s != 1.0:
            y = y * s
        o_ref[...] = y.astype(o_ref.dtype)
        off += n


def _proj(x2d, g, mod3, row_of_tile, sh_chunk, sc_chunk, w, b, splits, dtypes, scales, tm):
    n_tok, d = x2d.shape
    n_out = w.shape[1]
    assert sum(splits) == n_out and n_tok % tm == 0
    return pl.pallas_call(
        functools.partial(_proj_kernel, splits=tuple(splits), scales=tuple(scales)),
        grid=(n_tok // tm,),
        in_specs=[
            pl.BlockSpec((tm, d), lambda i: (i, 0)),
            pl.BlockSpec((1, d), lambda i: (0, 0)),
            pl.BlockSpec((1, 1, d), lambda i: (row_of_tile(i), 0, sh_chunk)),
            pl.BlockSpec((1, 1, d), lambda i: (row_of_tile(i), 0, sc_chunk)),
            pl.BlockSpec((d, n_out), lambda i: (0, 0)),
            pl.BlockSpec((1, n_out), lambda i: (0, 0)),
        ],
        out_specs=[pl.BlockSpec((tm, n), lambda i: (i, 0)) for n in splits],
        out_shape=[jax.ShapeDtypeStruct((n_tok, n), dt) for n, dt in zip(splits, dtypes)],
        compiler_params=_cparams(("parallel",)),
        name="norm_mod_proj",
    )(x2d, g.reshape(1, d), mod3, mod3, w, b.reshape(1, n_out))


def _resid_kernel(a_ref, w_ref, b_ref, res_ref, gate_ref, o_ref):
    y = jnp.dot(a_ref[...], w_ref[...], preferred_element_type=F32) + b_ref[...]
    o_ref[...] = res_ref[...] + gate_ref[0] * y


def _resid_proj(a, w, b, res, mod3, row_of_tile, gate_chunk, tm):
    n_tok, k = a.shape
    d = w.shape[1]
    return pl.pallas_call(
        _resid_kernel,
        grid=(n_tok // tm,),
        in_specs=[
            pl.BlockSpec((tm, k), lambda i: (i, 0)),
            pl.BlockSpec((k, d), lambda i: (0, 0)),
            pl.BlockSpec((1, d), lambda i: (0, 0)),
            pl.BlockSpec((tm, d), lambda i: (i, 0)),
            pl.BlockSpec((1, 1, d), lambda i: (row_of_tile(i), 0, gate_chunk)),
        ],
        out_specs=pl.BlockSpec((tm, d), lambda i: (i, 0)),
        out_shape=jax.ShapeDtypeStruct((n_tok, d), F32),
        compiler_params=_cparams(("parallel",)),
        name="proj_residual",
    )(a, w, b.reshape(1, d), res, mod3)


HEAD_GROUP = 4


def _head_group_attention(qg, keys, vals, biases):
    n_q, width = qg.shape
    dh = width // HEAD_GROUP
    head_of_lane = lax.broadcasted_iota(jnp.int32, (n_q, width), 1) // dh
    qs = jnp.concatenate([jnp.where(head_of_lane == h, qg, jnp.zeros_like(qg)) for h in range(HEAD_GROUP)],
                         axis=0)
    s_parts = []
    for kk, bb in zip(keys, biases):
        s = lax.dot_general(qs, kk, (((1,), (1,)), ((), ())), preferred_element_type=F32)
        s_parts.append(s if bb is None else s + bb)
    m = s_parts[0].max(axis=-1, keepdims=True)
    for s in s_parts[1:]:
        m = jnp.maximum(m, s.max(axis=-1, keepdims=True))
    p_parts = [jnp.exp2(s - m) for s in s_parts]
    l = p_parts[0].sum(axis=-1, keepdims=True)
    for p in p_parts[1:]:
        l = l + p.sum(axis=-1, keepdims=True)
    o = None
    for p, vv in zip(p_parts, vals):
        t = jnp.dot(p.astype(BF16), vv, preferred_element_type=F32)
        o = t if o is None else o + t
    o = o / l
    out = jnp.where(head_of_lane == 0, o[0:n_q], 0.0)
    for h in range(1, HEAD_GROUP):
        out = jnp.where(head_of_lane == h, o[h * n_q:(h + 1) * n_q], out)
    return out


NA_ROWS_PER_STEP = 4


def _na_kernel(q_ref, k_ref, v_ref, kc_ref, vc_ref, *rest, rows, kh):
    mb_refs, o_ref = rest[:-1], rest[-1]
    rps = len(mb_refs)
    n_win = kh * GRID_W
    width = HEAD_GROUP * (q_ref.shape[2] // NA_HEADS)

    for j, mb_ref in enumerate(mb_refs):
        r = pl.program_id(1) * rps + j
        rs = jnp.clip(r - kh // 2, 0, rows - kh)
        k0 = pl.multiple_of(rs * GRID_W, GRID_W)
        q0 = j * GRID_W
        for g in range(NA_HEADS // HEAD_GROUP):
            c0 = g * width
            qg = q_ref[0, q0:q0 + GRID_W, c0:c0 + width]
            kw = k_ref[0, pl.ds(k0, n_win), c0:c0 + width]
            vw = v_ref[0, pl.ds(k0, n_win), c0:c0 + width]
            kc = kc_ref[0, :, c0:c0 + width]
            vc = vc_ref[0, :, c0:c0 + width]
            bias = mb_ref[0, g * HEAD_GROUP:(g + 1) * HEAD_GROUP].reshape(HEAD_GROUP * GRID_W, n_win)
            o = _head_group_attention(qg, [kw, kc], [vw, vc], [bias, None])
            o_ref[0, q0:q0 + GRID_W, c0:c0 + width] = o.astype(o_ref.dtype)


def _window_bias(rpb, rows):
    kh = min(WIN_H, rows)
    cols = np.arange(GRID_W)
    col_start = np.clip(cols - WIN_W // 2, 0, GRID_W - WIN_W)
    ck = np.arange(GRID_W)
    in_win = (ck[None, :] >= col_start[:, None]) & (ck[None, :] < col_start[:, None] + WIN_W)
    dc = ck[None, :] - cols[:, None] + (WIN_W - 1)
    n_dc = 2 * WIN_W - 1
    sel = ((dc[None] == np.arange(n_dc)[:, None, None]) & in_win[None]).astype(np.float32)
    t = jnp.einsum("hrc,cqk->hrqk", rpb.astype(F32), jnp.asarray(sel), precision=HIGHEST)
    t = jnp.where(jnp.asarray(in_win)[None, None], t, NEG)
    per_delta = []
    for delta in range(kh):
        r0 = WIN_H - 1 - delta
        per_delta.append(jnp.transpose(t[:, r0:r0 + kh], (0, 2, 1, 3)).reshape(NA_HEADS, GRID_W, kh * GRID_W))
    return jnp.stack(per_delta) * LOG2_E


def _na_attention(q, k, v, kc, vc, mb):
    b, l, d = q.shape
    c = kc.shape[1]
    rows = l // GRID_W
    kh = mb.shape[0]

    rps = NA_ROWS_PER_STEP
    assert rows % rps == 0

    def bias_spec(j):
        def index(bi, s):
            r = s * rps + j
            return (r - jnp.clip(r - kh // 2, 0, rows - kh), 0, 0, 0)
        return pl.BlockSpec((1, NA_HEADS, GRID_W, kh * GRID_W), index)

    return pl.pallas_call(
        functools.partial(_na_kernel, rows=rows, kh=kh),
        grid=(b, rows // rps),
        in_specs=[
            pl.BlockSpec((1, rps * GRID_W, d), lambda bi, s: (bi, s, 0)),
            pl.BlockSpec((1, l, d), lambda bi, s: (bi, 0, 0)),
            pl.BlockSpec((1, l, d), lambda bi, s: (bi, 0, 0), pipeline_mode=pl.Buffered(1)),
            pl.BlockSpec((1, c, d), lambda bi, s: (bi, 0, 0)),
            pl.BlockSpec((1, c, d), lambda bi, s: (bi, 0, 0)),
        ] + [bias_spec(j) for j in range(rps)],
        out_specs=pl.BlockSpec((1, rps * GRID_W, d), lambda bi, s: (bi, s, 0)),
        out_shape=jax.ShapeDtypeStruct((b, l, d), BF16),
        compiler_params=_cparams(("parallel", "arbitrary")),
        name="na_attention",
    )(q, k, v, kc, vc, *([mb] * rps))


def _ctx_attn_kernel(q_ref, k_ref, v_ref, o_ref):
    width = HEAD_GROUP * (q_ref.shape[2] // NA_HEADS)

    def group(g, carry):
        c0 = pl.multiple_of(g * width, width)
        o = _head_group_attention(q_ref[0, :, pl.ds(c0, width)], [k_ref[0, :, pl.ds(c0, width)]],
                                  [v_ref[0, :, pl.ds(c0, width)]], [None])
        o_ref[0, :, pl.ds(c0, width)] = o.astype(o_ref.dtype)
        return carry

    lax.fori_loop(0, NA_HEADS // HEAD_GROUP, group, 0)


def _ctx_attention(qc, kc, vc):
    b, c, d = qc.shape
    spec = pl.BlockSpec((1, c, d), lambda bi: (bi, 0, 0))
    return pl.pallas_call(
        _ctx_attn_kernel,
        grid=(b,),
        in_specs=[spec, spec, spec],
        out_specs=spec,
        out_shape=jax.ShapeDtypeStruct((b, c, d), BF16),
        compiler_params=_cparams(("parallel",)),
        name="ctx_attention",
    )(qc, kc, vc)


GROUP_LANE0 = N_EXPERTS


def _src_specs(srcs, tm):
    d = srcs[0].shape[1]
    tiles = [s.shape[0] // tm for s in srcs]
    specs, first = [], 0
    for t in tiles:
        specs.append(pl.BlockSpec((tm, d), functools.partial(
            lambda i, first, t: (jnp.clip(i - first, 0, t - 1), 0), first=first, t=t)))
        first += t
    return specs, tiles


def _select_src(i, refs, tiles):
    x = refs[-1][...]
    first = sum(tiles[:-1])
    for ref, t in zip(refs[-2::-1], tiles[-2::-1]):
        x = jnp.where(i < first, ref[...], x)
        first -= t
    return x


TOK_ROWS = 8


def _store_token_major(ref, x, first=0):
    n, d = x.shape
    assert d == TOK_ROWS * LANES
    for j in range(TOK_ROWS):
        ref[pl.ds(first * TOK_ROWS + j, n, stride=TOK_ROWS), :] = x[:, j * LANES:(j + 1) * LANES]


def _load_token_major(ref, n, first=0):
    return jnp.concatenate([ref[pl.ds(first * TOK_ROWS + j, n, stride=TOK_ROWS), :] for j in range(TOK_ROWS)],
                           axis=1)


def _router_kernel(*refs, tiles):
    n_src = len(tiles)
    x_refs = refs[:n_src]
    g_ref, sh_ref, sc_ref, wr_ref, br_ref, h_ref, oh_ref, wd_ref, cnt_ref = refs[n_src:]
    i = pl.program_id(0)
    h = _norm_mod(_select_src(i, x_refs, tiles), g_ref[...], sh_ref[0], sc_ref[0])
    _store_token_major(h_ref, h)
    h_hi = h.astype(BF16)
    h_lo = (h - h_hi.astype(F32)).astype(BF16)
    wr = wr_ref[...]
    w_hi = wr.astype(BF16)
    w_lo = (wr - w_hi.astype(F32)).astype(BF16)
    logits = (jnp.dot(h_hi, w_hi, preferred_element_type=F32)
              + jnp.dot(h_lo, w_hi, preferred_element_type=F32)
              + jnp.dot(h_hi, w_lo, preferred_element_type=F32)) + br_ref[...]
    tm = logits.shape[0]
    lane = lax.broadcasted_iota(jnp.int32, (tm, LANES), 1)
    big = jnp.int32(LANES)

    gmask = (lane >= GROUP_LANE0) & (lane < GROUP_LANE0 + N_GROUPS)
    gl = jnp.where(gmask, logits, NEG)
    gmax = gl.max(axis=-1, keepdims=True)
    gsel = jnp.where(gmask & (gl == gmax), lane, big).min(axis=-1, keepdims=True) - GROUP_LANE0
    g_w = 1.0 / jnp.where(gmask, jnp.exp(gl - gmax), 0.0).sum(axis=-1, keepdims=True)

    e0 = gsel * EXPERTS_PER_GROUP
    emask = (lane >= e0) & (lane < e0 + EXPERTS_PER_GROUP)
    el = jnp.where(emask, logits, NEG)
    v1 = el.max(axis=-1, keepdims=True)
    i1 = jnp.where(emask & (el == v1), lane, big).min(axis=-1, keepdims=True)
    el2 = jnp.where(lane == i1, NEG, el)
    v2 = el2.max(axis=-1, keepdims=True)
    i2 = jnp.where(emask & (lane != i1) & (el2 == v2), lane, big).min(axis=-1, keepdims=True)
    t = jnp.exp(v2 - v1)
    w1 = g_w / (1.0 + t)
    w2 = g_w * t / (1.0 + t)

    sel1 = lane == i1
    sel2 = lane == i2
    oh = jnp.where(sel1 | sel2, 1.0, 0.0)
    oh_ref[...] = oh.astype(oh_ref.dtype)
    wd_ref[...] = jnp.where(sel1, w1, jnp.where(sel2, w2, 0.0))

    @pl.when(i == 0)
    def _():
        cnt_ref[...] = jnp.zeros_like(cnt_ref)

    cnt_ref[...] += oh.sum(axis=0, keepdims=True)


def _router(srcs, g, mod3, row_of_tile, sh_chunk, sc_chunk, wr, br, tm):
    d = srcs[0].shape[1]
    src_specs, tiles = _src_specs(srcs, tm)
    n = sum(tiles) * tm
    tok = pl.BlockSpec((tm, LANES), lambda i: (i, 0))
    return pl.pallas_call(
        functools.partial(_router_kernel, tiles=tuple(tiles)),
        grid=(sum(tiles),),
        in_specs=src_specs + [
            pl.BlockSpec((1, d), lambda i: (0, 0)),
            pl.BlockSpec((1, 1, d), lambda i: (row_of_tile(i), 0, sh_chunk)),
            pl.BlockSpec((1, 1, d), lambda i: (row_of_tile(i), 0, sc_chunk)),
            pl.BlockSpec((d, LANES), lambda i: (0, 0)),
            pl.BlockSpec((1, LANES), lambda i: (0, 0)),
        ],
        out_specs=[pl.BlockSpec((tm * TOK_ROWS, LANES), lambda i: (i, 0)), tok, tok,
                   pl.BlockSpec((1, LANES), lambda i: (0, 0))],
        out_shape=[jax.ShapeDtypeStruct((n * TOK_ROWS, LANES), F32), jax.ShapeDtypeStruct((n, LANES), BF16),
                   jax.ShapeDtypeStruct((n, LANES), F32), jax.ShapeDtypeStruct((1, LANES), F32)],
        compiler_params=_cparams(("arbitrary",)),
        name="moe_router",
    )(*srcs, g.reshape(1, d), mod3, mod3, wr, br)


def _dest_kernel(start_ref, oh_ref, wd_ref, dest_ref, wsel_ref, carry_ref):
    i = pl.program_id(0)

    @pl.when(i == 0)
    def _():
        carry_ref[...] = jnp.zeros_like(carry_ref)

    oh = oh_ref[...]
    tm = oh.shape[0]
    row = lax.broadcasted_iota(jnp.int32, (tm, tm), 0)
    col = lax.broadcasted_iota(jnp.int32, (tm, tm), 1)
    tri = jnp.where(row > col, 1.0, 0.0).astype(BF16)
    base = jnp.dot(tri, oh, preferred_element_type=F32) + carry_ref[...] + start_ref[...]
    ohf = oh.astype(F32)
    lane = lax.broadcasted_iota(jnp.int32, (tm, LANES), 1)
    sel = ohf > 0.0
    ea = jnp.where(sel, lane, LANES).min(axis=-1, keepdims=True)
    eb = jnp.where(sel, lane, -1).max(axis=-1, keepdims=True)
    wd = wd_ref[...]
    pick = lambda e, val: jnp.where(lane == e, val, 0.0).sum(axis=-1, keepdims=True)
    dcols = jnp.where(lane == 0, pick(ea, base), jnp.where(lane == 1, pick(eb, base), 0.0))
    dest_ref[0] = dcols.T[0:2, :].astype(jnp.int32)
    two = lax.broadcasted_iota(jnp.int32, (tm, 2), 1) == 0
    wsel_ref[...] = jnp.where(two, pick(ea, wd), pick(eb, wd))
    carry_ref[...] += ohf.sum(axis=0, keepdims=True)


def _dest(start, oh, wd, tm):
    n = oh.shape[0]
    tok = pl.BlockSpec((tm, LANES), lambda i: (i, 0))
    two = pl.BlockSpec((tm, 2), lambda i: (i, 0))
    return pl.pallas_call(
        _dest_kernel,
        grid=(n // tm,),
        in_specs=[pl.BlockSpec((1, LANES), lambda i: (0, 0)), tok, tok],
        out_specs=[pl.BlockSpec((1, 2, tm), lambda i: (i, 0, 0)), two],
        out_shape=[jax.ShapeDtypeStruct((n // tm, 2, tm), jnp.int32), jax.ShapeDtypeStruct((n, 2), F32)],
        scratch_shapes=[pltpu.VMEM((1, LANES), F32)],
        compiler_params=_cparams(("arbitrary",)),
        name="moe_dest",
    )(start, oh, wd)


def _tok_rows(t):
    return pl.ds(t * TOK_ROWS, TOK_ROWS)


def _wait_tokens(buf, n, sem):
    whole = buf.at[pl.ds(0, n * TOK_ROWS), :]
    pltpu.make_async_copy(whole, whole, sem).wait()


def _scatter_kernel(dest_ref, h_ref, xs_ref, sem):
    tm = h_ref.shape[0] // TOK_ROWS
    for r in range(tm):
        for k in range(2):
            pltpu.make_async_copy(h_ref.at[_tok_rows(r), :], xs_ref.at[dest_ref[0, k, r]], sem).start(priority=k)
    for k in range(2):
        _wait_tokens(h_ref, tm, sem)


def _scatter_rows(dest, h, tm):
    n = h.shape[0] // TOK_ROWS
    return pl.pallas_call(
        _scatter_kernel,
        grid=(n // tm,),
        in_specs=[
            pl.BlockSpec((1, 2, tm), lambda i: (i, 0, 0), memory_space=pltpu.SMEM),
            pl.BlockSpec((tm * TOK_ROWS, LANES), lambda i: (i, 0)),
        ],
        out_specs=pl.BlockSpec(memory_space=pl.ANY),
        out_shape=jax.ShapeDtypeStruct((2 * n, TOK_ROWS, LANES), F32),
        scratch_shapes=[pltpu.SemaphoreType.DMA(())],
        compiler_params=_cparams(("arbitrary",)),
        name="moe_scatter",
    )(dest, h)


def _gmm_kernel(vb_ref, ve_ref, lo_ref, hi_ref, x_ref, w1_ref, w3_ref, w2_ref, o_ref, w1b, w3b, w2b, *, bm):
    v = pl.program_id(0)
    pv = jnp.maximum(v - 1, 0)

    @pl.when((v == 0) | (ve_ref[v] != ve_ref[pv]))
    def _():
        w1b[...] = w1_ref[0, 0].astype(BF16)
        w3b[...] = w3_ref[0, 0].astype(BF16)
        w2b[...] = w2_ref[0, 0].astype(BF16)

    lo = lo_ref[v]
    hi = hi_ref[v]

    first = (v == 0) | (vb_ref[v] != vb_ref[pv])
    half = bm // 2

    def experts_on(r0, n):
        xb = _load_token_major(x_ref, n, r0).astype(BF16)
        a = jnp.dot(xb, w1b[...], preferred_element_type=F32)
        g = jnp.dot(xb, w3b[...], preferred_element_type=F32)
        hdn = (a * jax.nn.sigmoid(a) * g).astype(BF16)
        y = jnp.dot(hdn, w2b[...], preferred_element_type=F32)
        rows = r0 + lax.broadcasted_iota(jnp.int32, (n, 1), 0)
        mine = (rows >= lo) & (rows < hi)

        @pl.when(first)
        def _():
            _store_token_major(o_ref, jnp.where(mine, y, 0.0), r0)

        @pl.when(jnp.logical_not(first))
        def _():
            _store_token_major(o_ref, jnp.where(mine, y, _load_token_major(o_ref, n, r0)), r0)

    def zero_on_first(r0, n):
        @pl.when(first)
        def _():
            _store_token_major(o_ref, jnp.zeros((n, TOK_ROWS * LANES), F32), r0)

    @pl.when((hi > lo) & (lo < half) & (hi > half))
    def _():
        experts_on(0, bm)

    @pl.when((hi > lo) & (hi <= half))
    def _():
        experts_on(0, half)
        zero_on_first(half, half)

    @pl.when((hi > lo) & (lo >= half))
    def _():
        zero_on_first(0, half)
        experts_on(half, half)


def _gmm(visits, xs, layer, w1, w3, w2, bm):
    d, f = w1.shape[2:]
    n_vis = visits[0].shape[0]
    blk = pl.BlockSpec((bm * TOK_ROWS, LANES), lambda v, vb, ve, lo, hi: (vb[v], 0))
    grid_spec = pltpu.PrefetchScalarGridSpec(
        num_scalar_prefetch=4,
        grid=(n_vis,),
        in_specs=[
            blk,
            pl.BlockSpec((1, 1, d, f), lambda v, vb, ve, lo, hi: (layer, ve[v], 0, 0)),
            pl.BlockSpec((1, 1, d, f), lambda v, vb, ve, lo, hi: (layer, ve[v], 0, 0)),
            pl.BlockSpec((1, 1, f, d), lambda v, vb, ve, lo, hi: (layer, ve[v], 0, 0)),
        ],
        out_specs=blk,
        scratch_shapes=[pltpu.VMEM((d, f), BF16), pltpu.VMEM((d, f), BF16), pltpu.VMEM((f, d), BF16)],
    )
    return pl.pallas_call(
        functools.partial(_gmm_kernel, bm=bm),
        grid_spec=grid_spec,
        out_shape=jax.ShapeDtypeStruct(xs.shape, F32),
        compiler_params=_cparams(("arbitrary",)),
        name="moe_experts",
    )(*visits, xs, w1, w3, w2)


def _visit_plan(counts, n_rows, bm):
    counts = counts.astype(jnp.int32)
    end = jnp.cumsum(counts)
    start = end - counts
    n_blocks = n_rows // bm
    n_vis = n_blocks + N_EXPERTS
    tiles = jnp.where(counts > 0, (end - 1) // bm - start // bm + 1, 0)
    vend = jnp.cumsum(tiles)
    vstart = vend - tiles
    v = jnp.arange(n_vis, dtype=jnp.int32)[:, None]
    owns = (vstart[None, :] <= v) & (v < vend[None, :])
    take = lambda a: jnp.sum(jnp.where(owns, a[None, :], 0), axis=1)
    valid = v[:, 0] < vend[-1]
    last_e = jnp.max(jnp.where(counts > 0, jnp.arange(N_EXPERTS, dtype=jnp.int32), 0))
    e = jnp.where(valid, take(jnp.arange(N_EXPERTS, dtype=jnp.int32)), last_e)
    blk = jnp.where(valid, take(start // bm - vstart) + v[:, 0], n_blocks - 1)
    lo = jnp.where(valid, jnp.clip(take(start) - blk * bm, 0, bm), 0)
    hi = jnp.where(valid, jnp.clip(take(end) - blk * bm, 0, bm), 0)
    return start, (blk.astype(jnp.int32), e.astype(jnp.int32), lo.astype(jnp.int32), hi.astype(jnp.int32))


def _combine_kernel(dest_ref, dest_next_ref, ys_ref, *refs, tiles, final_norm):
    n_src = len(tiles)
    x_refs = refs[:n_src]
    w_ref, gate_ref = refs[n_src:n_src + 2]
    rest = refs[n_src + 2:]
    fg_ref = rest[0] if final_norm else None
    o_refs = rest[-n_src - 2:-2]
    buf, sems = rest[-2:]
    i = pl.program_id(0)
    n_steps = pl.num_programs(0)
    tm = x_refs[0].shape[0]

    def gather(d_ref, slot):
        for r in range(tm):
            for k in range(2):
                pltpu.make_async_copy(ys_ref.at[d_ref[0, k, r]], buf.at[slot, k, _tok_rows(r), :],
                                      sems.at[slot]).start(priority=k)

    @pl.when(i == 0)
    def _():
        gather(dest_ref, 0)

    @pl.when(i + 1 < n_steps)
    def _():
        gather(dest_next_ref, (i + 1) % 2)

    slot = i % 2
    for k in range(2):
        _wait_tokens(buf.at[slot, k], tm, sems.at[slot])
    w = w_ref[...]
    moe = (w[:, 0:1] * _load_token_major(buf.at[slot, 0], tm)
           + w[:, 1:2] * _load_token_major(buf.at[slot, 1], tm))
    y = _select_src(i, x_refs, tiles) + gate_ref[0] * moe
    if final_norm:
        y = y * lax.rsqrt(jnp.mean(y * y, axis=-1, keepdims=True) + EPS) * fg_ref[...]

    first = 0
    for o_ref, t in zip(o_refs, tiles):
        @pl.when((i >= first) & (i < first + t))
        def _(o_ref=o_ref):
            o_ref[...] = y
        first += t


def _combine(dest, wsel, ys, srcs, mod3, row_of_tile, gate_chunk, final_g, tm):
    d = srcs[0].shape[1]
    src_specs, tiles = _src_specs(srcs, tm)
    final_norm = final_g is not None
    extra_specs = [pl.BlockSpec((1, d), lambda i: (0, 0))] if final_norm else []
    extra_args = [final_g.reshape(1, d)] if final_norm else []
    n_steps = sum(tiles)
    return pl.pallas_call(
        functools.partial(_combine_kernel, tiles=tuple(tiles), final_norm=final_norm),
        grid=(n_steps,),
        in_specs=[
            pl.BlockSpec((1, 2, tm), lambda i: (i, 0, 0), memory_space=pltpu.SMEM),
            pl.BlockSpec((1, 2, tm), lambda i: (jnp.minimum(i + 1, n_steps - 1), 0, 0), memory_space=pltpu.SMEM),
            pl.BlockSpec(memory_space=pl.ANY),
        ] + src_specs + [
            pl.BlockSpec((tm, 2), lambda i: (i, 0)),
            pl.BlockSpec((1, 1, d), lambda i: (row_of_tile(i), 0, gate_chunk)),
        ] + extra_specs,
        out_specs=list(src_specs),
        out_shape=[jax.ShapeDtypeStruct(s.shape, F32) for s in srcs],
        scratch_shapes=[pltpu.VMEM((2, 2, tm * TOK_ROWS, LANES), F32), pltpu.SemaphoreType.DMA((2,))],
        compiler_params=_cparams(("arbitrary",)),
        name="moe_combine",
    )(dest, dest, ys, *srcs, wsel, mod3, *extra_args)


def _router_weights(w_group, b_group, w_expert, b_expert):
    d = w_group.shape[0]
    we = jnp.transpose(w_expert, (1, 0, 2)).reshape(d, N_EXPERTS)
    wr = jnp.concatenate([we, w_group, jnp.zeros((d, LANES - N_EXPERTS - N_GROUPS), F32)], axis=1)
    br = jnp.concatenate([b_expert.reshape(N_EXPERTS), b_group,
                          jnp.zeros((LANES - N_EXPERTS - N_GROUPS,), F32)]).reshape(1, LANES)
    return wr, br


def _moe_layer(srcs, g, mod3, row_of_tile, chunks, router_w, layer, w1, w3, w2, final_g, tm, bm):
    sh_chunk, sc_chunk, gate_chunk = chunks
    wr, br = _router_weights(*router_w)
    h, oh, wd, cnt = _router(srcs, g, mod3, row_of_tile, sh_chunk, sc_chunk, wr, br, tm)
    n = oh.shape[0]
    start, visits = _visit_plan(cnt[0, :N_EXPERTS], 2 * n, bm)
    start_row = jnp.zeros((1, LANES), F32).at[0, :N_EXPERTS].set(start.astype(F32))
    dest, wsel = _dest(start_row, oh, wd, tm)
    xs = _scatter_rows(dest, h, tm)
    ys = _gmm(visits, xs.reshape(2 * n * TOK_ROWS, LANES), layer, w1, w3, w2, bm)
    return _combine(dest, wsel, ys.reshape(2 * n, TOK_ROWS, LANES), srcs, mod3, row_of_tile, gate_chunk,
                    final_g, tm)


def _gelu_tanh(x):
    return x * (0.5 * (1.0 + jnp.tanh(0.7978845608028654 * (x + 0.044715 * (x * x * x)))))


def _lru_in_kernel(x_ref, g_ref, sh_ref, sc_ref, w_ref, b_ref, *o_refs, with_y):
    nb, tt, d = x_ref.shape
    h = _norm_mod(x_ref[...], g_ref[...], sh_ref[...], sc_ref[...]).reshape(nb * tt, d).astype(BF16)
    lw = o_refs[-1].shape[2]
    if with_y:
        y = jnp.dot(h, w_ref[:, 0:lw], preferred_element_type=F32) + b_ref[:, 0:lw]
        o_refs[0][...] = y.reshape(nb, tt, lw).astype(o_refs[0].dtype)
    off = w_ref.shape[1] - lw
    u = jnp.dot(h, w_ref[:, off:off + lw], preferred_element_type=F32) + b_ref[:, off:off + lw]
    o_refs[-1][...] = jnp.swapaxes(u.reshape(nb, tt, lw), 0, 1)


def _lru_in_proj(x3, g, mod3, mod_rows, w, b, with_y, tt):
    nb, t, d = x3.shape
    lw = w.shape[1] // 2 if with_y else w.shape[1]
    r0, nr = mod_rows
    mod_spec = lambda chunk: pl.BlockSpec((nr, 1, d), lambda i: (r0 // nr, 0, chunk))
    out_specs = [pl.BlockSpec((tt, nb, lw), lambda i: (i, 0, 0))]
    out_shape = [jax.ShapeDtypeStruct((t, nb, lw), F32)]
    if with_y:
        out_specs.insert(0, pl.BlockSpec((nb, tt, lw), lambda i: (0, i, 0)))
        out_shape.insert(0, jax.ShapeDtypeStruct((nb, t, lw), BF16))
    return pl.pallas_call(
        functools.partial(_lru_in_kernel, with_y=with_y),
        grid=(t // tt,),
        in_specs=[
            pl.BlockSpec((nb, tt, d), lambda i: (0, i, 0)),
            pl.BlockSpec((1, d), lambda i: (0, 0)),
            mod_spec(0), mod_spec(1),
            pl.BlockSpec(w.shape, lambda i: (0, 0)),
            pl.BlockSpec((1, w.shape[1]), lambda i: (0, 0)),
        ],
        out_specs=out_specs,
        out_shape=out_shape,
        compiler_params=_cparams(("parallel",)),
        name="rglru_in_proj",
    )(x3, g.reshape(1, d), mod3, mod3, w, b.reshape(1, w.shape[1]))


def _lru_tm_kernel(u_ref, up_ref, un_ref, cw_ref, cb_ref, wg_ref, bg_ref, lam_ref, h0_ref, *rest,
                   tt, n_t, reverse, has_other):
    other_ref = rest[0] if has_other else None
    h_ref, hend_ref, a_s, b_s, carry = rest[-5:]
    i = pl.program_id(0)
    ti = (n_t - 1 - i) if reverse else i
    _, nb, w = u_ref.shape
    cb = w // LRU_BLOCKS
    left = CONV_W // 2
    right = CONV_W - 1 - left

    @pl.when(i == 0)
    def _():
        carry[...] = h0_ref[...]

    prev = jnp.where(ti > 0, up_ref[...], 0.0)
    nxt = jnp.where(ti < n_t - 1, un_ref[...], 0.0)
    for n in range(LRU_BLOCKS):
        c0 = n * cb
        ext = jnp.concatenate([prev[:, :, c0:c0 + cb], u_ref[:, :, c0:c0 + cb], nxt[:, :, c0:c0 + cb]], axis=0)
        uc = cb_ref[:, c0:c0 + cb] + cw_ref[0:1, c0:c0 + cb] * ext[0:tt]
        for kk in range(1, CONV_W):
            uc = uc + cw_ref[kk:kk + 1, c0:c0 + cb] * ext[kk:kk + tt]
        ub = uc.reshape(tt * nb, cb).astype(BF16)
        r = jax.nn.sigmoid(jnp.dot(ub, wg_ref[0, n], preferred_element_type=F32) + bg_ref[0, :, c0:c0 + cb])
        ig = jax.nn.sigmoid(jnp.dot(ub, wg_ref[1, n], preferred_element_type=F32) + bg_ref[1, :, c0:c0 + cb])
        lam = lam_ref[:, c0:c0 + cb]
        rate = (-LRU_C * LOG2_E) * jnp.log1p(jnp.exp(-lam))
        a = jnp.exp2(r * rate)
        z = 1.0 - a * a
        root = jnp.where(z > 0.0, z * lax.rsqrt(z), 0.0)
        b = root * (ig * uc.reshape(tt * nb, cb))
        a_s[:, :, c0:c0 + cb] = a.reshape(tt, nb, cb)
        b_s[:, :, c0:c0 + cb] = b.reshape(tt, nb, cb)

    def step(s, h):
        t = (tt - 1 - s) if reverse else s
        h = a_s[t] * h + b_s[t]
        h_ref[t] = (h + other_ref[t]) if has_other else h
        return h

    h_last = lax.fori_loop(0, tt, step, carry[...], unroll=8)
    carry[...] = h_last

    @pl.when(i == n_t - 1)
    def _():
        hend_ref[...] = h_last


def _lru_tm_scan(u, h0, conv_w, conv_b, wg, bg, lam, reverse, tt, other=None):
    t, nb, w = u.shape
    n_t = t // tt
    left = CONV_W // 2
    right = CONV_W - 1 - left
    assert tt % left == 0 and t % tt == 0
    tmap = (lambda i: n_t - 1 - i) if reverse else (lambda i: i)
    full = lambda a: pl.BlockSpec(a.shape, lambda i: (0,) * a.ndim)
    conv_b = conv_b.reshape(1, w)
    bg = bg.reshape(2, 1, w)
    lam = lam.reshape(1, w)
    tile = pl.BlockSpec((tt, nb, w), lambda i: (tmap(i), 0, 0))
    others = [] if other is None else [other]
    return pl.pallas_call(
        functools.partial(_lru_tm_kernel, tt=tt, n_t=n_t, reverse=reverse, has_other=other is not None),
        grid=(n_t,),
        in_specs=[
            tile,
            pl.BlockSpec((left, nb, w), lambda i: (jnp.maximum(tmap(i) * (tt // left) - 1, 0), 0, 0)),
            pl.BlockSpec((right, nb, w), lambda i: (jnp.minimum((tmap(i) + 1) * (tt // right), t // right - 1), 0, 0)),
            full(conv_w), full(conv_b), full(wg), full(bg), full(lam), full(h0),
        ] + [tile] * len(others),
        out_specs=[tile, full(h0)],
        out_shape=[jax.ShapeDtypeStruct((t, nb, w), F32), jax.ShapeDtypeStruct(h0.shape, F32)],
        scratch_shapes=[pltpu.VMEM((tt, nb, w), F32), pltpu.VMEM((tt, nb, w), F32), pltpu.VMEM(h0.shape, F32)],
        compiler_params=_cparams(("arbitrary",)),
        name="rglru_scan_rev" if reverse else "rglru_scan_fwd",
    )(u, u, u, conv_w, conv_b, wg, bg, lam, h0, *others)


def _lru_out_tm_kernel(y_ref, h_ref, w_ref, b_ref, res_ref, gate_ref, o_ref):
    nb, tt, lw = y_ref.shape
    y = jnp.swapaxes(y_ref[...].astype(F32), 0, 1)
    a = (_gelu_tanh(y) * h_ref[...]).reshape(tt * nb, lw).astype(BF16)
    z = jnp.dot(a, w_ref[...], preferred_element_type=F32) + b_ref[...]
    z = jnp.swapaxes(z.reshape(tt, nb, z.shape[1]), 0, 1)
    o_ref[...] = res_ref[...] + gate_ref[...] * z


def _lru_out_tm_proj(y, h, w, b, res3, mod3, gate_chunk, tt):
    nb, t, lw = y.shape
    d = w.shape[1]
    tm_spec = pl.BlockSpec((tt, nb, lw), lambda i: (i, 0, 0))
    return pl.pallas_call(
        _lru_out_tm_kernel,
        grid=(t // tt,),
        in_specs=[
            pl.BlockSpec((nb, tt, lw), lambda i: (0, i, 0)),
            tm_spec,
            pl.BlockSpec((lw, d), lambda i: (0, 0)),
            pl.BlockSpec((1, d), lambda i: (0, 0)),
            pl.BlockSpec((nb, tt, d), lambda i: (0, i, 0)),
            pl.BlockSpec((nb, 1, d), lambda i: (0, 0, gate_chunk)),
        ],
        out_specs=pl.BlockSpec((nb, tt, d), lambda i: (0, i, 0)),
        out_shape=jax.ShapeDtypeStruct((nb, t, d), F32),
        compiler_params=_cparams(("parallel",)),
        name="rglru_out_proj",
    )(y, h, w, b.reshape(1, d), res3, mod3)


def _lru_states(ul, uc, conv_w, conv_b, w_gate, b_gate, lam, tt):
    wg = w_gate.astype(BF16)
    _, nb, w = ul.shape
    zero = jnp.zeros((nb, w), F32)
    h = None
    for dirn in range(2):
        rev = dirn == 1
        args = (conv_w, conv_b, wg[dirn], b_gate[dirn], lam[dirn], rev, tt)
        _, h_end = _lru_tm_scan(uc, zero, *args)
        h, _ = _lru_tm_scan(ul, h_end, *args, other=h)
    return h


TM = 512
TD = 1024
BM = 512
TT = 128


def kernel(x, c, ctx, c_ctx, ada_w, ada_b, norm_g, na_w_qkv, na_b_qkv, na_rpb, na_w_o, na_b_o, lru_w_in, lru_b_in, lru_conv_w, lru_conv_b, lru_w_gate, lru_b_gate, lru_lambda, lru_w_o, lru_b_o, moe_w_group, moe_b_group, moe_w_expert, moe_b_expert, moe_w1, moe_w3, moe_w2, final_g):
    b, l, d = x.shape
    n_ctx = ctx.shape[1]
    n_l, n_c = b * l, b * n_ctx
    assert l % TM == 0 and l % TD == 0 and n_ctx <= TM and TM % n_ctx == 0 and b + 1 <= MOD_ROWS
    ctx_row = b
    tiles_per_batch = l // TM
    lat_row = lambda i: i // tiles_per_batch
    dense_row = lambda i: i // (l // TD)
    ctx_tile_row = lambda i: ctx_row
    nl_tiles = n_l // TM
    both_row = lambda i: jnp.where(i < nl_tiles, i // tiles_per_batch, ctx_row)

    cvec = jnp.concatenate([c, c_ctx[None], jnp.zeros((MOD_ROWS - b - 1, d), F32)], axis=0)
    mod = _modulation(cvec, ada_w, ada_b)
    mod3 = [mod[i].reshape(MOD_ROWS, 1, N_MOD * d) for i in range(mod.shape[0])]
    xl = x.reshape(n_l, d)
    xc = ctx.reshape(n_c, d)

    w_qkv = na_w_qkv[0].astype(BF16)
    qk_scale = (d // NA_HEADS) ** -0.5 * LOG2_E
    q, k, v = _proj(xl, norm_g[0, 0], mod3[0], dense_row, 0, 1, w_qkv, na_b_qkv[0],
                    (d, d, d), (BF16,) * 3, (qk_scale, 1.0, 1.0), TD)
    qc, kc, vc = _proj(xc, norm_g[0, 0], mod3[0], ctx_tile_row, 0, 1, w_qkv, na_b_qkv[0],
                       (d, d, d), (BF16,) * 3, (qk_scale, 1.0, 1.0), n_ctx)
    to3 = lambda a, s: a.reshape(b, s, d)
    mb = _window_bias(na_rpb[0], l // GRID_W)
    o_l = _na_attention(to3(q, l), to3(k, l), to3(v, l), to3(kc, n_ctx), to3(vc, n_ctx), mb)
    o_c = _ctx_attention(to3(qc, n_ctx), to3(kc, n_ctx), to3(vc, n_ctx))
    w_o = na_w_o[0].astype(BF16)
    xl = _resid_proj(o_l.reshape(n_l, d), w_o, na_b_o[0], xl, mod3[0], dense_row, 2, TD)
    xc = _resid_proj(o_c.reshape(n_c, d), w_o, na_b_o[0], xc, mod3[0], ctx_tile_row, 2, n_ctx)
    xl, xc = _moe_layer([xl, xc], norm_g[0, 1], mod3[0], both_row, (3, 4, 5),
                        (moe_w_group[0], moe_b_group[0], moe_w_expert[0], moe_b_expert[0]),
                        0, moe_w1, moe_w3, moe_w2, None, TM, BM)

    w_in = lru_w_in[0].astype(BF16)
    lw = w_in.shape[1] // 2
    xl3 = xl.reshape(b, l, d)
    y_l, u_l = _lru_in_proj(xl3, norm_g[1, 0], mod3[1], (0, b), w_in, lru_b_in[0], True, TT)
    (u_c,) = _lru_in_proj(xc.reshape(b, n_ctx, d), norm_g[1, 0], mod3[1], (ctx_row, 1), w_in[:, lw:],
                          lru_b_in[0, lw:], False, TT)
    h_sum = _lru_states(u_l, u_c, lru_conv_w[0], lru_conv_b[0], lru_w_gate[0], lru_b_gate[0],
                        lru_lambda[0], TT)
    xl = _lru_out_tm_proj(y_l, h_sum, lru_w_o[0].astype(BF16), lru_b_o[0], xl3, mod3[1], 2, TT)
    xl = xl.reshape(n_l, d)
    (out,) = _moe_layer([xl], norm_g[1, 1], mod3[1], lat_row, (3, 4, 5),
                        (moe_w_group[1], moe_b_group[1], moe_w_expert[1], moe_b_expert[1]),
                        1, moe_w1, moe_w3, moe_w2, final_g, TM, BM)
    return out.reshape(b, l, d)
```

```python
import functools

import jax
import jax.numpy as jnp
import numpy as np
from jax import lax
from jax.experimental import pallas as pl
from jax.experimental.pallas import tpu as pltpu

F32 = jnp.float32
BF16 = jnp.bfloat16

GRID_W = 64
N_MOD = 6
NA_HEADS = 16
WIN_H = 8
WIN_W = 16
LRU_BLOCKS = 4
CONV_W = 4
LRU_C = 8.0
N_GROUPS = 4
EXPERTS_PER_GROUP = 8
N_EXPERTS = N_GROUPS * EXPERTS_PER_GROUP
EPS = 1e-6

LANES = 128
MOD_ROWS = 16
NEG = -1e30
VMEM_LIMIT = 56 * 1024 * 1024
HIGHEST = lax.Precision.HIGHEST
LOG2_E = 1.4426950408889634


def _cparams(sem, vmem=VMEM_LIMIT):
    return pltpu.CompilerParams(dimension_semantics=sem, vmem_limit_bytes=vmem)


def _mod_kernel(c_ref, w_ref, b_ref, o_ref):
    c = c_ref[...]
    s = c * jax.nn.sigmoid(c)
    o_ref[0] = jnp.dot(s, w_ref[0], precision=HIGHEST, preferred_element_type=F32) + b_ref[0]


def _modulation(cvec, ada_w, ada_b):
    depth, d, n = ada_w.shape
    tn = 1536
    return pl.pallas_call(
        _mod_kernel,
        grid=(depth, n // tn),
        in_specs=[
            pl.BlockSpec((MOD_ROWS, d), lambda l, j: (0, 0)),
            pl.BlockSpec((1, d, tn), lambda l, j: (l, 0, j)),
            pl.BlockSpec((1, 1, tn), lambda l, j: (l, 0, j)),
        ],
        out_specs=pl.BlockSpec((1, MOD_ROWS, tn), lambda l, j: (l, 0, j)),
        out_shape=jax.ShapeDtypeStruct((depth, MOD_ROWS, n), F32),
        compiler_params=_cparams(("arbitrary", "arbitrary")),
        name="adaln_mod",
    )(cvec, ada_w, ada_b.reshape(depth, 1, n))


def _norm_mod(x, g, sh, sc):
    ms = jnp.mean(x * x, axis=-1, keepdims=True)
    y = x * lax.rsqrt(ms + EPS) * g
    return y * (1.0 + sc) + sh


def _proj_kernel(x_ref, g_ref, sh_ref, sc_ref, w_ref, b_ref, *o_refs, splits, scales):
    h = _norm_mod(x_ref[...], g_ref[...], sh_ref[0], sc_ref[0]).astype(BF16)
    off = 0
    for o_ref, n, s in zip(o_refs, splits, scales):
        y = jnp.dot(h, w_ref[:, off:off + n], preferred_element_type=F32) + b_ref[:, off:off + n]
        if s != 1.0:
            y = y * s
        o_ref[...] = y.astype(o_ref.dtype)
        off += n


def _proj(x2d, g, mod3, row_of_tile, sh_chunk, sc_chunk, w, b, splits, dtypes, scales, tm):
    n_tok, d = x2d.shape
    n_out = w.shape[1]
    assert sum(splits) == n_out and n_tok % tm == 0
    return pl.pallas_call(
        functools.partial(_proj_kernel, splits=tuple(splits), scales=tuple(scales)),
        grid=(n_tok // tm,),
        in_specs=[
            pl.BlockSpec((tm, d), lambda i: (i, 0)),
            pl.BlockSpec((1, d), lambda i: (0, 0)),
            pl.BlockSpec((1, 1, d), lambda i: (row_of_tile(i), 0, sh_chunk)),
            pl.BlockSpec((1, 1, d), lambda i: (row_of_tile(i), 0, sc_chunk)),
            pl.BlockSpec((d, n_out), lambda i: (0, 0)),
            pl.BlockSpec((1, n_out), lambda i: (0, 0)),
        ],
        out_specs=[pl.BlockSpec((tm, n), lambda i: (i, 0)) for n in splits],
        out_shape=[jax.ShapeDtypeStruct((n_tok, n), dt) for n, dt in zip(splits, dtypes)],
        compiler_params=_cparams(("parallel",)),
        name="norm_mod_proj",
    )(x2d, g.reshape(1, d), mod3, mod3, w, b.reshape(1, n_out))


def _resid_kernel(a_ref, w_ref, b_ref, res_ref, gate_ref, o_ref):
    y = jnp.dot(a_ref[...], w_ref[...], preferred_element_type=F32) + b_ref[...]
    o_ref[...] = res_ref[...] + gate_ref[0] * y


def _resid_proj(a, w, b, res, mod3, row_of_tile, gate_chunk, tm):
    n_tok, k = a.shape
    d = w.shape[1]
    return pl.pallas_call(
        _resid_kernel,
        grid=(n_tok // tm,),
        in_specs=[
            pl.BlockSpec((tm, k), lambda i: (i, 0)),
            pl.BlockSpec((k, d), lambda i: (0, 0)),
            pl.BlockSpec((1, d), lambda i: (0, 0)),
            pl.BlockSpec((tm, d), lambda i: (i, 0)),
            pl.BlockSpec((1, 1, d), lambda i: (row_of_tile(i), 0, gate_chunk)),
        ],
        out_specs=pl.BlockSpec((tm, d), lambda i: (i, 0)),
        out_shape=jax.ShapeDtypeStruct((n_tok, d), F32),
        compiler_params=_cparams(("parallel",)),
        name="proj_residual",
    )(a, w, b.reshape(1, d), res, mod3)


HEAD_GROUP = 4


def _head_group_attention(qg, keys, vals, biases):
    n_q, width = qg.shape
    dh = width // HEAD_GROUP
    head_of_lane = lax.broadcasted_iota(jnp.int32, (n_q, width), 1) // dh
    qs = jnp.concatenate([jnp.where(head_of_lane == h, qg, jnp.zeros_like(qg)) for h in range(HEAD_GROUP)],
                         axis=0)
    s_parts = []
    for kk, bb in zip(keys, biases):
        s = lax.dot_general(qs, kk, (((1,), (1,)), ((), ())), preferred_element_type=F32)
        s_parts.append(s if bb is None else s + bb)
    m = s_parts[0].max(axis=-1, keepdims=True)
    for s in s_parts[1:]:
        m = jnp.maximum(m, s.max(axis=-1, keepdims=True))
    p_parts = [jnp.exp2(s - m) for s in s_parts]
    l = p_parts[0].sum(axis=-1, keepdims=True)
    for p in p_parts[1:]:
        l = l + p.sum(axis=-1, keepdims=True)
    o = None
    for p, vv in zip(p_parts, vals):
        t = jnp.dot(p.astype(BF16), vv, preferred_element_type=F32)
        o = t if o is None else o + t
    o = o / l
    out = jnp.where(head_of_lane == 0, o[0:n_q], 0.0)
    for h in range(1, HEAD_GROUP):
        out = jnp.where(head_of_lane == h, o[h * n_q:(h + 1) * n_q], out)
    return out


NA_ROWS_PER_STEP = 4


def _na_kernel(q_ref, k_ref, v_ref, kc_ref, vc_ref, *rest, rows, kh):
    mb_refs, o_ref = rest[:-1], rest[-1]
    rps = len(mb_refs)
    n_win = kh * GRID_W
    width = HEAD_GROUP * (q_ref.shape[2] // NA_HEADS)

    for j, mb_ref in enumerate(mb_refs):
        r = pl.program_id(1) * rps + j
        rs = jnp.clip(r - kh // 2, 0, rows - kh)
        k0 = pl.multiple_of(rs * GRID_W, GRID_W)
        q0 = j * GRID_W
        for g in range(NA_HEADS // HEAD_GROUP):
            c0 = g * width
            qg = q_ref[0, q0:q0 + GRID_W, c0:c0 + width]
            kw = k_ref[0, pl.ds(k0, n_win), c0:c0 + width]
            vw = v_ref[0, pl.ds(k0, n_win), c0:c0 + width]
            kc = kc_ref[0, :, c0:c0 + width]
            vc = vc_ref[0, :, c0:c0 + width]
            bias = mb_ref[0, g * HEAD_GROUP:(g + 1) * HEAD_GROUP].reshape(HEAD_GROUP * GRID_W, n_win)
            o = _head_group_attention(qg, [kw, kc], [vw, vc], [bias, None])
            o_ref[0, q0:q0 + GRID_W, c0:c0 + width] = o.astype(o_ref.dtype)


def _window_bias(rpb, rows):
    kh = min(WIN_H, rows)
    cols = np.arange(GRID_W)
    col_start = np.clip(cols - WIN_W // 2, 0, GRID_W - WIN_W)
    ck = np.arange(GRID_W)
    in_win = (ck[None, :] >= col_start[:, None]) & (ck[None, :] < col_start[:, None] + WIN_W)
    dc = ck[None, :] - cols[:, None] + (WIN_W - 1)
    n_dc = 2 * WIN_W - 1
    sel = ((dc[None] == np.arange(n_dc)[:, None, None]) & in_win[None]).astype(np.float32)
    t = jnp.einsum("hrc,cqk->hrqk", rpb.astype(F32), jnp.asarray(sel), precision=HIGHEST)
    t = jnp.where(jnp.asarray(in_win)[None, None], t, NEG)
    per_delta = []
    for delta in range(kh):
        r0 = WIN_H - 1 - delta
        per_delta.append(jnp.transpose(t[:, r0:r0 + kh], (0, 2, 1, 3)).reshape(NA_HEADS, GRID_W, kh * GRID_W))
    return jnp.stack(per_delta) * LOG2_E


def _na_attention(q, k, v, kc, vc, mb):
    b, l, d = q.shape
    c = kc.shape[1]
    rows = l // GRID_W
    kh = mb.shape[0]

    rps = NA_ROWS_PER_STEP
    assert rows % rps == 0

    def bias_spec(j):
        def index(bi, s):
            r = s * rps + j
            return (r - jnp.clip(r - kh // 2, 0, rows - kh), 0, 0, 0)
        return pl.BlockSpec((1, NA_HEADS, GRID_W, kh * GRID_W), index)

    return pl.pallas_call(
        functools.partial(_na_kernel, rows=rows, kh=kh),
        grid=(b, rows // rps),
        in_specs=[
            pl.BlockSpec((1, rps * GRID_W, d), lambda bi, s: (bi, s, 0)),
            pl.BlockSpec((1, l, d), lambda bi, s: (bi, 0, 0)),
            pl.BlockSpec((1, l, d), lambda bi, s: (bi, 0, 0), pipeline_mode=pl.Buffered(1)),
            pl.BlockSpec((1, c, d), lambda bi, s: (bi, 0, 0)),
            pl.BlockSpec((1, c, d), lambda bi, s: (bi, 0, 0)),
        ] + [bias_spec(j) for j in range(rps)],
        out_specs=pl.BlockSpec((1, rps * GRID_W, d), lambda bi, s: (bi, s, 0)),
        out_shape=jax.ShapeDtypeStruct((b, l, d), BF16),
        compiler_params=_cparams(("parallel", "arbitrary")),
        name="na_attention",
    )(q, k, v, kc, vc, *([mb] * rps))


def _ctx_attn_kernel(q_ref, k_ref, v_ref, o_ref):
    width = HEAD_GROUP * (q_ref.shape[2] // NA_HEADS)

    def group(g, carry):
        c0 = pl.multiple_of(g * width, width)
        o = _head_group_attention(q_ref[0, :, pl.ds(c0, width)], [k_ref[0, :, pl.ds(c0, width)]],
                                  [v_ref[0, :, pl.ds(c0, width)]], [None])
        o_ref[0, :, pl.ds(c0, width)] = o.astype(o_ref.dtype)
        return carry

    lax.fori_loop(0, NA_HEADS // HEAD_GROUP, group, 0)


def _ctx_attention(qc, kc, vc):
    b, c, d = qc.shape
    spec = pl.BlockSpec((1, c, d), lambda bi: (bi, 0, 0))
    return pl.pallas_call(
        _ctx_attn_kernel,
        grid=(b,),
        in_specs=[spec, spec, spec],
        out_specs=spec,
        out_shape=jax.ShapeDtypeStruct((b, c, d), BF16),
        compiler_params=_cparams(("parallel",)),
        name="ctx_attention",
    )(qc, kc, vc)


GROUP_LANE0 = N_EXPERTS


def _src_specs(srcs, tm):
    d = srcs[0].shape[1]
    tiles = [s.shape[0] // tm for s in srcs]
    specs, first = [], 0
    for t in tiles:
        specs.append(pl.BlockSpec((tm, d), functools.partial(
            lambda i, first, t: (jnp.clip(i - first, 0, t - 1), 0), first=first, t=t)))
        first += t
    return specs, tiles


def _select_src(i, refs, tiles):
    x = refs[-1][...]
    first = sum(tiles[:-1])
    for ref, t in zip(refs[-2::-1], tiles[-2::-1]):
        x = jnp.where(i < first, ref[...], x)
        first -= t
    return x


TOK_ROWS = 8


def _store_token_major(ref, x, first=0):
    n, d = x.shape
    assert d == TOK_ROWS * LANES
    for j in range(TOK_ROWS):
        ref[pl.ds(first * TOK_ROWS + j, n, stride=TOK_ROWS), :] = x[:, j * LANES:(j + 1) * LANES]


def _load_token_major(ref, n, first=0):
    return jnp.concatenate([ref[pl.ds(first * TOK_ROWS + j, n, stride=TOK_ROWS), :] for j in range(TOK_ROWS)],
                           axis=1)


def _router_kernel(*refs, tiles):
    n_src = len(tiles)
    x_refs = refs[:n_src]
    g_ref, sh_ref, sc_ref, wr_ref, br_ref, h_ref, rank_ref, eid_ref, wsel_ref, cnt_ref = refs[n_src:]
    i = pl.program_id(0)
    h = _norm_mod(_select_src(i, x_refs, tiles), g_ref[...], sh_ref[0], sc_ref[0])
    _store_token_major(h_ref, h)
    h_hi = h.astype(BF16)
    h_lo = (h - h_hi.astype(F32)).astype(BF16)
    wr = wr_ref[...]
    w_hi = wr.astype(BF16)
    w_lo = (wr - w_hi.astype(F32)).astype(BF16)
    logits = (jnp.dot(h_hi, w_hi, preferred_element_type=F32)
              + jnp.dot(h_lo, w_hi, preferred_element_type=F32)
              + jnp.dot(h_hi, w_lo, preferred_element_type=F32)) + br_ref[...]
    tm = logits.shape[0]
    lane = lax.broadcasted_iota(jnp.int32, (tm, LANES), 1)
    big = jnp.int32(LANES)

    gmask = (lane >= GROUP_LANE0) & (lane < GROUP_LANE0 + N_GROUPS)
    gl = jnp.where(gmask, logits, NEG)
    gmax = gl.max(axis=-1, keepdims=True)
    gsel = jnp.where(gmask & (gl == gmax), lane, big).min(axis=-1, keepdims=True) - GROUP_LANE0
    g_w = 1.0 / jnp.where(gmask, jnp.exp(gl - gmax), 0.0).sum(axis=-1, keepdims=True)

    e0 = gsel * EXPERTS_PER_GROUP
    emask = (lane >= e0) & (lane < e0 + EXPERTS_PER_GROUP)
    el = jnp.where(emask, logits, NEG)
    v1 = el.max(axis=-1, keepdims=True)
    i1 = jnp.where(emask & (el == v1), lane, big).min(axis=-1, keepdims=True)
    el2 = jnp.where(lane == i1, NEG, el)
    v2 = el2.max(axis=-1, keepdims=True)
    i2 = jnp.where(emask & (lane != i1) & (el2 == v2), lane, big).min(axis=-1, keepdims=True)
    t = jnp.exp(v2 - v1)
    w1 = g_w / (1.0 + t)
    w2 = g_w * t / (1.0 + t)

    @pl.when(i == 0)
    def _():
        cnt_ref[...] = jnp.zeros_like(cnt_ref)

    oh = jnp.where((lane == i1) | (lane == i2), 1.0, 0.0)
    row = lax.broadcasted_iota(jnp.int32, (tm, tm), 0)
    col = lax.broadcasted_iota(jnp.int32, (tm, tm), 1)
    tri = jnp.where(row > col, 1.0, 0.0).astype(BF16)
    before = jnp.dot(tri, oh.astype(BF16), preferred_element_type=F32) + cnt_ref[...]
    first_is_low = i1 < i2
    ea = jnp.where(first_is_low, i1, i2)
    eb = jnp.where(first_is_low, i2, i1)
    pick = lambda e: jnp.where(lane == e, before, 0.0).sum(axis=-1, keepdims=True)
    cols = jnp.where(lane == 0, pick(ea), jnp.where(lane == 1, pick(eb), jnp.where(
        lane == 2, ea.astype(F32), jnp.where(lane == 3, eb.astype(F32), 0.0))))
    cols_t = cols.T
    rank_ref[0] = cols_t[0:2, :].astype(jnp.int32)
    eid_ref[0] = cols_t[2:4, :].astype(jnp.int32)
    two = lax.broadcasted_iota(jnp.int32, (tm, 2), 1) == 0
    wsel_ref[...] = jnp.where(two, jnp.where(first_is_low, w1, w2), jnp.where(first_is_low, w2, w1))
    cnt_ref[...] += oh.sum(axis=0, keepdims=True)


def _router(srcs, g, mod3, row_of_tile, sh_chunk, sc_chunk, wr, br, tm):
    d = srcs[0].shape[1]
    src_specs, tiles = _src_specs(srcs, tm)
    n = sum(tiles) * tm
    slots = pl.BlockSpec((1, 2, tm), lambda i: (i, 0, 0))
    slot_shape = jax.ShapeDtypeStruct((n // tm, 2, tm), jnp.int32)
    return pl.pallas_call(
        functools.partial(_router_kernel, tiles=tuple(tiles)),
        grid=(sum(tiles),),
        in_specs=src_specs + [
            pl.BlockSpec((1, d), lambda i: (0, 0)),
            pl.BlockSpec((1, 1, d), lambda i: (row_of_tile(i), 0, sh_chunk)),
            pl.BlockSpec((1, 1, d), lambda i: (row_of_tile(i), 0, sc_chunk)),
            pl.BlockSpec((d, LANES), lambda i: (0, 0)),
            pl.BlockSpec((1, LANES), lambda i: (0, 0)),
        ],
        out_specs=[pl.BlockSpec((tm * TOK_ROWS, LANES), lambda i: (i, 0)), slots, slots,
                   pl.BlockSpec((tm, 2), lambda i: (i, 0)), pl.BlockSpec((1, LANES), lambda i: (0, 0))],
        out_shape=[jax.ShapeDtypeStruct((n * TOK_ROWS, LANES), F32), slot_shape, slot_shape,
                   jax.ShapeDtypeStruct((n, 2), F32), jax.ShapeDtypeStruct((1, LANES), F32)],
        compiler_params=_cparams(("arbitrary",)),
        name="moe_router",
    )(*srcs, g.reshape(1, d), mod3, mod3, wr, br)


def _tok_rows(t):
    return pl.ds(t * TOK_ROWS, TOK_ROWS)


def _wait_tokens(buf, n, sem):
    whole = buf.at[pl.ds(0, n * TOK_ROWS), :]
    pltpu.make_async_copy(whole, whole, sem).wait()


def _scatter_kernel(dest_ref, h_ref, xs_ref, sem):
    tm = h_ref.shape[0] // TOK_ROWS
    for r in range(tm):
        for k in range(2):
            pltpu.make_async_copy(h_ref.at[_tok_rows(r), :], xs_ref.at[dest_ref[0, k, r]], sem).start(priority=k)
    for k in range(2):
        _wait_tokens(h_ref, tm, sem)


def _scatter_rows(dest, h, tm):
    n = h.shape[0] // TOK_ROWS
    return pl.pallas_call(
        _scatter_kernel,
        grid=(n // tm,),
        in_specs=[
            pl.BlockSpec((1, 2, tm), lambda i: (i, 0, 0), memory_space=pltpu.SMEM),
            pl.BlockSpec((tm * TOK_ROWS, LANES), lambda i: (i, 0)),
        ],
        out_specs=pl.BlockSpec(memory_space=pl.ANY),
        out_shape=jax.ShapeDtypeStruct((2 * n, TOK_ROWS, LANES), F32),
        scratch_shapes=[pltpu.SemaphoreType.DMA(())],
        compiler_params=_cparams(("arbitrary",)),
        name="moe_scatter",
    )(dest, h)


def _gmm_kernel(vb_ref, ve_ref, lo_ref, hi_ref, x_ref, w1_ref, w3_ref, w2_ref, o_ref, w1b, w3b, w2b, *, bm):
    v = pl.program_id(0)
    pv = jnp.maximum(v - 1, 0)

    @pl.when((v == 0) | (ve_ref[v] != ve_ref[pv]))
    def _():
        w1b[...] = w1_ref[0, 0].astype(BF16)
        w3b[...] = w3_ref[0, 0].astype(BF16)
        w2b[...] = w2_ref[0, 0].astype(BF16)

    lo = lo_ref[v]
    hi = hi_ref[v]

    first = (v == 0) | (vb_ref[v] != vb_ref[pv])
    half = bm // 2

    def experts_on(r0, n):
        xb = _load_token_major(x_ref, n, r0).astype(BF16)
        a = jnp.dot(xb, w1b[...], preferred_element_type=F32)
        g = jnp.dot(xb, w3b[...], preferred_element_type=F32)
        hdn = (a * jax.nn.sigmoid(a) * g).astype(BF16)
        y = jnp.dot(hdn, w2b[...], preferred_element_type=F32)
        rows = r0 + lax.broadcasted_iota(jnp.int32, (n, 1), 0)
        mine = (rows >= lo) & (rows < hi)

        @pl.when(first)
        def _():
            _store_token_major(o_ref, jnp.where(mine, y, 0.0), r0)

        @pl.when(jnp.logical_not(first))
        def _():
            _store_token_major(o_ref, jnp.where(mine, y, _load_token_major(o_ref, n, r0)), r0)

    def zero_on_first(r0, n):
        @pl.when(first)
        def _():
            _store_token_major(o_ref, jnp.zeros((n, TOK_ROWS * LANES), F32), r0)

    @pl.when((hi > lo) & (lo < half) & (hi > half))
    def _():
        experts_on(0, bm)

    @pl.when((hi > lo) & (hi <= half))
    def _():
        experts_on(0, half)
        zero_on_first(half, half)

    @pl.when((hi > lo) & (lo >= half))
    def _():
        zero_on_first(0, half)
        experts_on(half, half)


def _gmm(visits, xs, layer, w1, w3, w2, bm):
    d, f = w1.shape[2:]
    n_vis = visits[0].shape[0]
    blk = pl.BlockSpec((bm * TOK_ROWS, LANES), lambda v, vb, ve, lo, hi: (vb[v], 0))
    grid_spec = pltpu.PrefetchScalarGridSpec(
        num_scalar_prefetch=4,
        grid=(n_vis,),
        in_specs=[
            blk,
            pl.BlockSpec((1, 1, d, f), lambda v, vb, ve, lo, hi: (layer, ve[v], 0, 0)),
            pl.BlockSpec((1, 1, d, f), lambda v, vb, ve, lo, hi: (layer, ve[v], 0, 0)),
            pl.BlockSpec((1, 1, f, d), lambda v, vb, ve, lo, hi: (layer, ve[v], 0, 0)),
        ],
        out_specs=blk,
        scratch_shapes=[pltpu.VMEM((d, f), BF16), pltpu.VMEM((d, f), BF16), pltpu.VMEM((f, d), BF16)],
    )
    return pl.pallas_call(
        functools.partial(_gmm_kernel, bm=bm),
        grid_spec=grid_spec,
        out_shape=jax.ShapeDtypeStruct(xs.shape, F32),
        compiler_params=_cparams(("arbitrary",)),
        name="moe_experts",
    )(*visits, xs, w1, w3, w2)


def _visit_plan(counts, n_rows, bm):
    counts = counts.astype(jnp.int32)
    end = jnp.cumsum(counts)
    start = end - counts
    n_blocks = n_rows // bm
    n_vis = n_blocks + N_EXPERTS
    tiles = jnp.where(counts > 0, (end - 1) // bm - start // bm + 1, 0)
    vend = jnp.cumsum(tiles)
    vstart = vend - tiles
    v = jnp.arange(n_vis, dtype=jnp.int32)[:, None]
    owns = (vstart[None, :] <= v) & (v < vend[None, :])
    take = lambda a: jnp.sum(jnp.where(owns, a[None, :], 0), axis=1)
    valid = v[:, 0] < vend[-1]
    last_e = jnp.max(jnp.where(counts > 0, jnp.arange(N_EXPERTS, dtype=jnp.int32), 0))
    e = jnp.where(valid, take(jnp.arange(N_EXPERTS, dtype=jnp.int32)), last_e)
    blk = jnp.where(valid, take(start // bm - vstart) + v[:, 0], n_blocks - 1)
    lo = jnp.where(valid, jnp.clip(take(start) - blk * bm, 0, bm), 0)
    hi = jnp.where(valid, jnp.clip(take(end) - blk * bm, 0, bm), 0)
    return start, (blk.astype(jnp.int32), e.astype(jnp.int32), lo.astype(jnp.int32), hi.astype(jnp.int32))


def _combine_kernel(dest_ref, dest_next_ref, ys_ref, *refs, tiles, final_norm):
    n_src = len(tiles)
    x_refs = refs[:n_src]
    w_ref, gate_ref = refs[n_src:n_src + 2]
    rest = refs[n_src + 2:]
    fg_ref = rest[0] if final_norm else None
    o_refs = rest[-n_src - 2:-2]
    buf, sems = rest[-2:]
    i = pl.program_id(0)
    n_steps = pl.num_programs(0)
    tm = x_refs[0].shape[0]

    def gather(d_ref, slot):
        for r in range(tm):
            for k in range(2):
                pltpu.make_async_copy(ys_ref.at[d_ref[0, k, r]], buf.at[slot, k, _tok_rows(r), :],
                                      sems.at[slot]).start(priority=k)

    @pl.when(i == 0)
    def _():
        gather(dest_ref, 0)

    @pl.when(i + 1 < n_steps)
    def _():
        gather(dest_next_ref, (i + 1) % 2)

    slot = i % 2
    for k in range(2):
        _wait_tokens(buf.at[slot, k], tm, sems.at[slot])
    w = w_ref[...]
    moe = (w[:, 0:1] * _load_token_major(buf.at[slot, 0], tm)
           + w[:, 1:2] * _load_token_major(buf.at[slot, 1], tm))
    y = _select_src(i, x_refs, tiles) + gate_ref[0] * moe
    if final_norm:
        y = y * lax.rsqrt(jnp.mean(y * y, axis=-1, keepdims=True) + EPS) * fg_ref[...]

    first = 0
    for o_ref, t in zip(o_refs, tiles):
        @pl.when((i >= first) & (i < first + t))
        def _(o_ref=o_ref):
            o_ref[...] = y
        first += t


def _combine(dest, wsel, ys, srcs, mod3, row_of_tile, gate_chunk, final_g, tm):
    d = srcs[0].shape[1]
    src_specs, tiles = _src_specs(srcs, tm)
    final_norm = final_g is not None
    extra_specs = [pl.BlockSpec((1, d), lambda i: (0, 0))] if final_norm else []
    extra_args = [final_g.reshape(1, d)] if final_norm else []
    n_steps = sum(tiles)
    return pl.pallas_call(
        functools.partial(_combine_kernel, tiles=tuple(tiles), final_norm=final_norm),
        grid=(n_steps,),
        in_specs=[
            pl.BlockSpec((1, 2, tm), lambda i: (i, 0, 0), memory_space=pltpu.SMEM),
            pl.BlockSpec((1, 2, tm), lambda i: (jnp.minimum(i + 1, n_steps - 1), 0, 0), memory_space=pltpu.SMEM),
            pl.BlockSpec(memory_space=pl.ANY),
        ] + src_specs + [
            pl.BlockSpec((tm, 2), lambda i: (i, 0)),
            pl.BlockSpec((1, 1, d), lambda i: (row_of_tile(i), 0, gate_chunk)),
        ] + extra_specs,
        out_specs=list(src_specs),
        out_shape=[jax.ShapeDtypeStruct(s.shape, F32) for s in srcs],
        scratch_shapes=[pltpu.VMEM((2, 2, tm * TOK_ROWS, LANES), F32), pltpu.SemaphoreType.DMA((2,))],
        compiler_params=_cparams(("arbitrary",)),
        name="moe_combine",
    )(dest, dest, ys, *srcs, wsel, mod3, *extra_args)


def _router_weights(w_group, b_group, w_expert, b_expert):
    d = w_group.shape[0]
    we = jnp.transpose(w_expert, (1, 0, 2)).reshape(d, N_EXPERTS)
    wr = jnp.concatenate([we, w_group, jnp.zeros((d, LANES - N_EXPERTS - N_GROUPS), F32)], axis=1)
    br = jnp.concatenate([b_expert.reshape(N_EXPERTS), b_group,
                          jnp.zeros((LANES - N_EXPERTS - N_GROUPS,), F32)]).reshape(1, LANES)
    return wr, br


def _moe_layer(srcs, g, mod3, row_of_tile, chunks, router_w, layer, w1, w3, w2, final_g, tm, bm):
    sh_chunk, sc_chunk, gate_chunk = chunks
    wr, br = _router_weights(*router_w)
    h, rank, eid, wsel, cnt = _router(srcs, g, mod3, row_of_tile, sh_chunk, sc_chunk, wr, br, tm)
    n = wsel.shape[0]
    start, visits = _visit_plan(cnt[0, :N_EXPERTS], 2 * n, bm)
    experts = jnp.arange(N_EXPERTS, dtype=jnp.int32)
    dest = rank + jnp.sum(jnp.where(eid[..., None] == experts, start.astype(jnp.int32), 0), axis=-1)
    xs = _scatter_rows(dest, h, tm)
    ys = _gmm(visits, xs.reshape(2 * n * TOK_ROWS, LANES), layer, w1, w3, w2, bm)
    return _combine(dest, wsel, ys.reshape(2 * n, TOK_ROWS, LANES), srcs, mod3, row_of_tile, gate_chunk,
                    final_g, tm)


def _gelu_tanh(x):
    return x * (0.5 * (1.0 + jnp.tanh(0.7978845608028654 * (x + 0.044715 * (x * x * x)))))


def _lru_in_kernel(x_ref, g_ref, sh_ref, sc_ref, w_ref, b_ref, *o_refs, with_y):
    nb, tt, d = x_ref.shape
    h = _norm_mod(x_ref[...], g_ref[...], sh_ref[...], sc_ref[...]).reshape(nb * tt, d).astype(BF16)
    lw = o_refs[-1].shape[2]
    if with_y:
        y = jnp.dot(h, w_ref[:, 0:lw], preferred_element_type=F32) + b_ref[:, 0:lw]
        o_refs[0][...] = y.reshape(nb, tt, lw).astype(o_refs[0].dtype)
    off = w_ref.shape[1] - lw
    u = jnp.dot(h, w_ref[:, off:off + lw], preferred_element_type=F32) + b_ref[:, off:off + lw]
    o_refs[-1][...] = jnp.swapaxes(u.reshape(nb, tt, lw), 0, 1)


def _lru_in_proj(x3, g, mod3, mod_rows, w, b, with_y, tt):
    nb, t, d = x3.shape
    lw = w.shape[1] // 2 if with_y else w.shape[1]
    r0, nr = mod_rows
    mod_spec = lambda chunk: pl.BlockSpec((nr, 1, d), lambda i: (r0 // nr, 0, chunk))
    out_specs = [pl.BlockSpec((tt, nb, lw), lambda i: (i, 0, 0))]
    out_shape = [jax.ShapeDtypeStruct((t, nb, lw), F32)]
    if with_y:
        out_specs.insert(0, pl.BlockSpec((nb, tt, lw), lambda i: (0, i, 0)))
        out_shape.insert(0, jax.ShapeDtypeStruct((nb, t, lw), BF16))
    return pl.pallas_call(
        functools.partial(_lru_in_kernel, with_y=with_y),
        grid=(t // tt,),
        in_specs=[
            pl.BlockSpec((nb, tt, d), lambda i: (0, i, 0)),
            pl.BlockSpec((1, d), lambda i: (0, 0)),
            mod_spec(0), mod_spec(1),
            pl.BlockSpec(w.shape, lambda i: (0, 0)),
            pl.BlockSpec((1, w.shape[1]), lambda i: (0, 0)),
        ],
        out_specs=out_specs,
        out_shape=out_shape,
        compiler_params=_cparams(("parallel",)),
        name="rglru_in_proj",
    )(x3, g.reshape(1, d), mod3, mod3, w, b.reshape(1, w.shape[1]))


def _lru_tm_kernel(u_ref, up_ref, un_ref, cw_ref, cb_ref, wg_ref, bg_ref, lam_ref, h0_ref, *rest,
                   tt, n_t, reverse, has_other):
    other_ref = rest[0] if has_other else None
    h_ref, hend_ref, a_s, b_s, carry = rest[-5:]
    i = pl.program_id(0)
    ti = (n_t - 1 - i) if reverse else i
    _, nb, w = u_ref.shape
    cb = w // LRU_BLOCKS
    left = CONV_W // 2
    right = CONV_W - 1 - left

    @pl.when(i == 0)
    def _():
        carry[...] = h0_ref[...]

    prev = jnp.where(ti > 0, up_ref[...], 0.0)
    nxt = jnp.where(ti < n_t - 1, un_ref[...], 0.0)
    for n in range(LRU_BLOCKS):
        c0 = n * cb
        ext = jnp.concatenate([prev[:, :, c0:c0 + cb], u_ref[:, :, c0:c0 + cb], nxt[:, :, c0:c0 + cb]], axis=0)
        uc = cb_ref[:, c0:c0 + cb] + cw_ref[0:1, c0:c0 + cb] * ext[0:tt]
        for kk in range(1, CONV_W):
            uc = uc + cw_ref[kk:kk + 1, c0:c0 + cb] * ext[kk:kk + tt]
        ub = uc.reshape(tt * nb, cb).astype(BF16)
        r = jax.nn.sigmoid(jnp.dot(ub, wg_ref[0, n], preferred_element_type=F32) + bg_ref[0, :, c0:c0 + cb])
        ig = jax.nn.sigmoid(jnp.dot(ub, wg_ref[1, n], preferred_element_type=F32) + bg_ref[1, :, c0:c0 + cb])
        lam = lam_ref[:, c0:c0 + cb]
        rate = (-LRU_C * LOG2_E) * jnp.log1p(jnp.exp(-lam))
        a = jnp.exp2(r * rate)
        z = 1.0 - a * a
        root = jnp.where(z > 0.0, z * lax.rsqrt(z), 0.0)
        b = root * (ig * uc.reshape(tt * nb, cb))
        a_s[:, :, c0:c0 + cb] = a.reshape(tt, nb, cb)
        b_s[:, :, c0:c0 + cb] = b.reshape(tt, nb, cb)

    def step(s, h):
        t = (tt - 1 - s) if reverse else s
        h = a_s[t] * h + b_s[t]
        h_ref[t] = (h + other_ref[t]) if has_other else h
        return h

    h_last = lax.fori_loop(0, tt, step, carry[...], unroll=8)
    carry[...] = h_last

    @pl.when(i == n_t - 1)
    def _():
        hend_ref[...] = h_last


def _lru_tm_scan(u, h0, conv_w, conv_b, wg, bg, lam, reverse, tt, other=None):
    t, nb, w = u.shape
    n_t = t // tt
    left = CONV_W // 2
    right = CONV_W - 1 - left
    assert tt % left == 0 and t % tt == 0
    tmap = (lambda i: n_t - 1 - i) if reverse else (lambda i: i)
    full = lambda a: pl.BlockSpec(a.shape, lambda i: (0,) * a.ndim)
    conv_b = conv_b.reshape(1, w)
    bg = bg.reshape(2, 1, w)
    lam = lam.reshape(1, w)
    tile = pl.BlockSpec((tt, nb, w), lambda i: (tmap(i), 0, 0))
    others = [] if other is None else [other]
    return pl.pallas_call(
        functools.partial(_lru_tm_kernel, tt=tt, n_t=n_t, reverse=reverse, has_other=other is not None),
        grid=(n_t,),
        in_specs=[
            tile,
            pl.BlockSpec((left, nb, w), lambda i: (jnp.maximum(tmap(i) * (tt // left) - 1, 0), 0, 0)),
            pl.BlockSpec((right, nb, w), lambda i: (jnp.minimum((tmap(i) + 1) * (tt // right), t // right - 1), 0, 0)),
            full(conv_w), full(conv_b), full(wg), full(bg), full(lam), full(h0),
        ] + [tile] * len(others),
        out_specs=[tile, full(h0)],
        out_shape=[jax.ShapeDtypeStruct((t, nb, w), F32), jax.ShapeDtypeStruct(h0.shape, F32)],
        scratch_shapes=[pltpu.VMEM((tt, nb, w), F32), pltpu.VMEM((tt, nb, w), F32), pltpu.VMEM(h0.shape, F32)],
        compiler_params=_cparams(("arbitrary",)),
        name="rglru_scan_rev" if reverse else "rglru_scan_fwd",
    )(u, u, u, conv_w, conv_b, wg, bg, lam, h0, *others)


def _lru_out_tm_kernel(y_ref, h_ref, w_ref, b_ref, res_ref, gate_ref, o_ref):
    nb, tt, lw = y_ref.shape
    y = jnp.swapaxes(y_ref[...].astype(F32), 0, 1)
    a = (_gelu_tanh(y) * h_ref[...]).reshape(tt * nb, lw).astype(BF16)
    z = jnp.dot(a, w_ref[...], preferred_element_type=F32) + b_ref[...]
    z = jnp.swapaxes(z.reshape(tt, nb, z.shape[1]), 0, 1)
    o_ref[...] = res_ref[...] + gate_ref[...] * z


def _lru_out_tm_proj(y, h, w, b, res3, mod3, gate_chunk, tt):
    nb, t, lw = y.shape
    d = w.shape[1]
    tm_spec = pl.BlockSpec((tt, nb, lw), lambda i: (i, 0, 0))
    return pl.pallas_call(
        _lru_out_tm_kernel,
        grid=(t // tt,),
        in_specs=[
            pl.BlockSpec((nb, tt, lw), lambda i: (0, i, 0)),
            tm_spec,
            pl.BlockSpec((lw, d), lambda i: (0, 0)),
            pl.BlockSpec((1, d), lambda i: (0, 0)),
            pl.BlockSpec((nb, tt, d), lambda i: (0, i, 0)),
            pl.BlockSpec((nb, 1, d), lambda i: (0, 0, gate_chunk)),
        ],
        out_specs=pl.BlockSpec((nb, tt, d), lambda i: (0, i, 0)),
        out_shape=jax.ShapeDtypeStruct((nb, t, d), F32),
        compiler_params=_cparams(("parallel",)),
        name="rglru_out_proj",
    )(y, h, w, b.reshape(1, d), res3, mod3)


def _lru_states(ul, uc, conv_w, conv_b, w_gate, b_gate, lam, tt):
    wg = w_gate.astype(BF16)
    _, nb, w = ul.shape
    zero = jnp.zeros((nb, w), F32)
    h = None
    for dirn in range(2):
        rev = dirn == 1
        args = (conv_w, conv_b, wg[dirn], b_gate[dirn], lam[dirn], rev, tt)
        _, h_end = _lru_tm_scan(uc, zero, *args)
        h, _ = _lru_tm_scan(ul, h_end, *args, other=h)
    return h


TM = 512
TD = 1024
BM = 512
TT = 128


def kernel(x, c, ctx, c_ctx, ada_w, ada_b, norm_g, na_w_qkv, na_b_qkv, na_rpb, na_w_o, na_b_o, lru_w_in, lru_b_in, lru_conv_w, lru_conv_b, lru_w_gate, lru_b_gate, lru_lambda, lru_w_o, lru_b_o, moe_w_group, moe_b_group, moe_w_expert, moe_b_expert, moe_w1, moe_w3, moe_w2, final_g):
    b, l, d = x.shape
    n_ctx = ctx.shape[1]
    n_l, n_c = b * l, b * n_ctx
    assert l % TM == 0 and l % TD == 0 and n_ctx <= TM and TM % n_ctx == 0 and b + 1 <= MOD_ROWS
    ctx_row = b
    tiles_per_batch = l // TM
    lat_row = lambda i: i // tiles_per_batch
    dense_row = lambda i: i // (l // TD)
    ctx_tile_row = lambda i: ctx_row
    nl_tiles = n_l // TM
    both_row = lambda i: jnp.where(i < nl_tiles, i // tiles_per_batch, ctx_row)

    cvec = jnp.concatenate([c, c_ctx[None], jnp.zeros((MOD_ROWS - b - 1, d), F32)], axis=0)
    mod = _modulation(cvec, ada_w, ada_b)
    mod3 = [mod[i].reshape(MOD_ROWS, 1, N_MOD * d) for i in range(mod.shape[0])]
    xl = x.reshape(n_l, d)
    xc = ctx.reshape(n_c, d)

    w_qkv = na_w_qkv[0].astype(BF16)
    qk_scale = (d // NA_HEADS) ** -0.5 * LOG2_E
    q, k, v = _proj(xl, norm_g[0, 0], mod3[0], dense_row, 0, 1, w_qkv, na_b_qkv[0],
                    (d, d, d), (BF16,) * 3, (qk_scale, 1.0, 1.0), TD)
    qc, kc, vc = _proj(xc, norm_g[0, 0], mod3[0], ctx_tile_row, 0, 1, w_qkv, na_b_qkv[0],
                       (d, d, d), (BF16,) * 3, (qk_scale, 1.0, 1.0), n_ctx)
    to3 = lambda a, s: a.reshape(b, s, d)
    mb = _window_bias(na_rpb[0], l // GRID_W)
    o_l = _na_attention(to3(q, l), to3(k, l), to3(v, l), to3(kc, n_ctx), to3(vc, n_ctx), mb)
    o_c = _ctx_attention(to3(qc, n_ctx), to3(kc, n_ctx), to3(vc, n_ctx))
    w_o = na_w_o[0].astype(BF16)
    xl = _resid_proj(o_l.reshape(n_l, d), w_o, na_b_o[0], xl, mod3[0], dense_row, 2, TD)
    xc = _resid_proj(o_c.reshape(n_c, d), w_o, na_b_o[0], xc, mod3[0], ctx_tile_row, 2, n_ctx)
    xl, xc = _moe_layer([xl, xc], norm_g[0, 1], mod3[0], both_row, (3, 4, 5),
                        (moe_w_group[0], moe_b_group[0], moe_w_expert[0], moe_b_expert[0]),
                        0, moe_w1, moe_w3, moe_w2, None, TM, BM)

    w_in = lru_w_in[0].astype(BF16)
    lw = w_in.shape[1] // 2
    xl3 = xl.reshape(b, l, d)
    y_l, u_l = _lru_in_proj(xl3, norm_g[1, 0], mod3[1], (0, b), w_in, lru_b_in[0], True, TT)
    (u_c,) = _lru_in_proj(xc.reshape(b, n_ctx, d), norm_g[1, 0], mod3[1], (ctx_row, 1), w_in[:, lw:],
                          lru_b_in[0, lw:], False, TT)
    h_sum = _lru_states(u_l, u_c, lru_conv_w[0], lru_conv_b[0], lru_w_gate[0], lru_b_gate[0],
                        lru_lambda[0], TT)
    xl = _lru_out_tm_proj(y_l, h_sum, lru_w_o[0].astype(BF16), lru_b_o[0], xl3, mod3[1], 2, TT)
    xl = xl.reshape(n_l, d)
    (out,) = _moe_layer([xl], norm_g[1, 1], mod3[1], lat_row, (3, 4, 5),
                        (moe_w_group[1], moe_b_group[1], moe_w_expert[1], moe_b_expert[1]),
                        1, moe_w1, moe_w3, moe_w2, final_g, TM, BM)
    return out.reshape(b, l, d)
```

```python
import functools

import jax
import jax.numpy as jnp
import numpy as np
from jax import lax
from jax.experimental import pallas as pl
from jax.experimental.pallas import tpu as pltpu

F32 = jnp.float32
BF16 = jnp.bfloat16

GRID_W = 64
N_MOD = 6
NA_HEADS = 16
WIN_H = 8
WIN_W = 16
LRU_BLOCKS = 4
CONV_W = 4
LRU_C = 8.0
N_GROUPS = 4
EXPERTS_PER_GROUP = 8
N_EXPERTS = N_GROUPS * EXPERTS_PER_GROUP
EPS = 1e-6

LANES = 128
MOD_ROWS = 16
NEG = -1e30
VMEM_LIMIT = 56 * 1024 * 1024
HIGHEST = lax.Precision.HIGHEST
LOG2_E = 1.4426950408889634


def _cparams(sem, vmem=VMEM_LIMIT):
    return pltpu.CompilerParams(dimension_semantics=sem, vmem_limit_bytes=vmem)


def _mod_kernel(c_ref, w_ref, b_ref, o_ref):
    c = c_ref[...]
    s = c * jax.nn.sigmoid(c)
    o_ref[0] = jnp.dot(s, w_ref[0], precision=HIGHEST, preferred_element_type=F32) + b_ref[0]


def _modulation(cvec, ada_w, ada_b):
    depth, d, n = ada_w.shape
    tn = 1536
    return pl.pallas_call(
        _mod_kernel,
        grid=(depth, n // tn),
        in_specs=[
            pl.BlockSpec((MOD_ROWS, d), lambda l, j: (0, 0)),
            pl.BlockSpec((1, d, tn), lambda l, j: (l, 0, j)),
            pl.BlockSpec((1, 1, tn), lambda l, j: (l, 0, j)),
        ],
        out_specs=pl.BlockSpec((1, MOD_ROWS, tn), lambda l, j: (l, 0, j)),
        out_shape=jax.ShapeDtypeStruct((depth, MOD_ROWS, n), F32),
        compiler_params=_cparams(("arbitrary", "arbitrary")),
        name="adaln_mod",
    )(cvec, ada_w, ada_b.reshape(depth, 1, n))


def _norm_mod(x, g, sh, sc):
    ms = jnp.mean(x * x, axis=-1, keepdims=True)
    y = x * lax.rsqrt(ms + EPS) * g
    return y * (1.0 + sc) + sh


def _proj_kernel(x_ref, g_ref, sh_ref, sc_ref, w_ref, b_ref, *o_refs, splits, scales):
    h = _norm_mod(x_ref[...], g_ref[...], sh_ref[0], sc_ref[0]).astype(BF16)
    off = 0
    for o_ref, n, s in zip(o_refs, splits, scales):
        y = jnp.dot(h, w_ref[:, off:off + n], preferred_element_type=F32) + b_ref[:, off:off + n]
        if s != 1.0:
            y = y * s
        o_ref[...] = y.astype(o_ref.dtype)
        off += n


def _proj(x2d, g, mod3, row_of_tile, sh_chunk, sc_chunk, w, b, splits, dtypes, scales, tm):
    n_tok, d = x2d.shape
    n_out = w.shape[1]
    assert sum(splits) == n_out and n_tok % tm == 0
    return pl.pallas_call(
        functools.partial(_proj_kernel, splits=tuple(splits), scales=tuple(scales)),
        grid=(n_tok // tm,),
        in_specs=[
            pl.BlockSpec((tm, d), lambda i: (i, 0)),
            pl.BlockSpec((1, d), lambda i: (0, 0)),
            pl.BlockSpec((1, 1, d), lambda i: (row_of_tile(i), 0, sh_chunk)),
            pl.BlockSpec((1, 1, d), lambda i: (row_of_tile(i), 0, sc_chunk)),
            pl.BlockSpec((d, n_out), lambda i: (0, 0)),
            pl.BlockSpec((1, n_out), lambda i: (0, 0)),
        ],
        out_specs=[pl.BlockSpec((tm, n), lambda i: (i, 0)) for n in splits],
        out_shape=[jax.ShapeDtypeStruct((n_tok, n), dt) for n, dt in zip(splits, dtypes)],
        compiler_params=_cparams(("parallel",)),
        name="norm_mod_proj",
    )(x2d, g.reshape(1, d), mod3, mod3, w, b.reshape(1, n_out))


def _resid_kernel(a_ref, w_ref, b_ref, res_ref, gate_ref, o_ref):
    y = jnp.dot(a_ref[...], w_ref[...], preferred_element_type=F32) + b_ref[...]
    o_ref[...] = res_ref[...] + gate_ref[0] * y


def _resid_proj(a, w, b, res, mod3, row_of_tile, gate_chunk, tm):
    n_tok, k = a.shape
    d = w.shape[1]
    return pl.pallas_call(
        _resid_kernel,
        grid=(n_tok // tm,),
        in_specs=[
            pl.BlockSpec((tm, k), lambda i: (i, 0)),
            pl.BlockSpec((k, d), lambda i: (0, 0)),
            pl.BlockSpec((1, d), lambda i: (0, 0)),
            pl.BlockSpec((tm, d), lambda i: (i, 0)),
            pl.BlockSpec((1, 1, d), lambda i: (row_of_tile(i), 0, gate_chunk)),
        ],
        out_specs=pl.BlockSpec((tm, d), lambda i: (i, 0)),
        out_shape=jax.ShapeDtypeStruct((n_tok, d), F32),
        compiler_params=_cparams(("parallel",)),
        name="proj_residual",
    )(a, w, b.reshape(1, d), res, mod3)


HEAD_GROUP = 4


def _head_group_attention(qg, keys, vals, biases):
    n_q, width = qg.shape
    dh = width // HEAD_GROUP
    head_of_lane = lax.broadcasted_iota(jnp.int32, (n_q, width), 1) // dh
    qs = jnp.concatenate([jnp.where(head_of_lane == h, qg, jnp.zeros_like(qg)) for h in range(HEAD_GROUP)],
                         axis=0)
    s_parts = []
    for kk, bb in zip(keys, biases):
        s = lax.dot_general(qs, kk, (((1,), (1,)), ((), ())), preferred_element_type=F32)
        s_parts.append(s if bb is None else s + bb)
    m = s_parts[0].max(axis=-1, keepdims=True)
    for s in s_parts[1:]:
        m = jnp.maximum(m, s.max(axis=-1, keepdims=True))
    p_parts = [jnp.exp2(s - m) for s in s_parts]
    l = p_parts[0].sum(axis=-1, keepdims=True)
    for p in p_parts[1:]:
        l = l + p.sum(axis=-1, keepdims=True)
    o = None
    for p, vv in zip(p_parts, vals):
        t = jnp.dot(p.astype(BF16), vv, preferred_element_type=F32)
        o = t if o is None else o + t
    o = o / l
    out = jnp.where(head_of_lane == 0, o[0:n_q], 0.0)
    for h in range(1, HEAD_GROUP):
        out = jnp.where(head_of_lane == h, o[h * n_q:(h + 1) * n_q], out)
    return out


NA_ROWS_PER_STEP = 4
NA_VMEM_LIMIT = 58 * 1024 * 1024


def _na_kernel(q_ref, k_ref, v_ref, kc_ref, vc_ref, *rest, rows, kh):
    mb_refs, o_ref = rest[:-1], rest[-1]
    rps = len(mb_refs)
    n_win = kh * GRID_W
    width = HEAD_GROUP * (q_ref.shape[2] // NA_HEADS)

    for j, mb_ref in enumerate(mb_refs):
        r = pl.program_id(1) * rps + j
        rs = jnp.clip(r - kh // 2, 0, rows - kh)
        k0 = pl.multiple_of(rs * GRID_W, GRID_W)
        q0 = j * GRID_W
        for g in range(NA_HEADS // HEAD_GROUP):
            c0 = g * width
            qg = q_ref[0, q0:q0 + GRID_W, c0:c0 + width]
            kw = k_ref[0, pl.ds(k0, n_win), c0:c0 + width]
            vw = v_ref[0, pl.ds(k0, n_win), c0:c0 + width]
            kc = kc_ref[0, :, c0:c0 + width]
            vc = vc_ref[0, :, c0:c0 + width]
            bias = mb_ref[0, g * HEAD_GROUP:(g + 1) * HEAD_GROUP].reshape(HEAD_GROUP * GRID_W, n_win)
            o = _head_group_attention(qg, [kw, kc], [vw, vc], [bias, None])
            o_ref[0, q0:q0 + GRID_W, c0:c0 + width] = o.astype(o_ref.dtype)


def _window_bias(rpb, rows):
    kh = min(WIN_H, rows)
    cols = np.arange(GRID_W)
    col_start = np.clip(cols - WIN_W // 2, 0, GRID_W - WIN_W)
    ck = np.arange(GRID_W)
    in_win = (ck[None, :] >= col_start[:, None]) & (ck[None, :] < col_start[:, None] + WIN_W)
    dc = ck[None, :] - cols[:, None] + (WIN_W - 1)
    n_dc = 2 * WIN_W - 1
    sel = ((dc[None] == np.arange(n_dc)[:, None, None]) & in_win[None]).astype(np.float32)
    t = jnp.einsum("hrc,cqk->hrqk", rpb.astype(F32), jnp.asarray(sel), precision=HIGHEST)
    t = jnp.where(jnp.asarray(in_win)[None, None], t, NEG)
    per_delta = []
    for delta in range(kh):
        r0 = WIN_H - 1 - delta
        per_delta.append(jnp.transpose(t[:, r0:r0 + kh], (0, 2, 1, 3)).reshape(NA_HEADS, GRID_W, kh * GRID_W))
    return jnp.stack(per_delta) * LOG2_E


def _na_attention(q, k, v, kc, vc, mb):
    b, l, d = q.shape
    c = kc.shape[1]
    rows = l // GRID_W
    kh = mb.shape[0]

    rps = NA_ROWS_PER_STEP
    assert rows % rps == 0

    def bias_spec(j):
        def index(bi, s):
            r = s * rps + j
            return (r - jnp.clip(r - kh // 2, 0, rows - kh), 0, 0, 0)
        return pl.BlockSpec((1, NA_HEADS, GRID_W, kh * GRID_W), index)

    return pl.pallas_call(
        functools.partial(_na_kernel, rows=rows, kh=kh),
        grid=(b, rows // rps),
        in_specs=[
            pl.BlockSpec((1, rps * GRID_W, d), lambda bi, s: (bi, s, 0)),
            pl.BlockSpec((1, l, d), lambda bi, s: (bi, 0, 0)),
            pl.BlockSpec((1, l, d), lambda bi, s: (bi, 0, 0)),
            pl.BlockSpec((1, c, d), lambda bi, s: (bi, 0, 0), pipeline_mode=pl.Buffered(1)),
            pl.BlockSpec((1, c, d), lambda bi, s: (bi, 0, 0), pipeline_mode=pl.Buffered(1)),
        ] + [bias_spec(j) for j in range(rps)],
        out_specs=pl.BlockSpec((1, rps * GRID_W, d), lambda bi, s: (bi, s, 0)),
        out_shape=jax.ShapeDtypeStruct((b, l, d), BF16),
        compiler_params=_cparams(("parallel", "arbitrary"), NA_VMEM_LIMIT),
        name="na_attention",
    )(q, k, v, kc, vc, *([mb] * rps))


def _ctx_attn_kernel(q_ref, k_ref, v_ref, o_ref):
    width = HEAD_GROUP * (q_ref.shape[2] // NA_HEADS)

    def group(g, carry):
        c0 = pl.multiple_of(g * width, width)
        o = _head_group_attention(q_ref[0, :, pl.ds(c0, width)], [k_ref[0, :, pl.ds(c0, width)]],
                                  [v_ref[0, :, pl.ds(c0, width)]], [None])
        o_ref[0, :, pl.ds(c0, width)] = o.astype(o_ref.dtype)
        return carry

    lax.fori_loop(0, NA_HEADS // HEAD_GROUP, group, 0)


def _ctx_attention(qc, kc, vc):
    b, c, d = qc.shape
    spec = pl.BlockSpec((1, c, d), lambda bi: (bi, 0, 0))
    return pl.pallas_call(
        _ctx_attn_kernel,
        grid=(b,),
        in_specs=[spec, spec, spec],
        out_specs=spec,
        out_shape=jax.ShapeDtypeStruct((b, c, d), BF16),
        compiler_params=_cparams(("parallel",)),
        name="ctx_attention",
    )(qc, kc, vc)


GROUP_LANE0 = N_EXPERTS


def _src_specs(srcs, tm):
    d = srcs[0].shape[1]
    tiles = [s.shape[0] // tm for s in srcs]
    specs, first = [], 0
    for t in tiles:
        specs.append(pl.BlockSpec((tm, d), functools.partial(
            lambda i, first, t: (jnp.clip(i - first, 0, t - 1), 0), first=first, t=t)))
        first += t
    return specs, tiles


def _select_src(i, refs, tiles):
    x = refs[-1][...]
    first = sum(tiles[:-1])
    for ref, t in zip(refs[-2::-1], tiles[-2::-1]):
        x = jnp.where(i < first, ref[...], x)
        first -= t
    return x


TOK_ROWS = 8


def _store_token_major(ref, x, first=0):
    n, d = x.shape
    assert d == TOK_ROWS * LANES
    for j in range(TOK_ROWS):
        ref[pl.ds(first * TOK_ROWS + j, n, stride=TOK_ROWS), :] = x[:, j * LANES:(j + 1) * LANES]


def _load_token_major(ref, n, first=0):
    return jnp.concatenate([ref[pl.ds(first * TOK_ROWS + j, n, stride=TOK_ROWS), :] for j in range(TOK_ROWS)],
                           axis=1)


def _router_kernel(*refs, tiles):
    n_src = len(tiles)
    x_refs = refs[:n_src]
    g_ref, sh_ref, sc_ref, wr_ref, br_ref, h_ref, rank_ref, eid_ref, wsel_ref, cnt_ref = refs[n_src:]
    i = pl.program_id(0)
    h = _norm_mod(_select_src(i, x_refs, tiles), g_ref[...], sh_ref[0], sc_ref[0])
    _store_token_major(h_ref, h)
    h_hi = h.astype(BF16)
    h_lo = (h - h_hi.astype(F32)).astype(BF16)
    wr = wr_ref[...]
    w_hi = wr.astype(BF16)
    w_lo = (wr - w_hi.astype(F32)).astype(BF16)
    logits = (jnp.dot(h_hi, w_hi, preferred_element_type=F32)
              + jnp.dot(h_lo, w_hi, preferred_element_type=F32)
              + jnp.dot(h_hi, w_lo, preferred_element_type=F32)) + br_ref[...]
    tm = logits.shape[0]
    lane = lax.broadcasted_iota(jnp.int32, (tm, LANES), 1)
    big = jnp.int32(LANES)

    gmask = (lane >= GROUP_LANE0) & (lane < GROUP_LANE0 + N_GROUPS)
    gl = jnp.where(gmask, logits, NEG)
    gmax = gl.max(axis=-1, keepdims=True)
    gsel = jnp.where(gmask & (gl == gmax), lane, big).min(axis=-1, keepdims=True) - GROUP_LANE0
    g_w = 1.0 / jnp.where(gmask, jnp.exp(gl - gmax), 0.0).sum(axis=-1, keepdims=True)

    e0 = gsel * EXPERTS_PER_GROUP
    emask = (lane >= e0) & (lane < e0 + EXPERTS_PER_GROUP)
    el = jnp.where(emask, logits, NEG)
    v1 = el.max(axis=-1, keepdims=True)
    i1 = jnp.where(emask & (el == v1), lane, big).min(axis=-1, keepdims=True)
    el2 = jnp.where(lane == i1, NEG, el)
    v2 = el2.max(axis=-1, keepdims=True)
    i2 = jnp.where(emask & (lane != i1) & (el2 == v2), lane, big).min(axis=-1, keepdims=True)
    t = jnp.exp(v2 - v1)
    w1 = g_w / (1.0 + t)
    w2 = g_w * t / (1.0 + t)

    @pl.when(i == 0)
    def _():
        cnt_ref[...] = jnp.zeros_like(cnt_ref)

    oh = jnp.where((lane == i1) | (lane == i2), 1.0, 0.0)
    row = lax.broadcasted_iota(jnp.int32, (tm, tm), 0)
    col = lax.broadcasted_iota(jnp.int32, (tm, tm), 1)
    tri = jnp.where(row > col, 1.0, 0.0).astype(BF16)
    before = jnp.dot(tri, oh.astype(BF16), preferred_element_type=F32) + cnt_ref[...]
    first_is_low = i1 < i2
    ea = jnp.where(first_is_low, i1, i2)
    eb = jnp.where(first_is_low, i2, i1)
    pick = lambda e: jnp.where(lane == e, before, 0.0).sum(axis=-1, keepdims=True)
    cols = jnp.where(lane == 0, pick(ea), jnp.where(lane == 1, pick(eb), jnp.where(
        lane == 2, ea.astype(F32), jnp.where(lane == 3, eb.astype(F32), 0.0))))
    cols_t = cols.T
    rank_ref[0] = cols_t[0:2, :].astype(jnp.int32)
    eid_ref[0] = cols_t[2:4, :].astype(jnp.int32)
    two = lax.broadcasted_iota(jnp.int32, (tm, 2), 1) == 0
    wsel_ref[...] = jnp.where(two, jnp.where(first_is_low, w1, w2), jnp.where(first_is_low, w2, w1))
    cnt_ref[...] += oh.sum(axis=0, keepdims=True)


def _router(srcs, g, mod3, row_of_tile, sh_chunk, sc_chunk, wr, br, tm):
    d = srcs[0].shape[1]
    src_specs, tiles = _src_specs(srcs, tm)
    n = sum(tiles) * tm
    slots = pl.BlockSpec((1, 2, tm), lambda i: (i, 0, 0))
    slot_shape = jax.ShapeDtypeStruct((n // tm, 2, tm), jnp.int32)
    return pl.pallas_call(
        functools.partial(_router_kernel, tiles=tuple(tiles)),
        grid=(sum(tiles),),
        in_specs=src_specs + [
            pl.BlockSpec((1, d), lambda i: (0, 0)),
            pl.BlockSpec((1, 1, d), lambda i: (row_of_tile(i), 0, sh_chunk)),
            pl.BlockSpec((1, 1, d), lambda i: (row_of_tile(i), 0, sc_chunk)),
            pl.BlockSpec((d, LANES), lambda i: (0, 0)),
            pl.BlockSpec((1, LANES), lambda i: (0, 0)),
        ],
        out_specs=[pl.BlockSpec((tm * TOK_ROWS, LANES), lambda i: (i, 0)), slots, slots,
                   pl.BlockSpec((tm, 2), lambda i: (i, 0)), pl.BlockSpec((1, LANES), lambda i: (0, 0))],
        out_shape=[jax.ShapeDtypeStruct((n * TOK_ROWS, LANES), F32), slot_shape, slot_shape,
                   jax.ShapeDtypeStruct((n, 2), F32), jax.ShapeDtypeStruct((1, LANES), F32)],
        compiler_params=_cparams(("arbitrary",)),
        name="moe_router",
    )(*srcs, g.reshape(1, d), mod3, mod3, wr, br)


def _tok_rows(t):
    return pl.ds(t * TOK_ROWS, TOK_ROWS)


def _wait_tokens(buf, n, sem):
    whole = buf.at[pl.ds(0, n * TOK_ROWS), :]
    pltpu.make_async_copy(whole, whole, sem).wait()


def _scatter_kernel(dest_ref, h_ref, xs_ref, sem):
    tm = h_ref.shape[0] // TOK_ROWS
    for r in range(tm):
        for k in range(2):
            pltpu.make_async_copy(h_ref.at[_tok_rows(r), :], xs_ref.at[dest_ref[0, k, r]], sem).start(priority=k)
    for k in range(2):
        _wait_tokens(h_ref, tm, sem)


def _scatter_rows(dest, h, tm):
    n = h.shape[0] // TOK_ROWS
    return pl.pallas_call(
        _scatter_kernel,
        grid=(n // tm,),
        in_specs=[
            pl.BlockSpec((1, 2, tm), lambda i: (i, 0, 0), memory_space=pltpu.SMEM),
            pl.BlockSpec((tm * TOK_ROWS, LANES), lambda i: (i, 0)),
        ],
        out_specs=pl.BlockSpec(memory_space=pl.ANY),
        out_shape=jax.ShapeDtypeStruct((2 * n, TOK_ROWS, LANES), F32),
        scratch_shapes=[pltpu.SemaphoreType.DMA(())],
        compiler_params=_cparams(("arbitrary",)),
        name="moe_scatter",
    )(dest, h)


def _gmm_kernel(vb_ref, ve_ref, lo_ref, hi_ref, x_ref, w1_ref, w3_ref, w2_ref, o_ref, w1b, w3b, w2b, *, bm):
    v = pl.program_id(0)
    pv = jnp.maximum(v - 1, 0)

    @pl.when((v == 0) | (ve_ref[v] != ve_ref[pv]))
    def _():
        w1b[...] = w1_ref[0, 0].astype(BF16)
        w3b[...] = w3_ref[0, 0].astype(BF16)
        w2b[...] = w2_ref[0, 0].astype(BF16)

    lo = lo_ref[v]
    hi = hi_ref[v]

    first = (v == 0) | (vb_ref[v] != vb_ref[pv])
    half = bm // 2

    def experts_on(r0, n):
        xb = _load_token_major(x_ref, n, r0).astype(BF16)
        a = jnp.dot(xb, w1b[...], preferred_element_type=F32)
        g = jnp.dot(xb, w3b[...], preferred_element_type=F32)
        hdn = (a * jax.nn.sigmoid(a) * g).astype(BF16)
        y = jnp.dot(hdn, w2b[...], preferred_element_type=F32)
        rows = r0 + lax.broadcasted_iota(jnp.int32, (n, 1), 0)
        mine = (rows >= lo) & (rows < hi)

        @pl.when(first)
        def _():
            _store_token_major(o_ref, jnp.where(mine, y, 0.0), r0)

        @pl.when(jnp.logical_not(first))
        def _():
            _store_token_major(o_ref, jnp.where(mine, y, _load_token_major(o_ref, n, r0)), r0)

    def zero_on_first(r0, n):
        @pl.when(first)
        def _():
            _store_token_major(o_ref, jnp.zeros((n, TOK_ROWS * LANES), F32), r0)

    @pl.when((hi > lo) & (lo < half) & (hi > half))
    def _():
        experts_on(0, bm)

    @pl.when((hi > lo) & (hi <= half))
    def _():
        experts_on(0, half)
        zero_on_first(half, half)

    @pl.when((hi > lo) & (lo >= half))
    def _():
        zero_on_first(0, half)
        experts_on(half, half)


def _gmm(visits, xs, layer, w1, w3, w2, bm):
    d, f = w1.shape[2:]
    n_vis = visits[0].shape[0]
    blk = pl.BlockSpec((bm * TOK_ROWS, LANES), lambda v, vb, ve, lo, hi: (vb[v], 0))
    grid_spec = pltpu.PrefetchScalarGridSpec(
        num_scalar_prefetch=4,
        grid=(n_vis,),
        in_specs=[
            blk,
            pl.BlockSpec((1, 1, d, f), lambda v, vb, ve, lo, hi: (layer, ve[v], 0, 0)),
            pl.BlockSpec((1, 1, d, f), lambda v, vb, ve, lo, hi: (layer, ve[v], 0, 0)),
            pl.BlockSpec((1, 1, f, d), lambda v, vb, ve, lo, hi: (layer, ve[v], 0, 0)),
        ],
        out_specs=blk,
        scratch_shapes=[pltpu.VMEM((d, f), BF16), pltpu.VMEM((d, f), BF16), pltpu.VMEM((f, d), BF16)],
    )
    return pl.pallas_call(
        functools.partial(_gmm_kernel, bm=bm),
        grid_spec=grid_spec,
        out_shape=jax.ShapeDtypeStruct(xs.shape, F32),
        compiler_params=_cparams(("arbitrary",)),
        name="moe_experts",
    )(*visits, xs, w1, w3, w2)


def _visit_plan(counts, n_rows, bm):
    counts = counts.astype(jnp.int32)
    end = jnp.cumsum(counts)
    start = end - counts
    n_blocks = n_rows // bm
    n_vis = n_blocks + N_EXPERTS
    tiles = jnp.where(counts > 0, (end - 1) // bm - start // bm + 1, 0)
    vend = jnp.cumsum(tiles)
    vstart = vend - tiles
    v = jnp.arange(n_vis, dtype=jnp.int32)[:, None]
    owns = (vstart[None, :] <= v) & (v < vend[None, :])
    take = lambda a: jnp.sum(jnp.where(owns, a[None, :], 0), axis=1)
    valid = v[:, 0] < vend[-1]
    last_e = jnp.max(jnp.where(counts > 0, jnp.arange(N_EXPERTS, dtype=jnp.int32), 0))
    e = jnp.where(valid, take(jnp.arange(N_EXPERTS, dtype=jnp.int32)), last_e)
    blk = jnp.where(valid, take(start // bm - vstart) + v[:, 0], n_blocks - 1)
    lo = jnp.where(valid, jnp.clip(take(start) - blk * bm, 0, bm), 0)
    hi = jnp.where(valid, jnp.clip(take(end) - blk * bm, 0, bm), 0)
    return start, (blk.astype(jnp.int32), e.astype(jnp.int32), lo.astype(jnp.int32), hi.astype(jnp.int32))


def _combine_kernel(dest_ref, dest_next_ref, ys_ref, *refs, tiles, final_norm):
    n_src = len(tiles)
    x_refs = refs[:n_src]
    w_ref, gate_ref = refs[n_src:n_src + 2]
    rest = refs[n_src + 2:]
    fg_ref = rest[0] if final_norm else None
    o_refs = rest[-n_src - 2:-2]
    buf, sems = rest[-2:]
    i = pl.program_id(0)
    n_steps = pl.num_programs(0)
    tm = x_refs[0].shape[0]

    def gather(d_ref, slot):
        for r in range(tm):
            for k in range(2):
                pltpu.make_async_copy(ys_ref.at[d_ref[0, k, r]], buf.at[slot, k, _tok_rows(r), :],
                                      sems.at[slot]).start(priority=k)

    @pl.when(i == 0)
    def _():
        gather(dest_ref, 0)

    @pl.when(i + 1 < n_steps)
    def _():
        gather(dest_next_ref, (i + 1) % 2)

    slot = i % 2
    for k in range(2):
        _wait_tokens(buf.at[slot, k], tm, sems.at[slot])
    w = w_ref[...]
    moe = (w[:, 0:1] * _load_token_major(buf.at[slot, 0], tm)
           + w[:, 1:2] * _load_token_major(buf.at[slot, 1], tm))
    y = _select_src(i, x_refs, tiles) + gate_ref[0] * moe
    if final_norm:
        y = y * lax.rsqrt(jnp.mean(y * y, axis=-1, keepdims=True) + EPS) * fg_ref[...]

    first = 0
    for o_ref, t in zip(o_refs, tiles):
        @pl.when((i >= first) & (i < first + t))
        def _(o_ref=o_ref):
            o_ref[...] = y
        first += t


def _combine(dest, wsel, ys, srcs, mod3, row_of_tile, gate_chunk, final_g, tm):
    d = srcs[0].shape[1]
    src_specs, tiles = _src_specs(srcs, tm)
    final_norm = final_g is not None
    extra_specs = [pl.BlockSpec((1, d), lambda i: (0, 0))] if final_norm else []
    extra_args = [final_g.reshape(1, d)] if final_norm else []
    n_steps = sum(tiles)
    return pl.pallas_call(
        functools.partial(_combine_kernel, tiles=tuple(tiles), final_norm=final_norm),
        grid=(n_steps,),
        in_specs=[
            pl.BlockSpec((1, 2, tm), lambda i: (i, 0, 0), memory_space=pltpu.SMEM),
            pl.BlockSpec((1, 2, tm), lambda i: (jnp.minimum(i + 1, n_steps - 1), 0, 0), memory_space=pltpu.SMEM),
            pl.BlockSpec(memory_space=pl.ANY),
        ] + src_specs + [
            pl.BlockSpec((tm, 2), lambda i: (i, 0)),
            pl.BlockSpec((1, 1, d), lambda i: (row_of_tile(i), 0, gate_chunk)),
        ] + extra_specs,
        out_specs=list(src_specs),
        out_shape=[jax.ShapeDtypeStruct(s.shape, F32) for s in srcs],
        scratch_shapes=[pltpu.VMEM((2, 2, tm * TOK_ROWS, LANES), F32), pltpu.SemaphoreType.DMA((2,))],
        compiler_params=_cparams(("arbitrary",)),
        name="moe_combine",
    )(dest, dest, ys, *srcs, wsel, mod3, *extra_args)


def _router_weights(w_group, b_group, w_expert, b_expert):
    d = w_group.shape[0]
    we = jnp.transpose(w_expert, (1, 0, 2)).reshape(d, N_EXPERTS)
    wr = jnp.concatenate([we, w_group, jnp.zeros((d, LANES - N_EXPERTS - N_GROUPS), F32)], axis=1)
    br = jnp.concatenate([b_expert.reshape(N_EXPERTS), b_group,
                          jnp.zeros((LANES - N_EXPERTS - N_GROUPS,), F32)]).reshape(1, LANES)
    return wr, br


def _moe_layer(srcs, g, mod3, row_of_tile, chunks, router_w, layer, w1, w3, w2, final_g, tm, bm):
    sh_chunk, sc_chunk, gate_chunk = chunks
    wr, br = _router_weights(*router_w)
    h, rank, eid, wsel, cnt = _router(srcs, g, mod3, row_of_tile, sh_chunk, sc_chunk, wr, br, tm)
    n = wsel.shape[0]
    start, visits = _visit_plan(cnt[0, :N_EXPERTS], 2 * n, bm)
    experts = jnp.arange(N_EXPERTS, dtype=jnp.int32)
    dest = rank + jnp.sum(jnp.where(eid[..., None] == experts, start.astype(jnp.int32), 0), axis=-1)
    xs = _scatter_rows(dest, h, tm)
    ys = _gmm(visits, xs.reshape(2 * n * TOK_ROWS, LANES), layer, w1, w3, w2, bm)
    return _combine(dest, wsel, ys.reshape(2 * n, TOK_ROWS, LANES), srcs, mod3, row_of_tile, gate_chunk,
                    final_g, tm)


def _gelu_tanh(x):
    return x * (0.5 * (1.0 + jnp.tanh(0.7978845608028654 * (x + 0.044715 * (x * x * x)))))


def _lru_in_kernel(x_ref, g_ref, sh_ref, sc_ref, w_ref, b_ref, *o_refs, with_y):
    nb, tt, d = x_ref.shape
    h = _norm_mod(x_ref[...], g_ref[...], sh_ref[...], sc_ref[...]).reshape(nb * tt, d).astype(BF16)
    lw = o_refs[-1].shape[2]
    if with_y:
        y = jnp.dot(h, w_ref[:, 0:lw], preferred_element_type=F32) + b_ref[:, 0:lw]
        o_refs[0][...] = y.reshape(nb, tt, lw).astype(o_refs[0].dtype)
    off = w_ref.shape[1] - lw
    u = jnp.dot(h, w_ref[:, off:off + lw], preferred_element_type=F32) + b_ref[:, off:off + lw]
    o_refs[-1][...] = jnp.swapaxes(u.reshape(nb, tt, lw), 0, 1)


def _lru_in_proj(x3, g, mod3, mod_rows, w, b, with_y, tt):
    nb, t, d = x3.shape
    lw = w.shape[1] // 2 if with_y else w.shape[1]
    r0, nr = mod_rows
    mod_spec = lambda chunk: pl.BlockSpec((nr, 1, d), lambda i: (r0 // nr, 0, chunk))
    out_specs = [pl.BlockSpec((tt, nb, lw), lambda i: (i, 0, 0))]
    out_shape = [jax.ShapeDtypeStruct((t, nb, lw), F32)]
    if with_y:
        out_specs.insert(0, pl.BlockSpec((nb, tt, lw), lambda i: (0, i, 0)))
        out_shape.insert(0, jax.ShapeDtypeStruct((nb, t, lw), BF16))
    return pl.pallas_call(
        functools.partial(_lru_in_kernel, with_y=with_y),
        grid=(t // tt,),
        in_specs=[
            pl.BlockSpec((nb, tt, d), lambda i: (0, i, 0)),
            pl.BlockSpec((1, d), lambda i: (0, 0)),
            mod_spec(0), mod_spec(1),
            pl.BlockSpec(w.shape, lambda i: (0, 0)),
            pl.BlockSpec((1, w.shape[1]), lambda i: (0, 0)),
        ],
        out_specs=out_specs,
        out_shape=out_shape,
        compiler_params=_cparams(("parallel",)),
        name="rglru_in_proj",
    )(x3, g.reshape(1, d), mod3, mod3, w, b.reshape(1, w.shape[1]))


def _lru_tm_kernel(u_ref, up_ref, un_ref, cw_ref, cb_ref, wg_ref, bg_ref, lam_ref, h0_ref, *rest,
                   tt, n_t, reverse, has_other):
    other_ref = rest[0] if has_other else None
    h_ref, hend_ref, a_s, b_s, carry = rest[-5:]
    i = pl.program_id(0)
    ti = (n_t - 1 - i) if reverse else i
    _, nb, w = u_ref.shape
    cb = w // LRU_BLOCKS
    left = CONV_W // 2
    right = CONV_W - 1 - left

    @pl.when(i == 0)
    def _():
        carry[...] = h0_ref[...]

    prev = jnp.where(ti > 0, up_ref[...], 0.0)
    nxt = jnp.where(ti < n_t - 1, un_ref[...], 0.0)
    for n in range(LRU_BLOCKS):
        c0 = n * cb
        ext = jnp.concatenate([prev[:, :, c0:c0 + cb], u_ref[:, :, c0:c0 + cb], nxt[:, :, c0:c0 + cb]], axis=0)
        uc = cb_ref[:, c0:c0 + cb] + cw_ref[0:1, c0:c0 + cb] * ext[0:tt]
        for kk in range(1, CONV_W):
            uc = uc + cw_ref[kk:kk + 1, c0:c0 + cb] * ext[kk:kk + tt]
        ub = uc.reshape(tt * nb, cb).astype(BF16)
        r = jax.nn.sigmoid(jnp.dot(ub, wg_ref[0, n], preferred_element_type=F32) + bg_ref[0, :, c0:c0 + cb])
        ig = jax.nn.sigmoid(jnp.dot(ub, wg_ref[1, n], preferred_element_type=F32) + bg_ref[1, :, c0:c0 + cb])
        lam = lam_ref[:, c0:c0 + cb]
        rate = (-LRU_C * LOG2_E) * jnp.log1p(jnp.exp(-lam))
        a = jnp.exp2(r * rate)
        z = 1.0 - a * a
        root = jnp.where(z > 0.0, z * lax.rsqrt(z), 0.0)
        b = root * (ig * uc.reshape(tt * nb, cb))
        a_s[:, :, c0:c0 + cb] = a.reshape(tt, nb, cb)
        b_s[:, :, c0:c0 + cb] = b.reshape(tt, nb, cb)

    def step(s, h):
        t = (tt - 1 - s) if reverse else s
        h = a_s[t] * h + b_s[t]
        h_ref[t] = (h + other_ref[t]) if has_other else h
        return h

    h_last = lax.fori_loop(0, tt, step, carry[...], unroll=8)
    carry[...] = h_last

    @pl.when(i == n_t - 1)
    def _():
        hend_ref[...] = h_last


def _lru_tm_scan(u, h0, conv_w, conv_b, wg, bg, lam, reverse, tt, other=None):
    t, nb, w = u.shape
    n_t = t // tt
    left = CONV_W // 2
    right = CONV_W - 1 - left
    assert tt % left == 0 and t % tt == 0
    tmap = (lambda i: n_t - 1 - i) if reverse else (lambda i: i)
    full = lambda a: pl.BlockSpec(a.shape, lambda i: (0,) * a.ndim)
    conv_b = conv_b.reshape(1, w)
    bg = bg.reshape(2, 1, w)
    lam = lam.reshape(1, w)
    tile = pl.BlockSpec((tt, nb, w), lambda i: (tmap(i), 0, 0))
    others = [] if other is None else [other]
    return pl.pallas_call(
        functools.partial(_lru_tm_kernel, tt=tt, n_t=n_t, reverse=reverse, has_other=other is not None),
        grid=(n_t,),
        in_specs=[
            tile,
            pl.BlockSpec((left, nb, w), lambda i: (jnp.maximum(tmap(i) * (tt // left) - 1, 0), 0, 0)),
            pl.BlockSpec((right, nb, w), lambda i: (jnp.minimum((tmap(i) + 1) * (tt // right), t // right - 1), 0, 0)),
            full(conv_w), full(conv_b), full(wg), full(bg), full(lam), full(h0),
        ] + [tile] * len(others),
        out_specs=[tile, full(h0)],
        out_shape=[jax.ShapeDtypeStruct((t, nb, w), F32), jax.ShapeDtypeStruct(h0.shape, F32)],
        scratch_shapes=[pltpu.VMEM((tt, nb, w), F32), pltpu.VMEM((tt, nb, w), F32), pltpu.VMEM(h0.shape, F32)],
        compiler_params=_cparams(("arbitrary",)),
        name="rglru_scan_rev" if reverse else "rglru_scan_fwd",
    )(u, u, u, conv_w, conv_b, wg, bg, lam, h0, *others)


def _lru_out_tm_kernel(y_ref, h_ref, w_ref, b_ref, res_ref, gate_ref, o_ref):
    nb, tt, lw = y_ref.shape
    y = jnp.swapaxes(y_ref[...].astype(F32), 0, 1)
    a = (_gelu_tanh(y) * h_ref[...]).reshape(tt * nb, lw).astype(BF16)
    z = jnp.dot(a, w_ref[...], preferred_element_type=F32) + b_ref[...]
    z = jnp.swapaxes(z.reshape(tt, nb, z.shape[1]), 0, 1)
    o_ref[...] = res_ref[...] + gate_ref[...] * z


def _lru_out_tm_proj(y, h, w, b, res3, mod3, gate_chunk, tt):
    nb, t, lw = y.shape
    d = w.shape[1]
    tm_spec = pl.BlockSpec((tt, nb, lw), lambda i: (i, 0, 0))
    return pl.pallas_call(
        _lru_out_tm_kernel,
        grid=(t // tt,),
        in_specs=[
            pl.BlockSpec((nb, tt, lw), lambda i: (0, i, 0)),
            tm_spec,
            pl.BlockSpec((lw, d), lambda i: (0, 0)),
            pl.BlockSpec((1, d), lambda i: (0, 0)),
            pl.BlockSpec((nb, tt, d), lambda i: (0, i, 0)),
            pl.BlockSpec((nb, 1, d), lambda i: (0, 0, gate_chunk)),
        ],
        out_specs=pl.BlockSpec((nb, tt, d), lambda i: (0, i, 0)),
        out_shape=jax.ShapeDtypeStruct((nb, t, d), F32),
        compiler_params=_cparams(("parallel",)),
        name="rglru_out_proj",
    )(y, h, w, b.reshape(1, d), res3, mod3)


def _lru_states(ul, uc, conv_w, conv_b, w_gate, b_gate, lam, tt):
    wg = w_gate.astype(BF16)
    _, nb, w = ul.shape
    zero = jnp.zeros((nb, w), F32)
    h = None
    for dirn in range(2):
        rev = dirn == 1
        args = (conv_w, conv_b, wg[dirn], b_gate[dirn], lam[dirn], rev, tt)
        _, h_end = _lru_tm_scan(uc, zero, *args)
        h, _ = _lru_tm_scan(ul, h_end, *args, other=h)
    return h


TM = 512
TD = 1024
BM = 512
TT = 128


def kernel(x, c, ctx, c_ctx, ada_w, ada_b, norm_g, na_w_qkv, na_b_qkv, na_rpb, na_w_o, na_b_o, lru_w_in, lru_b_in, lru_conv_w, lru_conv_b, lru_w_gate, lru_b_gate, lru_lambda, lru_w_o, lru_b_o, moe_w_group, moe_b_group, moe_w_expert, moe_b_expert, moe_w1, moe_w3, moe_w2, final_g):
    b, l, d = x.shape
    n_ctx = ctx.shape[1]
    n_l, n_c = b * l, b * n_ctx
    assert l % TM == 0 and l % TD == 0 and n_ctx <= TM and TM % n_ctx == 0 and b + 1 <= MOD_ROWS
    ctx_row = b
    tiles_per_batch = l // TM
    lat_row = lambda i: i // tiles_per_batch
    dense_row = lambda i: i // (l // TD)
    ctx_tile_row = lambda i: ctx_row
    nl_tiles = n_l // TM
    both_row = lambda i: jnp.where(i < nl_tiles, i // tiles_per_batch, ctx_row)

    cvec = jnp.concatenate([c, c_ctx[None], jnp.zeros((MOD_ROWS - b - 1, d), F32)], axis=0)
    mod = _modulation(cvec, ada_w, ada_b)
    mod3 = [mod[i].reshape(MOD_ROWS, 1, N_MOD * d) for i in range(mod.shape[0])]
    xl = x.reshape(n_l, d)
    xc = ctx.reshape(n_c, d)

    w_qkv = na_w_qkv[0].astype(BF16)
    qk_scale = (d // NA_HEADS) ** -0.5 * LOG2_E
    q, k, v = _proj(xl, norm_g[0, 0], mod3[0], dense_row, 0, 1, w_qkv, na_b_qkv[0],
                    (d, d, d), (BF16,) * 3, (qk_scale, 1.0, 1.0), TD)
    qc, kc, vc = _proj(xc, norm_g[0, 0], mod3[0], ctx_tile_row, 0, 1, w_qkv, na_b_qkv[0],
                       (d, d, d), (BF16,) * 3, (qk_scale, 1.0, 1.0), n_ctx)
    to3 = lambda a, s: a.reshape(b, s, d)
    mb = _window_bias(na_rpb[0], l // GRID_W)
    o_l = _na_attention(to3(q, l), to3(k, l), to3(v, l), to3(kc, n_ctx), to3(vc, n_ctx), mb)
    o_c = _ctx_attention(to3(qc, n_ctx), to3(kc, n_ctx), to3(vc, n_ctx))
    w_o = na_w_o[0].astype(BF16)
    xl = _resid_proj(o_l.reshape(n_l, d), w_o, na_b_o[0], xl, mod3[0], dense_row, 2, TD)
    xc = _resid_proj(o_c.reshape(n_c, d), w_o, na_b_o[0], xc, mod3[0], ctx_tile_row, 2, n_ctx)
    xl, xc = _moe_layer([xl, xc], norm_g[0, 1], mod3[0], both_row, (3, 4, 5),
                        (moe_w_group[0], moe_b_group[0], moe_w_expert[0], moe_b_expert[0]),
                        0, moe_w1, moe_w3, moe_w2, None, TM, BM)

    w_in = lru_w_in[0].astype(BF16)
    lw = w_in.shape[1] // 2
    xl3 = xl.reshape(b, l, d)
    y_l, u_l = _lru_in_proj(xl3, norm_g[1, 0], mod3[1], (0, b), w_in, lru_b_in[0], True, TT)
    (u_c,) = _lru_in_proj(xc.reshape(b, n_ctx, d), norm_g[1, 0], mod3[1], (ctx_row, 1), w_in[:, lw:],
                          lru_b_in[0, lw:], False, TT)
    h_sum = _lru_states(u_l, u_c, lru_conv_w[0], lru_conv_b[0], lru_w_gate[0], lru_b_gate[0],
                        lru_lambda[0], TT)
    xl = _lru_out_tm_proj(y_l, h_sum, lru_w_o[0].astype(BF16), lru_b_o[0], xl3, mod3[1], 2, TT)
    xl = xl.reshape(n_l, d)
    (out,) = _moe_layer([xl], norm_g[1, 1], mod3[1], lat_row, (3, 4, 5),
                        (moe_w_group[1], moe_b_group[1], moe_w_expert[1], moe_b_expert[1]),
                        1, moe_w1, moe_w3, moe_w2, final_g, TM, BM)
    return out.reshape(b, l, d)
```

```python
import functools

import jax
import jax.numpy as jnp
import numpy as np
from jax import lax
from jax.experimental import pallas as pl
from jax.experimental.pallas import tpu as pltpu

F32 = jnp.float32
BF16 = jnp.bfloat16

GRID_W = 64
N_MOD = 6
NA_HEADS = 16
WIN_H = 8
WIN_W = 16
LRU_BLOCKS = 4
CONV_W = 4
LRU_C = 8.0
N_GROUPS = 4
EXPERTS_PER_GROUP = 8
N_EXPERTS = N_GROUPS * EXPERTS_PER_GROUP
EPS = 1e-6

LANES = 128
MOD_ROWS = 16
NEG = -1e30
VMEM_LIMIT = 56 * 1024 * 1024
HIGHEST = lax.Precision.HIGHEST
LOG2_E = 1.4426950408889634


def _cparams(sem, vmem=VMEM_LIMIT):
    return pltpu.CompilerParams(dimension_semantics=sem, vmem_limit_bytes=vmem)


def _mod_kernel(c_ref, w_ref, b_ref, o_ref):
    c = c_ref[...]
    s = c * jax.nn.sigmoid(c)
    o_ref[0] = jnp.dot(s, w_ref[0], precision=HIGHEST, preferred_element_type=F32) + b_ref[0]


def _modulation(cvec, ada_w, ada_b):
    depth, d, n = ada_w.shape
    tn = 1536
    return pl.pallas_call(
        _mod_kernel,
        grid=(depth, n // tn),
        in_specs=[
            pl.BlockSpec((MOD_ROWS, d), lambda l, j: (0, 0)),
            pl.BlockSpec((1, d, tn), lambda l, j: (l, 0, j)),
            pl.BlockSpec((1, 1, tn), lambda l, j: (l, 0, j)),
        ],
        out_specs=pl.BlockSpec((1, MOD_ROWS, tn), lambda l, j: (l, 0, j)),
        out_shape=jax.ShapeDtypeStruct((depth, MOD_ROWS, n), F32),
        compiler_params=_cparams(("arbitrary", "arbitrary")),
        name="adaln_mod",
    )(cvec, ada_w, ada_b.reshape(depth, 1, n))


def _norm_mod(x, g, sh, sc):
    ms = jnp.mean(x * x, axis=-1, keepdims=True)
    y = x * lax.rsqrt(ms + EPS) * g
    return y * (1.0 + sc) + sh


def _proj_kernel(x_ref, g_ref, sh_ref, sc_ref, w_ref, b_ref, *o_refs, splits, scales):
    h = _norm_mod(x_ref[...], g_ref[...], sh_ref[0], sc_ref[0]).astype(BF16)
    off = 0
    for o_ref, n, s in zip(o_refs, splits, scales):
        y = jnp.dot(h, w_ref[:, off:off + n], preferred_element_type=F32) + b_ref[:, off:off + n]
        if s != 1.0:
            y = y * s
        o_ref[...] = y.astype(o_ref.dtype)
        off += n


def _proj(x2d, g, mod3, row_of_tile, sh_chunk, sc_chunk, w, b, splits, dtypes, scales, tm):
    n_tok, d = x2d.shape
    n_out = w.shape[1]
    assert sum(splits) == n_out and n_tok % tm == 0
    return pl.pallas_call(
        functools.partial(_proj_kernel, splits=tuple(splits), scales=tuple(scales)),
        grid=(n_tok // tm,),
        in_specs=[
            pl.BlockSpec((tm, d), lambda i: (i, 0)),
            pl.BlockSpec((1, d), lambda i: (0, 0)),
            pl.BlockSpec((1, 1, d), lambda i: (row_of_tile(i), 0, sh_chunk)),
            pl.BlockSpec((1, 1, d), lambda i: (row_of_tile(i), 0, sc_chunk)),
            pl.BlockSpec((d, n_out), lambda i: (0, 0)),
            pl.BlockSpec((1, n_out), lambda i: (0, 0)),
        ],
        out_specs=[pl.BlockSpec((tm, n), lambda i: (i, 0)) for n in splits],
        out_shape=[jax.ShapeDtypeStruct((n_tok, n), dt) for n, dt in zip(splits, dtypes)],
        compiler_params=_cparams(("parallel",)),
        name="norm_mod_proj",
    )(x2d, g.reshape(1, d), mod3, mod3, w, b.reshape(1, n_out))


def _resid_kernel(a_ref, w_ref, b_ref, res_ref, gate_ref, o_ref):
    y = jnp.dot(a_ref[...], w_ref[...], preferred_element_type=F32) + b_ref[...]
    o_ref[...] = res_ref[...] + gate_ref[0] * y


def _resid_proj(a, w, b, res, mod3, row_of_tile, gate_chunk, tm):
    n_tok, k = a.shape
    d = w.shape[1]
    return pl.pallas_call(
        _resid_kernel,
        grid=(n_tok // tm,),
        in_specs=[
            pl.BlockSpec((tm, k), lambda i: (i, 0)),
            pl.BlockSpec((k, d), lambda i: (0, 0)),
            pl.BlockSpec((1, d), lambda i: (0, 0)),
            pl.BlockSpec((tm, d), lambda i: (i, 0)),
            pl.BlockSpec((1, 1, d), lambda i: (row_of_tile(i), 0, gate_chunk)),
        ],
        out_specs=pl.BlockSpec((tm, d), lambda i: (i, 0)),
        out_shape=jax.ShapeDtypeStruct((n_tok, d), F32),
        compiler_params=_cparams(("parallel",)),
        name="proj_residual",
    )(a, w, b.reshape(1, d), res, mod3)


HEAD_GROUP = 4


def _head_group_attention(qg, keys, vals, biases):
    n_q, width = qg.shape
    dh = width // HEAD_GROUP
    head_of_lane = lax.broadcasted_iota(jnp.int32, (n_q, width), 1) // dh
    qs = jnp.concatenate([jnp.where(head_of_lane == h, qg, jnp.zeros_like(qg)) for h in range(HEAD_GROUP)],
                         axis=0)
    s_parts = []
    for kk, bb in zip(keys, biases):
        s = lax.dot_general(qs, kk, (((1,), (1,)), ((), ())), preferred_element_type=F32)
        s_parts.append(s if bb is None else s + bb)
    m = s_parts[0].max(axis=-1, keepdims=True)
    for s in s_parts[1:]:
        m = jnp.maximum(m, s.max(axis=-1, keepdims=True))
    p_parts = [jnp.exp2(s - m) for s in s_parts]
    l = p_parts[0].sum(axis=-1, keepdims=True)
    for p in p_parts[1:]:
        l = l + p.sum(axis=-1, keepdims=True)
    o = None
    for p, vv in zip(p_parts, vals):
        t = jnp.dot(p.astype(BF16), vv, preferred_element_type=F32)
        o = t if o is None else o + t
    o = o / l
    out = jnp.where(head_of_lane == 0, o[0:n_q], 0.0)
    for h in range(1, HEAD_GROUP):
        out = jnp.where(head_of_lane == h, o[h * n_q:(h + 1) * n_q], out)
    return out


NA_ROWS_PER_STEP = 4


def _na_kernel(q_ref, k_ref, v_ref, kc_ref, vc_ref, *rest, rows, kh):
    mb_refs, o_ref = rest[:-1], rest[-1]
    rps = len(mb_refs)
    n_win = kh * GRID_W
    width = HEAD_GROUP * (q_ref.shape[2] // NA_HEADS)

    for j, mb_ref in enumerate(mb_refs):
        r = pl.program_id(1) * rps + j
        rs = jnp.clip(r - kh // 2, 0, rows - kh)
        k0 = pl.multiple_of(rs * GRID_W, GRID_W)
        q0 = j * GRID_W
        for g in range(NA_HEADS // HEAD_GROUP):
            c0 = g * width
            qg = q_ref[0, q0:q0 + GRID_W, c0:c0 + width]
            kw = k_ref[0, pl.ds(k0, n_win), c0:c0 + width]
            vw = v_ref[0, pl.ds(k0, n_win), c0:c0 + width]
            kc = kc_ref[0, :, c0:c0 + width]
            vc = vc_ref[0, :, c0:c0 + width]
            bias = mb_ref[0, g * HEAD_GROUP:(g + 1) * HEAD_GROUP].reshape(HEAD_GROUP * GRID_W, n_win)
            o = _head_group_attention(qg, [kw, kc], [vw, vc], [bias, None])
            o_ref[0, q0:q0 + GRID_W, c0:c0 + width] = o.astype(o_ref.dtype)


def _window_bias(rpb, rows):
    kh = min(WIN_H, rows)
    cols = np.arange(GRID_W)
    col_start = np.clip(cols - WIN_W // 2, 0, GRID_W - WIN_W)
    ck = np.arange(GRID_W)
    in_win = (ck[None, :] >= col_start[:, None]) & (ck[None, :] < col_start[:, None] + WIN_W)
    dc = ck[None, :] - cols[:, None] + (WIN_W - 1)
    n_dc = 2 * WIN_W - 1
    sel = ((dc[None] == np.arange(n_dc)[:, None, None]) & in_win[None]).astype(np.float32)
    t = jnp.einsum("hrc,cqk->hrqk", rpb.astype(F32), jnp.asarray(sel), precision=HIGHEST)
    t = jnp.where(jnp.asarray(in_win)[None, None], t, NEG)
    per_delta = []
    for delta in range(kh):
        r0 = WIN_H - 1 - delta
        per_delta.append(jnp.transpose(t[:, r0:r0 + kh], (0, 2, 1, 3)).reshape(NA_HEADS, GRID_W, kh * GRID_W))
    return jnp.stack(per_delta) * LOG2_E


def _na_attention(q, k, v, kc, vc, mb):
    b, l, d = q.shape
    c = kc.shape[1]
    rows = l // GRID_W
    kh = mb.shape[0]

    rps = NA_ROWS_PER_STEP
    assert rows % rps == 0

    def bias_spec(j):
        def index(bi, s):
            r = s * rps + j
            return (r - jnp.clip(r - kh // 2, 0, rows - kh), 0, 0, 0)
        return pl.BlockSpec((1, NA_HEADS, GRID_W, kh * GRID_W), index)

    return pl.pallas_call(
        functools.partial(_na_kernel, rows=rows, kh=kh),
        grid=(b, rows // rps),
        in_specs=[
            pl.BlockSpec((1, rps * GRID_W, d), lambda bi, s: (bi, s, 0)),
            pl.BlockSpec((1, l, d), lambda bi, s: (bi, 0, 0)),
            pl.BlockSpec((1, l, d), lambda bi, s: (bi, 0, 0), pipeline_mode=pl.Buffered(1)),
            pl.BlockSpec((1, c, d), lambda bi, s: (bi, 0, 0)),
            pl.BlockSpec((1, c, d), lambda bi, s: (bi, 0, 0)),
        ] + [bias_spec(j) for j in range(rps)],
        out_specs=pl.BlockSpec((1, rps * GRID_W, d), lambda bi, s: (bi, s, 0)),
        out_shape=jax.ShapeDtypeStruct((b, l, d), BF16),
        compiler_params=_cparams(("parallel", "arbitrary")),
        name="na_attention",
    )(q, k, v, kc, vc, *([mb] * rps))


def _ctx_attn_kernel(q_ref, k_ref, v_ref, o_ref):
    width = HEAD_GROUP * (q_ref.shape[2] // NA_HEADS)

    def group(g, carry):
        c0 = pl.multiple_of(g * width, width)
        o = _head_group_attention(q_ref[0, :, pl.ds(c0, width)], [k_ref[0, :, pl.ds(c0, width)]],
                                  [v_ref[0, :, pl.ds(c0, width)]], [None])
        o_ref[0, :, pl.ds(c0, width)] = o.astype(o_ref.dtype)
        return carry

    lax.fori_loop(0, NA_HEADS // HEAD_GROUP, group, 0)


def _ctx_attention(qc, kc, vc):
    b, c, d = qc.shape
    spec = pl.BlockSpec((1, c, d), lambda bi: (bi, 0, 0))
    return pl.pallas_call(
        _ctx_attn_kernel,
        grid=(b,),
        in_specs=[spec, spec, spec],
        out_specs=spec,
        out_shape=jax.ShapeDtypeStruct((b, c, d), BF16),
        compiler_params=_cparams(("parallel",)),
        name="ctx_attention",
    )(qc, kc, vc)


GROUP_LANE0 = N_EXPERTS


def _src_specs(srcs, tm):
    d = srcs[0].shape[1]
    tiles = [s.shape[0] // tm for s in srcs]
    specs, first = [], 0
    for t in tiles:
        specs.append(pl.BlockSpec((tm, d), functools.partial(
            lambda i, first, t: (jnp.clip(i - first, 0, t - 1), 0), first=first, t=t)))
        first += t
    return specs, tiles


def _select_src(i, refs, tiles):
    x = refs[-1][...]
    first = sum(tiles[:-1])
    for ref, t in zip(refs[-2::-1], tiles[-2::-1]):
        x = jnp.where(i < first, ref[...], x)
        first -= t
    return x


TOK_ROWS = 8


def _store_token_major(ref, x, first=0):
    n, d = x.shape
    assert d == TOK_ROWS * LANES
    for j in range(TOK_ROWS):
        ref[pl.ds(first * TOK_ROWS + j, n, stride=TOK_ROWS), :] = x[:, j * LANES:(j + 1) * LANES]


def _load_token_major(ref, n, first=0):
    return jnp.concatenate([ref[pl.ds(first * TOK_ROWS + j, n, stride=TOK_ROWS), :] for j in range(TOK_ROWS)],
                           axis=1)


def _router_kernel(*refs, tiles):
    n_src = len(tiles)
    x_refs = refs[:n_src]
    (g_ref, sh_ref, sc_ref, wr_ref, br_ref, h_ref, rank_ref, eid_ref, wsel_ref, cnt_ref,
     tri_ref, whi_ref, wlo_ref) = refs[n_src:]
    i = pl.program_id(0)

    @pl.when(i == 0)
    def _():
        cnt_ref[...] = jnp.zeros_like(cnt_ref)
        wr = wr_ref[...]
        hi = wr.astype(BF16)
        whi_ref[...] = hi
        wlo_ref[...] = (wr - hi.astype(F32)).astype(BF16)
        n_t = tri_ref.shape[0]
        row = lax.broadcasted_iota(jnp.int32, (n_t, n_t), 0)
        col = lax.broadcasted_iota(jnp.int32, (n_t, n_t), 1)
        tri_ref[...] = jnp.where(row > col, 1.0, 0.0).astype(BF16)

    h = _norm_mod(_select_src(i, x_refs, tiles), g_ref[...], sh_ref[0], sc_ref[0])
    _store_token_major(h_ref, h)
    h_hi = h.astype(BF16)
    h_lo = (h - h_hi.astype(F32)).astype(BF16)
    w_hi = whi_ref[...]
    w_lo = wlo_ref[...]
    logits = (jnp.dot(h_hi, w_hi, preferred_element_type=F32)
              + jnp.dot(h_lo, w_hi, preferred_element_type=F32)
              + jnp.dot(h_hi, w_lo, preferred_element_type=F32)) + br_ref[...]
    tm = logits.shape[0]
    lane = lax.broadcasted_iota(jnp.int32, (tm, LANES), 1)
    big = jnp.int32(LANES)

    gmask = (lane >= GROUP_LANE0) & (lane < GROUP_LANE0 + N_GROUPS)
    gl = jnp.where(gmask, logits, NEG)
    gmax = gl.max(axis=-1, keepdims=True)
    gsel = jnp.where(gmask & (gl == gmax), lane, big).min(axis=-1, keepdims=True) - GROUP_LANE0
    g_w = 1.0 / jnp.where(gmask, jnp.exp(gl - gmax), 0.0).sum(axis=-1, keepdims=True)

    e0 = gsel * EXPERTS_PER_GROUP
    emask = (lane >= e0) & (lane < e0 + EXPERTS_PER_GROUP)
    el = jnp.where(emask, logits, NEG)
    v1 = el.max(axis=-1, keepdims=True)
    i1 = jnp.where(emask & (el == v1), lane, big).min(axis=-1, keepdims=True)
    el2 = jnp.where(lane == i1, NEG, el)
    v2 = el2.max(axis=-1, keepdims=True)
    i2 = jnp.where(emask & (lane != i1) & (el2 == v2), lane, big).min(axis=-1, keepdims=True)
    t = jnp.exp(v2 - v1)
    w1 = g_w / (1.0 + t)
    w2 = g_w * t / (1.0 + t)

    oh = jnp.where((lane == i1) | (lane == i2), 1.0, 0.0)
    before = jnp.dot(tri_ref[...], oh.astype(BF16), preferred_element_type=F32) + cnt_ref[...]
    first_is_low = i1 < i2
    ea = jnp.where(first_is_low, i1, i2)
    eb = jnp.where(first_is_low, i2, i1)
    pick = lambda e: jnp.where(lane == e, before, 0.0).sum(axis=-1, keepdims=True)
    cols = jnp.where(lane == 0, pick(ea), jnp.where(lane == 1, pick(eb), jnp.where(
        lane == 2, ea.astype(F32), jnp.where(lane == 3, eb.astype(F32), 0.0))))
    cols_t = cols.T
    rank_ref[0] = cols_t[0:2, :].astype(jnp.int32)
    eid_ref[0] = cols_t[2:4, :].astype(jnp.int32)
    two = lax.broadcasted_iota(jnp.int32, (tm, 2), 1) == 0
    wsel_ref[...] = jnp.where(two, jnp.where(first_is_low, w1, w2), jnp.where(first_is_low, w2, w1))
    cnt_ref[...] += oh.sum(axis=0, keepdims=True)


def _router(srcs, g, mod3, row_of_tile, sh_chunk, sc_chunk, wr, br, tm):
    d = srcs[0].shape[1]
    src_specs, tiles = _src_specs(srcs, tm)
    n = sum(tiles) * tm
    slots = pl.BlockSpec((1, 2, tm), lambda i: (i, 0, 0))
    slot_shape = jax.ShapeDtypeStruct((n // tm, 2, tm), jnp.int32)
    return pl.pallas_call(
        functools.partial(_router_kernel, tiles=tuple(tiles)),
        grid=(sum(tiles),),
        in_specs=src_specs + [
            pl.BlockSpec((1, d), lambda i: (0, 0)),
            pl.BlockSpec((1, 1, d), lambda i: (row_of_tile(i), 0, sh_chunk)),
            pl.BlockSpec((1, 1, d), lambda i: (row_of_tile(i), 0, sc_chunk)),
            pl.BlockSpec((d, LANES), lambda i: (0, 0)),
            pl.BlockSpec((1, LANES), lambda i: (0, 0)),
        ],
        out_specs=[pl.BlockSpec((tm * TOK_ROWS, LANES), lambda i: (i, 0)), slots, slots,
                   pl.BlockSpec((tm, 2), lambda i: (i, 0)), pl.BlockSpec((1, LANES), lambda i: (0, 0))],
        out_shape=[jax.ShapeDtypeStruct((n * TOK_ROWS, LANES), F32), slot_shape, slot_shape,
                   jax.ShapeDtypeStruct((n, 2), F32), jax.ShapeDtypeStruct((1, LANES), F32)],
        scratch_shapes=[pltpu.VMEM((tm, tm), BF16), pltpu.VMEM((d, LANES), BF16), pltpu.VMEM((d, LANES), BF16)],
        compiler_params=_cparams(("arbitrary",)),
        name="moe_router",
    )(*srcs, g.reshape(1, d), mod3, mod3, wr, br)


def _tok_rows(t):
    return pl.ds(t * TOK_ROWS, TOK_ROWS)


def _wait_tokens(buf, n, sem):
    whole = buf.at[pl.ds(0, n * TOK_ROWS), :]
    pltpu.make_async_copy(whole, whole, sem).wait()


def _scatter_kernel(dest_ref, h_ref, xs_ref, sem):
    tm = h_ref.shape[0] // TOK_ROWS
    for r in range(tm):
        for k in range(2):
            pltpu.make_async_copy(h_ref.at[_tok_rows(r), :], xs_ref.at[dest_ref[0, k, r]], sem).start(priority=k)
    for k in range(2):
        _wait_tokens(h_ref, tm, sem)


def _scatter_rows(dest, h, tm):
    n = h.shape[0] // TOK_ROWS
    return pl.pallas_call(
        _scatter_kernel,
        grid=(n // tm,),
        in_specs=[
            pl.BlockSpec((1, 2, tm), lambda i: (i, 0, 0), memory_space=pltpu.SMEM),
            pl.BlockSpec((tm * TOK_ROWS, LANES), lambda i: (i, 0)),
        ],
        out_specs=pl.BlockSpec(memory_space=pl.ANY),
        out_shape=jax.ShapeDtypeStruct((2 * n, TOK_ROWS, LANES), F32),
        scratch_shapes=[pltpu.SemaphoreType.DMA(())],
        compiler_params=_cparams(("arbitrary",)),
        name="moe_scatter",
    )(dest, h)


def _gmm_kernel(vb_ref, ve_ref, lo_ref, hi_ref, x_ref, w1_ref, w3_ref, w2_ref, o_ref, w1b, w3b, w2b, *, bm):
    v = pl.program_id(0)
    pv = jnp.maximum(v - 1, 0)

    @pl.when((v == 0) | (ve_ref[v] != ve_ref[pv]))
    def _():
        w1b[...] = w1_ref[0, 0].astype(BF16)
        w3b[...] = w3_ref[0, 0].astype(BF16)
        w2b[...] = w2_ref[0, 0].astype(BF16)

    lo = lo_ref[v]
    hi = hi_ref[v]

    first = (v == 0) | (vb_ref[v] != vb_ref[pv])
    half = bm // 2

    def experts_on(r0, n):
        xb = _load_token_major(x_ref, n, r0).astype(BF16)
        a = jnp.dot(xb, w1b[...], preferred_element_type=F32)
        g = jnp.dot(xb, w3b[...], preferred_element_type=F32)
        hdn = (a * jax.nn.sigmoid(a) * g).astype(BF16)
        y = jnp.dot(hdn, w2b[...], preferred_element_type=F32)
        rows = r0 + lax.broadcasted_iota(jnp.int32, (n, 1), 0)
        mine = (rows >= lo) & (rows < hi)

        @pl.when(first)
        def _():
            _store_token_major(o_ref, jnp.where(mine, y, 0.0), r0)

        @pl.when(jnp.logical_not(first))
        def _():
            _store_token_major(o_ref, jnp.where(mine, y, _load_token_major(o_ref, n, r0)), r0)

    def zero_on_first(r0, n):
        @pl.when(first)
        def _():
            _store_token_major(o_ref, jnp.zeros((n, TOK_ROWS * LANES), F32), r0)

    @pl.when((hi > lo) & (lo < half) & (hi > half))
    def _():
        experts_on(0, bm)

    @pl.when((hi > lo) & (hi <= half))
    def _():
        experts_on(0, half)
        zero_on_first(half, half)

    @pl.when((hi > lo) & (lo >= half))
    def _():
        zero_on_first(0, half)
        experts_on(half, half)


def _gmm(visits, xs, layer, w1, w3, w2, bm):
    d, f = w1.shape[2:]
    n_vis = visits[0].shape[0]
    blk = pl.BlockSpec((bm * TOK_ROWS, LANES), lambda v, vb, ve, lo, hi: (vb[v], 0))
    grid_spec = pltpu.PrefetchScalarGridSpec(
        num_scalar_prefetch=4,
        grid=(n_vis,),
        in_specs=[
            blk,
            pl.BlockSpec((1, 1, d, f), lambda v, vb, ve, lo, hi: (layer, ve[v], 0, 0)),
            pl.BlockSpec((1, 1, d, f), lambda v, vb, ve, lo, hi: (layer, ve[v], 0, 0)),
            pl.BlockSpec((1, 1, f, d), lambda v, vb, ve, lo, hi: (layer, ve[v], 0, 0)),
        ],
        out_specs=blk,
        scratch_shapes=[pltpu.VMEM((d, f), BF16), pltpu.VMEM((d, f), BF16), pltpu.VMEM((f, d), BF16)],
    )
    return pl.pallas_call(
        functools.partial(_gmm_kernel, bm=bm),
        grid_spec=grid_spec,
        out_shape=jax.ShapeDtypeStruct(xs.shape, F32),
        compiler_params=_cparams(("arbitrary",)),
        name="moe_experts",
    )(*visits, xs, w1, w3, w2)


def _visit_plan(counts, n_rows, bm):
    counts = counts.astype(jnp.int32)
    end = jnp.cumsum(counts)
    start = end - counts
    n_blocks = n_rows // bm
    n_vis = n_blocks + N_EXPERTS
    tiles = jnp.where(counts > 0, (end - 1) // bm - start // bm + 1, 0)
    vend = jnp.cumsum(tiles)
    vstart = vend - tiles
    v = jnp.arange(n_vis, dtype=jnp.int32)[:, None]
    owns = (vstart[None, :] <= v) & (v < vend[None, :])
    take = lambda a: jnp.sum(jnp.where(owns, a[None, :], 0), axis=1)
    valid = v[:, 0] < vend[-1]
    last_e = jnp.max(jnp.where(counts > 0, jnp.arange(N_EXPERTS, dtype=jnp.int32), 0))
    e = jnp.where(valid, take(jnp.arange(N_EXPERTS, dtype=jnp.int32)), last_e)
    blk = jnp.where(valid, take(start // bm - vstart) + v[:, 0], n_blocks - 1)
    lo = jnp.where(valid, jnp.clip(take(start) - blk * bm, 0, bm), 0)
    hi = jnp.where(valid, jnp.clip(take(end) - blk * bm, 0, bm), 0)
    return start, (blk.astype(jnp.int32), e.astype(jnp.int32), lo.astype(jnp.int32), hi.astype(jnp.int32))


def _combine_kernel(dest_ref, dest_next_ref, ys_ref, *refs, tiles, final_norm):
    n_src = len(tiles)
    x_refs = refs[:n_src]
    w_ref, gate_ref = refs[n_src:n_src + 2]
    rest = refs[n_src + 2:]
    fg_ref = rest[0] if final_norm else None
    o_refs = rest[-n_src - 2:-2]
    buf, sems = rest[-2:]
    i = pl.program_id(0)
    n_steps = pl.num_programs(0)
    tm = x_refs[0].shape[0]

    def gather(d_ref, slot):
        for r in range(tm):
            for k in range(2):
                pltpu.make_async_copy(ys_ref.at[d_ref[0, k, r]], buf.at[slot, k, _tok_rows(r), :],
                                      sems.at[slot]).start(priority=k)

    @pl.when(i == 0)
    def _():
        gather(dest_ref, 0)

    @pl.when(i + 1 < n_steps)
    def _():
        gather(dest_next_ref, (i + 1) % 2)

    slot = i % 2
    for k in range(2):
        _wait_tokens(buf.at[slot, k], tm, sems.at[slot])
    w = w_ref[...]
    moe = (w[:, 0:1] * _load_token_major(buf.at[slot, 0], tm)
           + w[:, 1:2] * _load_token_major(buf.at[slot, 1], tm))
    y = _select_src(i, x_refs, tiles) + gate_ref[0] * moe
    if final_norm:
        y = y * lax.rsqrt(jnp.mean(y * y, axis=-1, keepdims=True) + EPS) * fg_ref[...]

    first = 0
    for o_ref, t in zip(o_refs, tiles):
        @pl.when((i >= first) & (i < first + t))
        def _(o_ref=o_ref):
            o_ref[...] = y
        first += t


def _combine(dest, wsel, ys, srcs, mod3, row_of_tile, gate_chunk, final_g, tm):
    d = srcs[0].shape[1]
    src_specs, tiles = _src_specs(srcs, tm)
    final_norm = final_g is not None
    extra_specs = [pl.BlockSpec((1, d), lambda i: (0, 0))] if final_norm else []
    extra_args = [final_g.reshape(1, d)] if final_norm else []
    n_steps = sum(tiles)
    return pl.pallas_call(
        functools.partial(_combine_kernel, tiles=tuple(tiles), final_norm=final_norm),
        grid=(n_steps,),
        in_specs=[
            pl.BlockSpec((1, 2, tm), lambda i: (i, 0, 0), memory_space=pltpu.SMEM),
            pl.BlockSpec((1, 2, tm), lambda i: (jnp.minimum(i + 1, n_steps - 1), 0, 0), memory_space=pltpu.SMEM),
            pl.BlockSpec(memory_space=pl.ANY),
        ] + src_specs + [
            pl.BlockSpec((tm, 2), lambda i: (i, 0)),
            pl.BlockSpec((1, 1, d), lambda i: (row_of_tile(i), 0, gate_chunk)),
        ] + extra_specs,
        out_specs=list(src_specs),
        out_shape=[jax.ShapeDtypeStruct(s.shape, F32) for s in srcs],
        scratch_shapes=[pltpu.VMEM((2, 2, tm * TOK_ROWS, LANES), F32), pltpu.SemaphoreType.DMA((2,))],
        compiler_params=_cparams(("arbitrary",)),
        name="moe_combine",
    )(dest, dest, ys, *srcs, wsel, mod3, *extra_args)


def _router_weights(w_group, b_group, w_expert, b_expert):
    d = w_group.shape[0]
    we = jnp.transpose(w_expert, (1, 0, 2)).reshape(d, N_EXPERTS)
    wr = jnp.concatenate([we, w_group, jnp.zeros((d, LANES - N_EXPERTS - N_GROUPS), F32)], axis=1)
    br = jnp.concatenate([b_expert.reshape(N_EXPERTS), b_group,
                          jnp.zeros((LANES - N_EXPERTS - N_GROUPS,), F32)]).reshape(1, LANES)
    return wr, br


def _moe_layer(srcs, g, mod3, row_of_tile, chunks, router_w, layer, w1, w3, w2, final_g, tm, bm):
    sh_chunk, sc_chunk, gate_chunk = chunks
    wr, br = _router_weights(*router_w)
    h, rank, eid, wsel, cnt = _router(srcs, g, mod3, row_of_tile, sh_chunk, sc_chunk, wr, br, tm)
    n = wsel.shape[0]
    start, visits = _visit_plan(cnt[0, :N_EXPERTS], 2 * n, bm)
    experts = jnp.arange(N_EXPERTS, dtype=jnp.int32)
    dest = rank + jnp.sum(jnp.where(eid[..., None] == experts, start.astype(jnp.int32), 0), axis=-1)
    xs = _scatter_rows(dest, h, tm)
    ys = _gmm(visits, xs.reshape(2 * n * TOK_ROWS, LANES), layer, w1, w3, w2, bm)
    return _combine(dest, wsel, ys.reshape(2 * n, TOK_ROWS, LANES), srcs, mod3, row_of_tile, gate_chunk,
                    final_g, tm)


def _gelu_tanh(x):
    return x * (0.5 * (1.0 + jnp.tanh(0.7978845608028654 * (x + 0.044715 * (x * x * x)))))


def _lru_in_kernel(x_ref, g_ref, sh_ref, sc_ref, w_ref, b_ref, *o_refs, with_y):
    nb, tt, d = x_ref.shape
    h = _norm_mod(x_ref[...], g_ref[...], sh_ref[...], sc_ref[...]).reshape(nb * tt, d).astype(BF16)
    lw = o_refs[-1].shape[2]
    if with_y:
        y = jnp.dot(h, w_ref[:, 0:lw], preferred_element_type=F32) + b_ref[:, 0:lw]
        o_refs[0][...] = y.reshape(nb, tt, lw).astype(o_refs[0].dtype)
    off = w_ref.shape[1] - lw
    u = jnp.dot(h, w_ref[:, off:off + lw], preferred_element_type=F32) + b_ref[:, off:off + lw]
    o_refs[-1][...] = jnp.swapaxes(u.reshape(nb, tt, lw), 0, 1)


def _lru_in_proj(x3, g, mod3, mod_rows, w, b, with_y, tt):
    nb, t, d = x3.shape
    lw = w.shape[1] // 2 if with_y else w.shape[1]
    r0, nr = mod_rows
    mod_spec = lambda chunk: pl.BlockSpec((nr, 1, d), lambda i: (r0 // nr, 0, chunk))
    out_specs = [pl.BlockSpec((tt, nb, lw), lambda i: (i, 0, 0))]
    out_shape = [jax.ShapeDtypeStruct((t, nb, lw), F32)]
    if with_y:
        out_specs.insert(0, pl.BlockSpec((nb, tt, lw), lambda i: (0, i, 0)))
        out_shape.insert(0, jax.ShapeDtypeStruct((nb, t, lw), BF16))
    return pl.pallas_call(
        functools.partial(_lru_in_kernel, with_y=with_y),
        grid=(t // tt,),
        in_specs=[
            pl.BlockSpec((nb, tt, d), lambda i: (0, i, 0)),
            pl.BlockSpec((1, d), lambda i: (0, 0)),
            mod_spec(0), mod_spec(1),
            pl.BlockSpec(w.shape, lambda i: (0, 0)),
            pl.BlockSpec((1, w.shape[1]), lambda i: (0, 0)),
        ],
        out_specs=out_specs,
        out_shape=out_shape,
        compiler_params=_cparams(("parallel",)),
        name="rglru_in_proj",
    )(x3, g.reshape(1, d), mod3, mod3, w, b.reshape(1, w.shape[1]))


def _lru_tm_kernel(u_ref, up_ref, un_ref, cw_ref, cb_ref, wg_ref, bg_ref, lam_ref, h0_ref, *rest,
                   tt, n_t, reverse, has_other):
    other_ref = rest[0] if has_other else None
    h_ref, hend_ref, a_s, b_s, carry = rest[-5:]
    i = pl.program_id(0)
    ti = (n_t - 1 - i) if reverse else i
    _, nb, w = u_ref.shape
    cb = w // LRU_BLOCKS
    left = CONV_W // 2
    right = CONV_W - 1 - left

    @pl.when(i == 0)
    def _():
        carry[...] = h0_ref[...]

    prev = jnp.where(ti > 0, up_ref[...], 0.0)
    nxt = jnp.where(ti < n_t - 1, un_ref[...], 0.0)
    for n in range(LRU_BLOCKS):
        c0 = n * cb
        ext = jnp.concatenate([prev[:, :, c0:c0 + cb], u_ref[:, :, c0:c0 + cb], nxt[:, :, c0:c0 + cb]], axis=0)
        uc = cb_ref[:, c0:c0 + cb] + cw_ref[0:1, c0:c0 + cb] * ext[0:tt]
        for kk in range(1, CONV_W):
            uc = uc + cw_ref[kk:kk + 1, c0:c0 + cb] * ext[kk:kk + tt]
        ub = uc.reshape(tt * nb, cb).astype(BF16)
        r = jax.nn.sigmoid(jnp.dot(ub, wg_ref[0, n], preferred_element_type=F32) + bg_ref[0, :, c0:c0 + cb])
        ig = jax.nn.sigmoid(jnp.dot(ub, wg_ref[1, n], preferred_element_type=F32) + bg_ref[1, :, c0:c0 + cb])
        lam = lam_ref[:, c0:c0 + cb]
        rate = (-LRU_C * LOG2_E) * jnp.log1p(jnp.exp(-lam))
        a = jnp.exp2(r * rate)
        z = 1.0 - a * a
        root = jnp.where(z > 0.0, z * lax.rsqrt(z), 0.0)
        b = root * (ig * uc.reshape(tt * nb, cb))
        a_s[:, :, c0:c0 + cb] = a.reshape(tt, nb, cb)
        b_s[:, :, c0:c0 + cb] = b.reshape(tt, nb, cb)

    def step(s, h):
        t = (tt - 1 - s) if reverse else s
        h = a_s[t] * h + b_s[t]
        h_ref[t] = (h + other_ref[t]) if has_other else h
        return h

    h_last = lax.fori_loop(0, tt, step, carry[...], unroll=8)
    carry[...] = h_last

    @pl.when(i == n_t - 1)
    def _():
        hend_ref[...] = h_last


def _lru_tm_scan(u, h0, conv_w, conv_b, wg, bg, lam, reverse, tt, other=None):
    t, nb, w = u.shape
    n_t = t // tt
    left = CONV_W // 2
    right = CONV_W - 1 - left
    assert tt % left == 0 and t % tt == 0
    tmap = (lambda i: n_t - 1 - i) if reverse else (lambda i: i)
    full = lambda a: pl.BlockSpec(a.shape, lambda i: (0,) * a.ndim)
    conv_b = conv_b.reshape(1, w)
    bg = bg.reshape(2, 1, w)
    lam = lam.reshape(1, w)
    tile = pl.BlockSpec((tt, nb, w), lambda i: (tmap(i), 0, 0))
    others = [] if other is None else [other]
    return pl.pallas_call(
        functools.partial(_lru_tm_kernel, tt=tt, n_t=n_t, reverse=reverse, has_other=other is not None),
        grid=(n_t,),
        in_specs=[
            tile,
            pl.BlockSpec((left, nb, w), lambda i: (jnp.maximum(tmap(i) * (tt // left) - 1, 0), 0, 0)),
            pl.BlockSpec((right, nb, w), lambda i: (jnp.minimum((tmap(i) + 1) * (tt // right), t // right - 1), 0, 0)),
            full(conv_w), full(conv_b), full(wg), full(bg), full(lam), full(h0),
        ] + [tile] * len(others),
        out_specs=[tile, full(h0)],
        out_shape=[jax.ShapeDtypeStruct((t, nb, w), F32), jax.ShapeDtypeStruct(h0.shape, F32)],
        scratch_shapes=[pltpu.VMEM((tt, nb, w), F32), pltpu.VMEM((tt, nb, w), F32), pltpu.VMEM(h0.shape, F32)],
        compiler_params=_cparams(("arbitrary",)),
        name="rglru_scan_rev" if reverse else "rglru_scan_fwd",
    )(u, u, u, conv_w, conv_b, wg, bg, lam, h0, *others)


def _lru_out_tm_kernel(y_ref, h_ref, w_ref, b_ref, res_ref, gate_ref, o_ref):
    nb, tt, lw = y_ref.shape
    y = jnp.swapaxes(y_ref[...].astype(F32), 0, 1)
    a = (_gelu_tanh(y) * h_ref[...]).reshape(tt * nb, lw).astype(BF16)
    z = jnp.dot(a, w_ref[...], preferred_element_type=F32) + b_ref[...]
    z = jnp.swapaxes(z.reshape(tt, nb, z.shape[1]), 0, 1)
    o_ref[...] = res_ref[...] + gate_ref[...] * z


def _lru_out_tm_proj(y, h, w, b, res3, mod3, gate_chunk, tt):
    nb, t, lw = y.shape
    d = w.shape[1]
    tm_spec = pl.BlockSpec((tt, nb, lw), lambda i: (i, 0, 0))
    return pl.pallas_call(
        _lru_out_tm_kernel,
        grid=(t // tt,),
        in_specs=[
            pl.BlockSpec((nb, tt, lw), lambda i: (0, i, 0)),
            tm_spec,
            pl.BlockSpec((lw, d), lambda i: (0, 0)),
            pl.BlockSpec((1, d), lambda i: (0, 0)),
            pl.BlockSpec((nb, tt, d), lambda i: (0, i, 0)),
            pl.BlockSpec((nb, 1, d), lambda i: (0, 0, gate_chunk)),
        ],
        out_specs=pl.BlockSpec((nb, tt, d), lambda i: (0, i, 0)),
        out_shape=jax.ShapeDtypeStruct((nb, t, d), F32),
        compiler_params=_cparams(("parallel",)),
        name="rglru_out_proj",
    )(y, h, w, b.reshape(1, d), res3, mod3)


def _lru_states(ul, uc, conv_w, conv_b, w_gate, b_gate, lam, tt):
    wg = w_gate.astype(BF16)
    _, nb, w = ul.shape
    zero = jnp.zeros((nb, w), F32)
    h = None
    for dirn in range(2):
        rev = dirn == 1
        args = (conv_w, conv_b, wg[dirn], b_gate[dirn], lam[dirn], rev, tt)
        _, h_end = _lru_tm_scan(uc, zero, *args)
        h, _ = _lru_tm_scan(ul, h_end, *args, other=h)
    return h


TM = 512
TD = 1024
BM = 512
TT = 128


def kernel(x, c, ctx, c_ctx, ada_w, ada_b, norm_g, na_w_qkv, na_b_qkv, na_rpb, na_w_o, na_b_o, lru_w_in, lru_b_in, lru_conv_w, lru_conv_b, lru_w_gate, lru_b_gate, lru_lambda, lru_w_o, lru_b_o, moe_w_group, moe_b_group, moe_w_expert, moe_b_expert, moe_w1, moe_w3, moe_w2, final_g):
    b, l, d = x.shape
    n_ctx = ctx.shape[1]
    n_l, n_c = b * l, b * n_ctx
    assert l % TM == 0 and l % TD == 0 and n_ctx <= TM and TM % n_ctx == 0 and b + 1 <= MOD_ROWS
    ctx_row = b
    tiles_per_batch = l // TM
    lat_row = lambda i: i // tiles_per_batch
    dense_row = lambda i: i // (l // TD)
    ctx_tile_row = lambda i: ctx_row
    nl_tiles = n_l // TM
    both_row = lambda i: jnp.where(i < nl_tiles, i // tiles_per_batch, ctx_row)

    cvec = jnp.concatenate([c, c_ctx[None], jnp.zeros((MOD_ROWS - b - 1, d), F32)], axis=0)
    mod = _modulation(cvec, ada_w, ada_b)
    mod3 = [mod[i].reshape(MOD_ROWS, 1, N_MOD * d) for i in range(mod.shape[0])]
    xl = x.reshape(n_l, d)
    xc = ctx.reshape(n_c, d)

    w_qkv = na_w_qkv[0].astype(BF16)
    qk_scale = (d // NA_HEADS) ** -0.5 * LOG2_E
    q, k, v = _proj(xl, norm_g[0, 0], mod3[0], dense_row, 0, 1, w_qkv, na_b_qkv[0],
                    (d, d, d), (BF16,) * 3, (qk_scale, 1.0, 1.0), TD)
    qc, kc, vc = _proj(xc, norm_g[0, 0], mod3[0], ctx_tile_row, 0, 1, w_qkv, na_b_qkv[0],
                       (d, d, d), (BF16,) * 3, (qk_scale, 1.0, 1.0), n_ctx)
    to3 = lambda a, s: a.reshape(b, s, d)
    mb = _window_bias(na_rpb[0], l // GRID_W)
    o_l = _na_attention(to3(q, l), to3(k, l), to3(v, l), to3(kc, n_ctx), to3(vc, n_ctx), mb)
    o_c = _ctx_attention(to3(qc, n_ctx), to3(kc, n_ctx), to3(vc, n_ctx))
    w_o = na_w_o[0].astype(BF16)
    xl = _resid_proj(o_l.reshape(n_l, d), w_o, na_b_o[0], xl, mod3[0], dense_row, 2, TD)
    xc = _resid_proj(o_c.reshape(n_c, d), w_o, na_b_o[0], xc, mod3[0], ctx_tile_row, 2, n_ctx)
    xl, xc = _moe_layer([xl, xc], norm_g[0, 1], mod3[0], both_row, (3, 4, 5),
                        (moe_w_group[0], moe_b_group[0], moe_w_expert[0], moe_b_expert[0]),
                        0, moe_w1, moe_w3, moe_w2, None, TM, BM)

    w_in = lru_w_in[0].astype(BF16)
    lw = w_in.shape[1] // 2
    xl3 = xl.reshape(b, l, d)
    y_l, u_l = _lru_in_proj(xl3, norm_g[1, 0], mod3[1], (0, b), w_in, lru_b_in[0], True, TT)
    (u_c,) = _lru_in_proj(xc.reshape(b, n_ctx, d), norm_g[1, 0], mod3[1], (ctx_row, 1), w_in[:, lw:],
                          lru_b_in[0, lw:], False, TT)
    h_sum = _lru_states(u_l, u_c, lru_conv_w[0], lru_conv_b[0], lru_w_gate[0], lru_b_gate[0],
                        lru_lambda[0], TT)
    xl = _lru_out_tm_proj(y_l, h_sum, lru_w_o[0].astype(BF16), lru_b_o[0], xl3, mod3[1], 2, TT)
    xl = xl.reshape(n_l, d)
    (out,) = _moe_layer([xl], norm_g[1, 1], mod3[1], lat_row, (3, 4, 5),
                        (moe_w_group[1], moe_b_group[1], moe_w_expert[1], moe_b_expert[1]),
                        1, moe_w1, moe_w3, moe_w2, final_g, TM, BM)
    return out.reshape(b, l, d)
```
